```python
import jax
import jax.numpy as jnp
from jax import lax
import numpy as np

D_MODEL = 1024
BATCH = 32
SEQ = 256
DEPTH = 1
DEC_BATCH = 8
DEC_SEQ = 2048
PAST_LEN = 512

GRID_W = 64
HEAD_DIM = 64
N_Q_HEADS = 8
N_KV_HEADS = 2
GQA_GROUP = N_Q_HEADS // N_KV_HEADS
ATTN_WIDTH = N_Q_HEADS * HEAD_DIM
KV_WIDTH = N_KV_HEADS * HEAD_DIM
WINDOW = 128
BLOCK = 128
ROPE_BASE = 10000.0
ROPE_PAIRS = HEAD_DIM // 4
RWKV_HEADS = 8
RWKV_HEAD_SIZE = 64
RWKV_WIDTH = RWKV_HEADS * RWKV_HEAD_SIZE
N_DIR = 2
DECAY_RANK = 64
ICLR_RANK = 64
GATE_RANK = 128
RW_SHIFT_COLS = 3 * RWKV_WIDTH + N_DIR * DECAY_RANK + N_DIR * ICLR_RANK + GATE_RANK
W_IN_COLS = ATTN_WIDTH + 2 * KV_WIDTH + 2 * D_MODEL + RW_SHIFT_COLS
IN_SPLIT_IDX = (ATTN_WIDTH, ATTN_WIDTH + KV_WIDTH, ATTN_WIDTH + 2 * KV_WIDTH,
                ATTN_WIDTH + 2 * KV_WIDTH + D_MODEL, ATTN_WIDTH + 2 * KV_WIDTH + 2 * D_MODEL)
RW_SPLIT_IDX = (RWKV_WIDTH, 2 * RWKV_WIDTH, 3 * RWKV_WIDTH,
                3 * RWKV_WIDTH + N_DIR * DECAY_RANK,
                3 * RWKV_WIDTH + N_DIR * (DECAY_RANK + ICLR_RANK))
D_FF = 2816
CONV_W = 3
NORM_EPS = 1e-6
GN_EPS = 64e-5
MASK_VALUE = -1e30

kernel_name = 'hybrid_dit_window_gqa_rwkv7_step'


def rms_norm(x, g):
    xf = x.astype(jnp.float32)
    return xf * lax.rsqrt(jnp.mean(xf * xf, axis=-1, keepdims=True) + NORM_EPS) * g.astype(jnp.float32)


def modulation(cond, w_ada, b_ada):
    m = jax.nn.silu(cond) @ w_ada + b_ada
    return jnp.split(m[:, None, :], 6, axis=-1)


def neighbours(h):
    hp = jnp.pad(h, ((0, 0), (1, 1), (0, 0)))
    return hp[:, :-2], hp[:, 2:]


def axial_rope(x):
    T = x.shape[1]
    rows = T // GRID_W
    row = jnp.repeat(jnp.arange(rows), GRID_W).astype(jnp.float32)
    col = jnp.tile(jnp.arange(GRID_W), rows).astype(jnp.float32)
    freqs = ROPE_BASE ** (-jnp.arange(ROPE_PAIRS, dtype=jnp.float32) / ROPE_PAIRS)

    def rotate(xh, pos):
        ang = (pos[:, None] * freqs)[:, None, :]
        cos, sin = jnp.cos(ang), jnp.sin(ang)
        x1, x2 = jnp.split(xh.astype(jnp.float32), 2, axis=-1)
        return jnp.concatenate([x1 * cos - x2 * sin, x2 * cos + x1 * sin], axis=-1)

    x_row, x_col = jnp.split(x, 2, axis=-1)
    return jnp.concatenate([rotate(x_row, row), rotate(x_col, col)], axis=-1).astype(x.dtype)


def dense_context_attention(q, k, v, sink):
    B, T = q.shape[:2]
    qg = q.reshape(B, T, N_KV_HEADS, GQA_GROUP, HEAD_DIM)
    s = jnp.einsum('btkgd,bskd->bkgts', qg, k).astype(jnp.float32) * (HEAD_DIM ** -0.5)
    sk = jnp.broadcast_to(sink.reshape(N_KV_HEADS, GQA_GROUP, 1, 1).astype(jnp.float32),
                          (B, N_KV_HEADS, GQA_GROUP, T, 1))
    p = jax.nn.softmax(jnp.concatenate([s, sk], axis=-1), axis=-1)[..., :-1]
    o = jnp.einsum('bkgts,bskd->btkgd', p, v)
    return o.reshape(B, T, ATTN_WIDTH)


def banded_latent_attention(q, k, v, k_ctx, v_ctx, sink):
    B, T = q.shape[:2]
    NB = T // BLOCK
    P = k_ctx.shape[1]
    qb = q.reshape(B, NB, BLOCK, N_KV_HEADS, GQA_GROUP, HEAD_DIM)
    pad = ((0, 0), (BLOCK, BLOCK), (0, 0), (0, 0))
    kp = jnp.pad(k, pad).reshape(B, NB + 2, BLOCK, N_KV_HEADS, HEAD_DIM)
    vp = jnp.pad(v, pad).reshape(B, NB + 2, BLOCK, N_KV_HEADS, HEAD_DIM)
    kw = jnp.concatenate([kp[:, :NB], kp[:, 1:NB + 1], kp[:, 2:]], axis=2)
    vw = jnp.concatenate([vp[:, :NB], vp[:, 1:NB + 1], vp[:, 2:]], axis=2)
    blk = jnp.arange(NB)[:, None, None]
    qi = blk * BLOCK + jnp.arange(BLOCK)[None, :, None]
    kj = (blk - 1) * BLOCK + jnp.arange(3 * BLOCK)[None, None, :]
    valid = (jnp.abs(kj - qi) <= WINDOW) & (kj >= 0) & (kj < T)
    scale = HEAD_DIM ** -0.5
    s_lat = jnp.einsum('bnqkgd,bnskd->bnkgqs', qb, kw).astype(jnp.float32) * scale
    s_lat = jnp.where(valid[None, :, None, None], s_lat, MASK_VALUE)
    s_ctx = jnp.einsum('bnqkgd,bskd->bnkgqs', qb, k_ctx).astype(jnp.float32) * scale
    sk = jnp.broadcast_to(sink.reshape(1, 1, N_KV_HEADS, GQA_GROUP, 1, 1).astype(jnp.float32),
                          (B, NB, N_KV_HEADS, GQA_GROUP, BLOCK, 1))
    p = jax.nn.softmax(jnp.concatenate([s_lat, s_ctx, sk], axis=-1), axis=-1)
    p_lat = p[..., :3 * BLOCK]
    p_ctx = p[..., 3 * BLOCK:3 * BLOCK + P]
    o = (jnp.einsum('bnkgqs,bnskd->bnqkgd', p_lat, vw)
         + jnp.einsum('bnkgqs,bskd->bnqkgd', p_ctx, v_ctx))
    return o.reshape(B, T, ATTN_WIDTH)


def rwkv7_bidirectional(zr, state0, p):
    B, T = zr.shape[:2]
    f32 = jnp.float32
    mu = p['rwkv_mu']
    prev, nxt = neighbours(zr)
    zr = zr + mu[0] * (prev - zr) + mu[1] * (nxt - zr)
    r, k, v, zw, za, zg = jnp.split(zr, RW_SPLIT_IDX, axis=-1)
    zw = zw.reshape(B, T, N_DIR, DECAY_RANK)
    za = za.reshape(B, T, N_DIR, ICLR_RANK)
    logit = (p['rwkv_w0'] + jnp.einsum('btdr,drc->btdc', jnp.tanh(zw), p['rwkv_w2'])).astype(f32)
    decay = jnp.exp(-jnp.exp(-jax.nn.softplus(-logit) - 0.5))
    a = jax.nn.sigmoid(p['rwkv_a0'] + jnp.einsum('btdr,drc->btdc', za, p['rwkv_a2']))
    g = jax.nn.sigmoid(zg) @ p['rwkv_g2']

    def heads(t):
        return t.reshape(t.shape[:-1] + (RWKV_HEADS, RWKV_HEAD_SIZE))

    kk = heads(k * p['rwkv_k_k']).astype(f32)
    kk = kk / jnp.maximum(jnp.sqrt(jnp.sum(kk * kk, axis=-1, keepdims=True)), 1e-12)
    k_dir = heads(k[:, :, None, :] * (1.0 + (a - 1.0) * p['rwkv_k_a']))
    b_dir = heads(a) * kk[:, :, None]
    r_h, v_h = heads(r), heads(v)

    def two(t):
        return jnp.broadcast_to(t[:, :, None], (B, T, N_DIR) + t.shape[2:])

    def scan_order(t):
        return jnp.stack([t[:, :, 0], jnp.flip(t[:, :, 1], axis=1)], axis=2)

    xs = tuple(jnp.moveaxis(scan_order(t).astype(f32), 1, 0)
               for t in (heads(decay), k_dir, two(v_h), two(r_h), two(-kk), b_dir))

    def step(S, inp):
        w_t, k_t, v_t, r_t, a_t, b_t = inp
        sa = jnp.einsum('bdhij,bdhj->bdhi', S, a_t)
        S = S * w_t[..., None, :] + sa[..., None] * b_t[..., None, :] + v_t[..., None] * k_t[..., None, :]
        return S, jnp.einsum('bdhij,bdhj->bdhi', S, r_t)

    s_final, ys = lax.scan(step, state0.astype(f32), xs)
    ys = jnp.moveaxis(ys, 0, 1)
    y = ys[:, :, 0] + jnp.flip(ys[:, :, 1], axis=1)
    mean = jnp.mean(y, axis=-1, keepdims=True)
    var = jnp.mean(jnp.square(y - mean), axis=-1, keepdims=True)
    y = ((y - mean) * lax.rsqrt(var + GN_EPS)).reshape(B, T, RWKV_WIDTH) * p['rwkv_ln_g'] + p['rwkv_ln_b']
    bonus = jnp.sum(jnp.sum(r_h[:, :, None] * k_dir * p['rwkv_r_k'], axis=-1, keepdims=True), axis=2) * v_h
    out = (y + bonus.reshape(B, T, RWKV_WIDTH)) * g
    return out, s_final


def conv_ffn(h, w_up, conv_w, conv_b, w_down):
    u = h @ w_up
    prev, nxt = neighbours(u)
    u = prev * conv_w[0] + u * conv_w[1] + nxt * conv_w[2] + conv_b
    val, gate = jnp.split(u, 2, axis=-1)
    return (jax.nn.silu(gate) * val) @ w_down


def trunk_layer(x, cond, p, ctx):
    shift1, scale1, gate1, shift2, scale2, gate2 = modulation(cond, p['w_ada'], p['b_ada'])
    B, T = x.shape[:2]
    h = rms_norm(x, p['g_norm1']) * (1.0 + scale1) + shift1
    z = h @ p['w_in']
    q, k, v, ga, gb, zr = jnp.split(z, IN_SPLIT_IDX, axis=-1)
    q = q.reshape(B, T, N_Q_HEADS, HEAD_DIM)
    k = k.reshape(B, T, N_KV_HEADS, HEAD_DIM)
    v = v.reshape(B, T, N_KV_HEADS, HEAD_DIM)
    if ctx is None:
        attn = dense_context_attention(q, k, v, p['attn_sink'])
        state0 = jnp.zeros((B, N_DIR, RWKV_HEADS, RWKV_HEAD_SIZE, RWKV_HEAD_SIZE), jnp.float32)
    else:
        k_ctx, v_ctx, state0 = ctx
        k = axial_rope(k)
        attn = banded_latent_attention(axial_rope(q), k, v, k_ctx, v_ctx, p['attn_sink'])
    rw, s_final = rwkv7_bidirectional(zr, state0, p)
    merged = jax.nn.sigmoid(ga) * (attn @ p['w_proj_a']) + jax.nn.sigmoid(gb) * (rw @ p['w_proj_b'])
    x = x + gate1 * (merged @ p['w_out'])
    h2 = rms_norm(x, p['g_norm2']) * (1.0 + scale2) + shift2
    x = x + gate2 * conv_ffn(h2, p['w_ffn_up'], p['ffn_conv_w'], p['ffn_conv_b'], p['w_ffn_down'])
    return x, (k, v, s_final)


def setup_inputs(seed: int = 0) -> dict:
    key = jax.random.key(seed)
    ks = iter(jax.random.split(key, 40))
    f32 = jnp.float32

    def nrm(shape, scale):
        return jax.random.normal(next(ks), shape, f32) * scale

    L = DEPTH
    return {
        'x_prompt': nrm((BATCH, SEQ, D_MODEL), 1.0),
        'x_sample': nrm((DEC_BATCH, DEC_SEQ, D_MODEL), 1.0),
        'c': nrm((DEC_BATCH, D_MODEL), 1.0),
        'cache_k': nrm((DEC_BATCH, L, PAST_LEN, N_KV_HEADS, HEAD_DIM), 1.0),
        'cache_v': nrm((DEC_BATCH, L, PAST_LEN, N_KV_HEADS, HEAD_DIM), 1.0),
        'state_rwkv': nrm((DEC_BATCH, L, N_DIR, RWKV_HEADS, RWKV_HEAD_SIZE, RWKV_HEAD_SIZE), 0.3),
        'c_ctx': nrm((D_MODEL,), 1.0),
        'w_ada': nrm((L, D_MODEL, 6 * D_MODEL), 0.5 * D_MODEL ** -0.5),
        'b_ada': nrm((L, 6 * D_MODEL), 0.01),
        'g_norm1': 1.0 + nrm((L, D_MODEL), 0.02),
        'w_in': nrm((L, D_MODEL, W_IN_COLS), D_MODEL ** -0.5),
        'attn_sink': nrm((L, N_Q_HEADS), 0.5),
        'w_proj_a': nrm((L, ATTN_WIDTH, D_MODEL), ATTN_WIDTH ** -0.5),
        'w_proj_b': nrm((L, RWKV_WIDTH, D_MODEL), RWKV_WIDTH ** -0.5),
        'rwkv_mu': jax.random.uniform(next(ks), (L, 2, RW_SHIFT_COLS), f32, 0.0, 0.5),
        'rwkv_w0': -1.0 + nrm((L, N_DIR, RWKV_WIDTH), 0.5),
        'rwkv_w2': nrm((L, N_DIR, DECAY_RANK, RWKV_WIDTH), 0.1),
        'rwkv_a0': nrm((L, N_DIR, RWKV_WIDTH), 0.5),
        'rwkv_a2': nrm((L, N_DIR, ICLR_RANK, RWKV_WIDTH), 0.1),
        'rwkv_k_k': 0.85 + nrm((L, RWKV_WIDTH), 0.1),
        'rwkv_k_a': 1.0 + nrm((L, RWKV_WIDTH), 0.1),
        'rwkv_r_k': nrm((L, RWKV_HEADS, RWKV_HEAD_SIZE), 0.1),
        'rwkv_g2': nrm((L, GATE_RANK, RWKV_WIDTH), GATE_RANK ** -0.5),
        'rwkv_ln_g': 1.0 + nrm((L, RWKV_WIDTH), 0.02),
        'rwkv_ln_b': nrm((L, RWKV_WIDTH), 0.01),
        'w_out': nrm((L, D_MODEL, D_MODEL), D_MODEL ** -0.5),
        'g_norm2': 1.0 + nrm((L, D_MODEL), 0.02),
        'w_ffn_up': nrm((L, D_MODEL, 2 * D_FF), D_MODEL ** -0.5),
        'ffn_conv_w': nrm((L, CONV_W, 2 * D_FF), CONV_W ** -0.5),
        'ffn_conv_b': nrm((L, 2 * D_FF), 0.01),
        'w_ffn_down': nrm((L, D_FF, D_MODEL), D_FF ** -0.5),
        'g_final': 1.0 + nrm((D_MODEL,), 0.02),
    }


def reference(x_prompt, x_sample, c, cache_k, cache_v, state_rwkv, c_ctx, w_ada, b_ada, g_norm1,
              w_in, attn_sink, w_proj_a, w_proj_b, rwkv_mu, rwkv_w0, rwkv_w2, rwkv_a0, rwkv_a2,
              rwkv_k_k, rwkv_k_a, rwkv_r_k, rwkv_g2, rwkv_ln_g, rwkv_ln_b, w_out, g_norm2,
              w_ffn_up, ffn_conv_w, ffn_conv_b, w_ffn_down, g_final):
    hp, hs = x_prompt, x_sample
    new_k, new_v, new_s = [], [], []
    for l in range(DEPTH):
        p = dict(w_ada=w_ada[l], b_ada=b_ada[l], g_norm1=g_norm1[l], w_in=w_in[l],
                 attn_sink=attn_sink[l], w_proj_a=w_proj_a[l], w_proj_b=w_proj_b[l],
                 rwkv_mu=rwkv_mu[l], rwkv_w0=rwkv_w0[l], rwkv_w2=rwkv_w2[l], rwkv_a0=rwkv_a0[l],
                 rwkv_a2=rwkv_a2[l], rwkv_k_k=rwkv_k_k[l], rwkv_k_a=rwkv_k_a[l], rwkv_r_k=rwkv_r_k[l],
                 rwkv_g2=rwkv_g2[l], rwkv_ln_g=rwkv_ln_g[l], rwkv_ln_b=rwkv_ln_b[l], w_out=w_out[l],
                 g_norm2=g_norm2[l], w_ffn_up=w_ffn_up[l], ffn_conv_w=ffn_conv_w[l],
                 ffn_conv_b=ffn_conv_b[l], w_ffn_down=w_ffn_down[l])
        hp, (kc, vc, sc) = trunk_layer(hp, c_ctx[None, :], p, None)
        new_k.append(kc)
        new_v.append(vc)
        new_s.append(sc)
        hs, _ = trunk_layer(hs, c, p, (cache_k[:, l], cache_v[:, l], state_rwkv[:, l]))
    y_prompt = rms_norm(hp, g_final).astype(x_prompt.dtype)
    y_sample = rms_norm(hs, g_final).astype(x_sample.dtype)
    new_cache_k = jnp.stack(new_k, axis=1)
    new_cache_v = jnp.stack(new_v, axis=1)
    new_state_rwkv = jnp.stack(new_s, axis=1)
    return (y_prompt, y_sample, new_cache_k, new_cache_v, new_state_rwkv)
```

```python
import functools

import jax
import jax.numpy as jnp
from jax import lax
from jax.experimental import pallas as pl
from jax.experimental.pallas import tpu as pltpu

F32 = jnp.float32
BF16 = jnp.bfloat16

D_MODEL = 1024
GRID_W = 64
HEAD_DIM = 64
N_Q_HEADS = 8
N_KV_HEADS = 2
GQA_GROUP = N_Q_HEADS // N_KV_HEADS
ATTN_WIDTH = N_Q_HEADS * HEAD_DIM
KV_WIDTH = N_KV_HEADS * HEAD_DIM
WINDOW = 128
BLOCK = 128
ROPE_BASE = 10000.0
ROPE_PAIRS = HEAD_DIM // 4
RWKV_HEADS = 8
RWKV_HEAD_SIZE = 64
RWKV_WIDTH = RWKV_HEADS * RWKV_HEAD_SIZE
N_DIR = 2
DECAY_RANK = 64
ICLR_RANK = 64
GATE_RANK = 128
RW_SHIFT_COLS = 3 * RWKV_WIDTH + N_DIR * DECAY_RANK + N_DIR * ICLR_RANK + GATE_RANK
D_FF = 2816
NORM_EPS = 1e-6
GN_EPS = 64e-5
MASK_VALUE = -1e30

Q_OFF = 0
K_OFF = ATTN_WIDTH
V_OFF = K_OFF + KV_WIDTH
GA_OFF = V_OFF + KV_WIDTH
GB_OFF = GA_OFF + D_MODEL
ZR_OFF = GB_OFF + D_MODEL
W_IN_COLS = ZR_OFF + RW_SHIFT_COLS

ZW_OFF = 3 * RWKV_WIDTH
ZA_OFF = ZW_OFF + N_DIR * DECAY_RANK
ZG_OFF = ZA_OFF + N_DIR * ICLR_RANK

SCAN_CHUNK = 64
NEUMANN_STEPS = 6
SUBLANES = 8
FF_CHUNK = 256
VMEM_LIMIT = 56 * 1024 * 1024

_NT = (((1,), (1,)), ((), ()))
_TN = (((0,), (0,)), ((), ()))


def _params(*sem):
    return pltpu.CompilerParams(dimension_semantics=sem, vmem_limit_bytes=VMEM_LIMIT)


def _const_spec(shape):
    nd = len(shape)
    return pl.BlockSpec(shape, lambda *_: (0,) * nd, pipeline_mode=pl.Buffered(1))


def _dot(a, b):
    return jnp.dot(a, b, preferred_element_type=F32)


def _split_dot(x, m):
    hi = x.astype(BF16)
    lo = (x - hi.astype(F32)).astype(BF16)
    return _dot(hi, m) + _dot(lo, m)


def _sigmoid(x):
    return 1.0 / (1.0 + jnp.exp(-x))


def _mod_kernel(c_ref, w_ref, b_ref, o_ref):
    c = c_ref[...]
    s = (c * _sigmoid(c)).astype(BF16)
    o_ref[...] = _dot(s, w_ref[...].astype(BF16)) + b_ref[...]


def _modulation(cond, w_ada, b_ada):
    rows = cond.shape[0]
    n = w_ada.shape[1]
    tn = D_MODEL
    return pl.pallas_call(
        _mod_kernel,
        grid=(n // tn,),
        in_specs=[pl.BlockSpec((rows, D_MODEL), lambda j: (0, 0)),
                  pl.BlockSpec((D_MODEL, tn), lambda j: (0, j)),
                  pl.BlockSpec((1, tn), lambda j: (0, j))],
        out_specs=pl.BlockSpec((rows, tn), lambda j: (0, j)),
        out_shape=jax.ShapeDtypeStruct((rows, n), F32),
        compiler_params=_params("parallel"),
        name="modulation",
    )(cond, w_ada, b_ada.reshape(1, n))


def _swap16(x):
    w = x.shape[1]
    lane = lax.broadcasted_iota(jnp.int32, x.shape, 1)
    first = (lane % 32) < 16
    return jnp.where(first, pltpu.roll(x, w - 16, 1), pltpu.roll(x, 16, 1))


def _in_proj_kernel(*refs, rope):
    if rope:
        x_ref, mod_ref, g_ref, w_ref, cos_ref, sin_ref, q_ref, k_ref, v_ref, ga_ref, gb_ref, zr_ref = refs
    else:
        x_ref, mod_ref, g_ref, w_ref, q_ref, k_ref, v_ref, ga_ref, gb_ref, zr_ref = refs
    x = x_ref[...]
    m = mod_ref[0]
    ms = jnp.mean(x * x, axis=-1, keepdims=True)
    h = x * lax.rsqrt(ms + NORM_EPS) * g_ref[...]
    h = (h * (1.0 + m[1:2, :]) + m[0:1, :]).astype(BF16)

    q = _dot(h, w_ref[:, Q_OFF:K_OFF])
    k = _dot(h, w_ref[:, K_OFF:V_OFF])
    v = _dot(h, w_ref[:, V_OFF:GA_OFF])
    if rope:
        cos = cos_ref[...]
        sin = sin_ref[...]
        reps = ATTN_WIDTH // cos.shape[1]
        cos_q = jnp.concatenate([cos] * reps, axis=1)
        sin_q = jnp.concatenate([sin] * reps, axis=1)
        q = q * cos_q + _swap16(q) * sin_q
        k = k * cos + _swap16(k) * sin
    q_ref[...] = (q * (HEAD_DIM ** -0.5)).astype(q_ref.dtype)
    k_ref[...] = k.astype(k_ref.dtype)
    v_ref[...] = v.astype(v_ref.dtype)
    ga_ref[...] = _sigmoid(_dot(h, w_ref[:, GA_OFF:GB_OFF])).astype(ga_ref.dtype)
    gb_ref[...] = _sigmoid(_dot(h, w_ref[:, GB_OFF:ZR_OFF])).astype(gb_ref.dtype)
    zr_ref[...] = _dot(h, w_ref[:, ZR_OFF:W_IN_COLS])


def _in_proj(x, mod, g1, w_in, seq_len, mod_row, rope_tabs, kv_dtype):
    ntok = x.shape[0]
    tm = min(512, seq_len) if rope_tabs is not None else 512
    tiles_per_seq = max(seq_len // tm, 1)
    rope = rope_tabs is not None
    tok = lambda width: pl.BlockSpec((tm, width), lambda i: (i, 0))
    in_specs = [tok(D_MODEL),
                pl.BlockSpec((1, 6, D_MODEL), lambda i: (mod_row(i), 0, 0)),
                _const_spec((1, D_MODEL)),
                _const_spec((D_MODEL, W_IN_COLS))]
    args = [x, mod, g1, w_in]
    if rope:
        in_specs += [pl.BlockSpec((tm, 2 * HEAD_DIM), lambda i: (i % tiles_per_seq, 0))] * 2
        args += list(rope_tabs)
    out_shape = [jax.ShapeDtypeStruct((ntok, ATTN_WIDTH), BF16),
                 jax.ShapeDtypeStruct((ntok, KV_WIDTH), kv_dtype),
                 jax.ShapeDtypeStruct((ntok, KV_WIDTH), kv_dtype),
                 jax.ShapeDtypeStruct((ntok, D_MODEL), BF16),
                 jax.ShapeDtypeStruct((ntok, D_MODEL), BF16),
                 jax.ShapeDtypeStruct((ntok, RW_SHIFT_COLS), F32)]
    out_specs = [tok(ATTN_WIDTH), tok(KV_WIDTH), tok(KV_WIDTH), tok(D_MODEL), tok(D_MODEL), tok(RW_SHIFT_COLS)]
    return pl.pallas_call(
        functools.partial(_in_proj_kernel, rope=rope),
        grid=(ntok // tm,),
        in_specs=in_specs, out_specs=out_specs, out_shape=out_shape,
        compiler_params=_params("parallel"),
        name="in_proj_rope" if rope else "in_proj",
    )(*args)


def _softmax_pv(s, sink_col, v):
    m = jnp.maximum(jnp.max(s, axis=-1, keepdims=True), sink_col)
    p = jnp.exp(s - m)
    denom = jnp.sum(p, axis=-1, keepdims=True) + jnp.exp(sink_col - m)
    return _dot(p.astype(BF16), v) / denom


def _group_rows(q, g):
    return jnp.concatenate(
        [q[:, (g * GQA_GROUP + i) * HEAD_DIM:(g * GQA_GROUP + i + 1) * HEAD_DIM] for i in range(GQA_GROUP)], axis=0)


def _sink_rows(sink_ref, g, t):
    return jnp.concatenate(
        [jnp.full((t, 1), sink_ref[g * GQA_GROUP + i], F32) for i in range(GQA_GROUP)], axis=0)


def _ungroup(outs, t):
    return jnp.concatenate([o[i * t:(i + 1) * t, :] for o in outs for i in range(GQA_GROUP)], axis=1)


def _attn_ctx_kernel(sink_ref, q_ref, k_ref, v_ref, o_ref):
    q = q_ref[0]
    k = k_ref[0].astype(BF16)
    v = v_ref[0].astype(BF16)
    t = q.shape[0]
    outs = []
    for g in range(N_KV_HEADS):
        kg = k[:, g * HEAD_DIM:(g + 1) * HEAD_DIM]
        vg = v[:, g * HEAD_DIM:(g + 1) * HEAD_DIM]
        s = lax.dot_general(_group_rows(q, g), kg, _NT, preferred_element_type=F32)
        outs.append(_softmax_pv(s, _sink_rows(sink_ref, g, t), vg))
    o_ref[0] = _ungroup(outs, t).astype(o_ref.dtype)


def _attn_ctx(q, k, v, sink):
    b, t, _ = q.shape
    seq = lambda width: pl.BlockSpec((1, t, width), lambda i: (i, 0, 0))
    return pl.pallas_call(
        _attn_ctx_kernel,
        grid=(b,),
        in_specs=[pl.BlockSpec(memory_space=pltpu.SMEM), seq(ATTN_WIDTH), seq(KV_WIDTH), seq(KV_WIDTH)],
        out_specs=seq(ATTN_WIDTH),
        out_shape=jax.ShapeDtypeStruct((b, t, ATTN_WIDTH), BF16),
        compiler_params=_params("parallel"),
        name="attn_ctx",
    )(sink, q, k, v)


def _attn_lat_kernel(sink_ref, q_ref, kp_ref, kc_ref, kn_ref, vp_ref, vc_ref, vn_ref, kx_ref, vx_ref, o_ref):
    n = pl.program_id(1)
    nb = pl.num_programs(1)
    q = q_ref[0]
    k = jnp.concatenate([kp_ref[0], kc_ref[0], kn_ref[0], kx_ref[0].astype(BF16)], axis=0)
    v = jnp.concatenate([vp_ref[0], vc_ref[0], vn_ref[0], vx_ref[0].astype(BF16)], axis=0)
    nkeys = k.shape[0]
    rows = GQA_GROUP * BLOCK
    rq = lax.broadcasted_iota(jnp.int32, (rows, nkeys), 0) % BLOCK
    col = lax.broadcasted_iota(jnp.int32, (rows, nkeys), 1)
    rel = col - BLOCK - rq
    lo = jnp.where(n > 0, 0, BLOCK)
    hi = jnp.where(n < nb - 1, 3 * BLOCK, 2 * BLOCK)
    valid = ((jnp.abs(rel) <= WINDOW) & (col >= lo) & (col < hi)) | (col >= 3 * BLOCK)
    outs = []
    for g in range(N_KV_HEADS):
        kg = k[:, g * HEAD_DIM:(g + 1) * HEAD_DIM]
        vg = v[:, g * HEAD_DIM:(g + 1) * HEAD_DIM]
        s = lax.dot_general(_group_rows(q, g), kg, _NT, preferred_element_type=F32)
        s = jnp.where(valid, s, MASK_VALUE)
        outs.append(_softmax_pv(s, _sink_rows(sink_ref, g, BLOCK), vg))
    o_ref[0] = _ungroup(outs, BLOCK).astype(o_ref.dtype)


def _attn_lat(q, k, v, k_ctx, v_ctx, sink):
    b, t, _ = q.shape
    nb = t // BLOCK
    p = k_ctx.shape[1]
    blk = lambda width, f: pl.BlockSpec((1, BLOCK, width), lambda i, n: (i, f(n), 0))
    prev = lambda n: jnp.maximum(n - 1, 0)
    cur = lambda n: n
    nxt = lambda n: jnp.minimum(n + 1, nb - 1)
    ctx = pl.BlockSpec((1, p, KV_WIDTH), lambda i, n: (i, 0, 0))
    return pl.pallas_call(
        _attn_lat_kernel,
        grid=(b, nb),
        in_specs=[pl.BlockSpec(memory_space=pltpu.SMEM), blk(ATTN_WIDTH, cur),
                  blk(KV_WIDTH, prev), blk(KV_WIDTH, cur), blk(KV_WIDTH, nxt),
                  blk(KV_WIDTH, prev), blk(KV_WIDTH, cur), blk(KV_WIDTH, nxt), ctx, ctx],
        out_specs=blk(ATTN_WIDTH, cur),
        out_shape=jax.ShapeDtypeStruct((b, t, ATTN_WIDTH), BF16),
        compiler_params=_params("parallel", "parallel"),
        name="attn_lat",
    )(sink, q, k, k, k, v, v, v, k_ctx, v_ctx)


def _rw_prep_kernel(z_ref, zp_ref, zn_ref, mu_ref, w2_ref, w0_ref, a2_ref, a0_ref, g2_ref, kk_ref, ka_ref,
                    rk_ref, bd_ref, lw_ref, kd_ref, bb_ref, a_ref, r_ref, v_ref, g_ref, bonus_ref):
    i = pl.program_id(1)
    nt = pl.num_programs(1)
    z = z_ref[0]
    tm = z.shape[0]
    row = lax.broadcasted_iota(jnp.int32, z.shape, 0)
    halo_prev = jnp.where(i > 0, zp_ref[0, SUBLANES - 1:SUBLANES, :], 0.0)
    halo_next = jnp.where(i < nt - 1, zn_ref[0, 0:1, :], 0.0)
    prev = jnp.where(row == 0, halo_prev, pltpu.roll(z, 1, 0))
    nxt = jnp.where(row == tm - 1, halo_next, pltpu.roll(z, tm - 1, 0))
    mu = mu_ref[...]
    z = z + mu[0:1, :] * (prev - z) + mu[1:2, :] * (nxt - z)

    r = z[:, 0:RWKV_WIDTH]
    k = z[:, RWKV_WIDTH:2 * RWKV_WIDTH]
    v = z[:, 2 * RWKV_WIDTH:3 * RWKV_WIDTH]
    zw = z[:, ZW_OFF:ZA_OFF]
    za = z[:, ZA_OFF:ZG_OFF]
    zg = z[:, ZG_OFF:RW_SHIFT_COLS]

    logit = w0_ref[...] + _dot(jnp.tanh(zw).astype(BF16), w2_ref[...])
    lw = -jnp.exp(F32(-0.5)) * _sigmoid(logit)
    a = _sigmoid(a0_ref[...] + _dot(za.astype(BF16), a2_ref[...]))
    g_ref[0] = _dot(_sigmoid(zg).astype(BF16), g2_ref[...])

    bd = bd_ref[...]
    kk = k * kk_ref[...]
    kk = kk * lax.rsqrt(jnp.maximum(_split_dot(kk * kk, bd), 1e-24))
    ka = ka_ref[...]
    ksum = jnp.zeros_like(k)
    for d in range(N_DIR):
        ad = a[:, d * RWKV_WIDTH:(d + 1) * RWKV_WIDTH]
        kd = k * (1.0 + (ad - 1.0) * ka)
        lw_ref[d, 0] = lw[:, d * RWKV_WIDTH:(d + 1) * RWKV_WIDTH]
        kd_ref[d, 0] = kd
        bb_ref[d, 0] = ad * kk
        ksum = ksum + kd
    a_ref[0] = -kk
    r_ref[0] = r
    v_ref[0] = v
    bonus_ref[0] = _split_dot(r * ksum * rk_ref[...], bd) * v


def _rw_prep(zr, p):
    b, t, _ = zr.shape
    tm = 256
    nt = t // tm
    hb = tm // SUBLANES
    last_hb = t // SUBLANES - 1
    tok = lambda: pl.BlockSpec((1, tm, RWKV_WIDTH), lambda i, j: (i, j, 0))
    tok2 = lambda: pl.BlockSpec((N_DIR, 1, tm, RWKV_WIDTH), lambda i, j: (0, i, j, 0))
    one = jax.ShapeDtypeStruct((b, t, RWKV_WIDTH), F32)
    two = jax.ShapeDtypeStruct((N_DIR, b, t, RWKV_WIDTH), F32)
    in_specs = [pl.BlockSpec((1, tm, RW_SHIFT_COLS), lambda i, j: (i, j, 0)),
                pl.BlockSpec((1, SUBLANES, RW_SHIFT_COLS), lambda i, j: (i, jnp.maximum(j * hb - 1, 0), 0)),
                pl.BlockSpec((1, SUBLANES, RW_SHIFT_COLS), lambda i, j: (i, jnp.minimum((j + 1) * hb, last_hb), 0)),
                _const_spec((2, RW_SHIFT_COLS)),
                _const_spec((N_DIR * DECAY_RANK, N_DIR * RWKV_WIDTH)), _const_spec((1, N_DIR * RWKV_WIDTH)),
                _const_spec((N_DIR * ICLR_RANK, N_DIR * RWKV_WIDTH)), _const_spec((1, N_DIR * RWKV_WIDTH)),
                _const_spec((GATE_RANK, RWKV_WIDTH)),
                _const_spec((1, RWKV_WIDTH)), _const_spec((1, RWKV_WIDTH)), _const_spec((1, RWKV_WIDTH)),
                _const_spec((RWKV_WIDTH, RWKV_WIDTH))]
    return pl.pallas_call(
        _rw_prep_kernel,
        grid=(b, nt),
        in_specs=in_specs,
        out_specs=[tok2(), tok2(), tok2(), tok(), tok(), tok(), tok(), tok()],
        out_shape=[two, two, two, one, one, one, one, one],
        compiler_params=_params("parallel", "parallel"),
        name="rw_prep",
    )(zr, zr, zr, p["mu"], p["w2"], p["w0"], p["a2"], p["a0"], p["g2"], p["k_k"], p["k_a"], p["r_k"], p["bd"])


def _rw_scan_kernel(*refs, has_s0, nchunks):
    if has_s0:
        s0_ref, lw_ref, kd_ref, bb_ref, a_ref, r_ref, v_ref, y_ref, sf_ref, s_scr = refs
    else:
        lw_ref, kd_ref, bb_ref, a_ref, r_ref, v_ref, y_ref, sf_ref, s_scr = refs
    d = pl.program_id(1)
    i = pl.program_id(2)
    c = SCAN_CHUNK
    n = RWKV_HEAD_SIZE

    @pl.when(i == 0)
    def _():
        if has_s0:
            s_scr[...] = s0_ref[0, 0]
        else:
            s_scr[...] = jnp.zeros_like(s_scr)

    sgn = 1 - 2 * d
    diff = (lax.broadcasted_iota(jnp.int32, (c, c), 0) - lax.broadcasted_iota(jnp.int32, (c, c), 1)) * sgn
    incl = diff >= 0
    strict = diff > 0
    tri = incl.astype(F32)

    def chunk(jj, carry):
        j = jnp.where(d == 0, jj, nchunks - 1 - jj)
        rows = pl.ds(pl.multiple_of(j * c, c), c)
        lw = lw_ref[0, 0, rows, :]
        kd = kd_ref[0, 0, rows, :]
        bb = bb_ref[0, 0, rows, :]
        a = a_ref[0, rows, :]
        r = r_ref[0, rows, :]
        v = v_ref[0, rows, :]

        cum = jnp.dot(tri, lw, preferred_element_type=F32, precision=lax.Precision.HIGHEST)
        ctot = jnp.sum(lw, axis=0, keepdims=True)
        at_all = a * jnp.exp(cum - lw)
        rt_all = r * jnp.exp(cum)
        en = jnp.exp(-cum)
        bt_all = bb * en
        kt_all = kd * en
        eh = jnp.exp(ctot - cum)
        bh_all = (bb * eh).astype(BF16)
        kh_all = (kd * eh).astype(BF16)
        gam_all = jnp.exp(ctot)
        v_bf = v.astype(BF16)

        ys = []
        for h in range(RWKV_HEADS):
            hs = slice(h * n, (h + 1) * n)
            at, rt = at_all[:, hs], rt_all[:, hs]
            vh = v_bf[:, hs]
            lhs = jnp.concatenate([at, rt], axis=0).astype(BF16)
            rhs = jnp.concatenate([bt_all[:, hs], kt_all[:, hs]], axis=0).astype(BF16)
            amat = lax.dot_general(lhs, rhs, _NT, preferred_element_type=F32)
            a_ab = jnp.where(strict, amat[:c, :c], 0.0)
            a_ak = jnp.where(strict, amat[:c, c:], 0.0).astype(BF16)
            a_rb = jnp.where(incl, amat[c:, :c], 0.0).astype(BF16)
            a_rk = jnp.where(incl, amat[c:, c:], 0.0).astype(BF16)
            z = jnp.concatenate([at, _dot(a_ak, vh)], axis=1)
            x = a_ab
            for step in range(NEUMANN_STEPS):
                xb = x.astype(BF16)
                if step < NEUMANN_STEPS - 1:
                    prod = _dot(xb, jnp.concatenate([z.astype(BF16), xb], axis=1))
                    z = z + prod[:, :2 * n]
                    x = prod[:, 2 * n:]
                else:
                    z = z + _dot(xb, z.astype(BF16))
            w = z[:, :n].astype(BF16)
            u0 = z[:, n:].astype(BF16)
            qm = (rt + _dot(a_rb, w)).astype(BF16)
            y0 = _dot(a_rb, u0) + _dot(a_rk, vh)
            bh = bh_all[:, hs]
            mt = lax.dot_general(w, bh, _TN, preferred_element_type=F32)
            n0t = (lax.dot_general(u0, bh, _TN, preferred_element_type=F32)
                   + lax.dot_general(vh, kh_all[:, hs], _TN, preferred_element_type=F32))
            s = s_scr[h]
            sb = s.astype(BF16)
            ys.append(lax.dot_general(qm, sb, _NT, preferred_element_type=F32) + y0)
            s_scr[h] = s * gam_all[:, hs] + _dot(sb, mt.astype(BF16)) + n0t
        y_ref[0, 0, rows, :] = jnp.concatenate(ys, axis=1)
        return carry

    lax.fori_loop(0, nchunks, chunk, 0)

    @pl.when(i == pl.num_programs(2) - 1)
    def _():
        sf_ref[0, 0] = s_scr[...]


def _rw_scan(s0, lw, kd, bb, a, r, v):
    _, b, t, _ = lw.shape
    tb = min(t, 512)
    nblk = t // tb
    order = lambda d, i: jnp.where(d == 0, i, nblk - 1 - i)
    dir_spec = lambda: pl.BlockSpec((1, 1, tb, RWKV_WIDTH), lambda bi, d, i: (d, bi, order(d, i), 0))
    tok_spec = lambda: pl.BlockSpec((1, tb, RWKV_WIDTH), lambda bi, d, i: (bi, order(d, i), 0))
    st_spec = lambda: pl.BlockSpec((1, 1, RWKV_HEADS, RWKV_HEAD_SIZE, RWKV_HEAD_SIZE),
                                   lambda bi, d, i: (bi, d, 0, 0, 0))
    has_s0 = s0 is not None
    in_specs = [dir_spec(), dir_spec(), dir_spec(), tok_spec(), tok_spec(), tok_spec()]
    args = [lw, kd, bb, a, r, v]
    if has_s0:
        in_specs = [st_spec()] + in_specs
        args = [s0] + args
    return pl.pallas_call(
        functools.partial(_rw_scan_kernel, has_s0=has_s0, nchunks=tb // SCAN_CHUNK),
        grid=(b, N_DIR, nblk),
        in_specs=in_specs,
        out_specs=[dir_spec(), st_spec()],
        out_shape=[jax.ShapeDtypeStruct((N_DIR, b, t, RWKV_WIDTH), F32),
                   jax.ShapeDtypeStruct((b, N_DIR, RWKV_HEADS, RWKV_HEAD_SIZE, RWKV_HEAD_SIZE), F32)],
        scratch_shapes=[pltpu.VMEM((RWKV_HEADS, RWKV_HEAD_SIZE, RWKV_HEAD_SIZE), F32)],
        compiler_params=_params("parallel", "parallel", "arbitrary"),
        name="rw_scan",
    )(*args)


def _post_kernel(x_ref, mod_ref, attn_ref, y_ref, bonus_ref, g_ref, ga_ref, gb_ref, lng_ref, lnb_ref, bd_ref,
                 wa_ref, wb_ref, wo_ref, o_ref):
    m = mod_ref[0]
    y = y_ref[0, 0] + y_ref[1, 0]
    bd = bd_ref[...]
    inv_n = 1.0 / RWKV_HEAD_SIZE
    mean = _split_dot(y, bd) * inv_n
    yc = y - mean
    var = _split_dot(yc * yc, bd) * inv_n
    yn = yc * lax.rsqrt(var + GN_EPS) * lng_ref[...] + lnb_ref[...]
    rw = ((yn + bonus_ref[0]) * g_ref[0]).astype(BF16)
    merged = (ga_ref[0].astype(F32) * _dot(attn_ref[0], wa_ref[...])
              + gb_ref[0].astype(F32) * _dot(rw, wb_ref[...]))
    o_ref[0] = x_ref[0] + m[2:3, :] * _dot(merged.astype(BF16), wo_ref[...])


def _post(x, mod, mod_row, attn, y, bonus, g, ga, gb, p):
    b, t, _ = x.shape
    tm = min(t, 512)
    tok = lambda width: pl.BlockSpec((1, tm, width), lambda i, j: (i, j, 0))
    in_specs = [tok(D_MODEL),
                pl.BlockSpec((1, 6, D_MODEL), lambda i, j: (mod_row(i), 0, 0)),
                tok(ATTN_WIDTH),
                pl.BlockSpec((N_DIR, 1, tm, RWKV_WIDTH), lambda i, j: (0, i, j, 0)),
                tok(RWKV_WIDTH), tok(RWKV_WIDTH), tok(D_MODEL), tok(D_MODEL),
                _const_spec((1, RWKV_WIDTH)), _const_spec((1, RWKV_WIDTH)),
                _const_spec((RWKV_WIDTH, RWKV_WIDTH)),
                _const_spec((ATTN_WIDTH, D_MODEL)), _const_spec((RWKV_WIDTH, D_MODEL)),
                _const_spec((D_MODEL, D_MODEL))]
    return pl.pallas_call(
        _post_kernel,
        grid=(b, t // tm),
        in_specs=in_specs,
        out_specs=tok(D_MODEL),
        out_shape=jax.ShapeDtypeStruct((b, t, D_MODEL), F32),
        compiler_params=_params("parallel", "parallel"),
        name="post",
    )(x, mod, attn, y, bonus, g, ga, gb, p["ln_g"], p["ln_b"], p["bd"], p["w_proj_a"], p["w_proj_b"], p["w_out"])


def _ffn_kernel(x_ref, xp_ref, xn_ref, mod_ref, g2_ref, wup_ref, cw_ref, cb_ref, wdn_ref, gf_ref, o_ref):
    i = pl.program_id(1)
    nt = pl.num_programs(1)
    m = mod_ref[0]
    g2 = g2_ref[...]

    def norm_mod(x):
        ms = jnp.mean(x * x, axis=-1, keepdims=True)
        return x * lax.rsqrt(ms + NORM_EPS) * g2 * (1.0 + m[4:5, :]) + m[3:4, :]

    x = x_ref[0]
    tm = x.shape[0]
    hp = jnp.where(i > 0, norm_mod(xp_ref[0]), 0.0)
    hn = jnp.where(i < nt - 1, norm_mod(xn_ref[0]), 0.0)
    h = jnp.concatenate([hp, norm_mod(x), hn], axis=0).astype(BF16)
    rows = tm + 2 * SUBLANES
    cw = cw_ref[...]
    cb = cb_ref[...]

    def conv(u, lo):
        w = cw[:, lo:lo + FF_CHUNK]
        out = (pltpu.roll(u, 1, 0) * w[0:1, :] + u * w[1:2, :] + pltpu.roll(u, rows - 1, 0) * w[2:3, :]
               + cb[:, lo:lo + FF_CHUNK])
        return out[SUBLANES:SUBLANES + tm, :]

    acc = jnp.zeros((tm, D_MODEL), F32)
    for j in range(D_FF // FF_CHUNK):
        lo = j * FF_CHUNK
        val = conv(_dot(h, wup_ref[:, lo:lo + FF_CHUNK]), lo)
        gate = conv(_dot(h, wup_ref[:, D_FF + lo:D_FF + lo + FF_CHUNK]), D_FF + lo)
        act = (gate * _sigmoid(gate) * val).astype(BF16)
        acc = acc + _dot(act, wdn_ref[lo:lo + FF_CHUNK, :])
    x2 = x + m[5:6, :] * acc
    ms = jnp.mean(x2 * x2, axis=-1, keepdims=True)
    o_ref[0] = x2 * lax.rsqrt(ms + NORM_EPS) * gf_ref[...]


def _ffn(x, mod, mod_row, p):
    b, t, _ = x.shape
    tm = min(t, 512)
    hb = tm // SUBLANES
    last_hb = t // SUBLANES - 1
    in_specs = [pl.BlockSpec((1, tm, D_MODEL), lambda i, j: (i, j, 0)),
                pl.BlockSpec((1, SUBLANES, D_MODEL), lambda i, j: (i, jnp.maximum(j * hb - 1, 0), 0)),
                pl.BlockSpec((1, SUBLANES, D_MODEL), lambda i, j: (i, jnp.minimum((j + 1) * hb, last_hb), 0)),
                pl.BlockSpec((1, 6, D_MODEL), lambda i, j: (mod_row(i), 0, 0)),
                _const_spec((1, D_MODEL)),
                _const_spec((D_MODEL, 2 * D_FF)), _const_spec((3, 2 * D_FF)), _const_spec((1, 2 * D_FF)),
                _const_spec((D_FF, D_MODEL)), _const_spec((1, D_MODEL))]
    return pl.pallas_call(
        _ffn_kernel,
        grid=(b, t // tm),
        in_specs=in_specs,
        out_specs=pl.BlockSpec((1, tm, D_MODEL), lambda i, j: (i, j, 0)),
        out_shape=jax.ShapeDtypeStruct((b, t, D_MODEL), F32),
        compiler_params=_params("parallel", "parallel"),
        name="ffn",
    )(x, x, x, mod, p["g_norm2"], p["w_ffn_up"], p["conv_w"], p["conv_b"], p["w_ffn_down"], p["g_final"])


def _rope_tables(t):
    rows = t // GRID_W
    row = jnp.repeat(jnp.arange(rows), GRID_W).astype(F32)
    col = jnp.tile(jnp.arange(GRID_W), rows).astype(F32)
    freqs = ROPE_BASE ** (-jnp.arange(ROPE_PAIRS, dtype=F32) / ROPE_PAIRS)
    ar = row[:, None] * freqs
    ac = col[:, None] * freqs
    cos = jnp.concatenate([jnp.cos(ar), jnp.cos(ar), jnp.cos(ac), jnp.cos(ac)], axis=1)
    sin = jnp.concatenate([-jnp.sin(ar), jnp.sin(ar), -jnp.sin(ac), jnp.sin(ac)], axis=1)
    return jnp.tile(cos, (1, 2)), jnp.tile(sin, (1, 2))


def _block_diag2(w):
    z = jnp.zeros_like(w[0])
    return jnp.concatenate([jnp.concatenate([w[0], z], axis=1), jnp.concatenate([z, w[1]], axis=1)], axis=0)


def _trunk(x, mod, mod_row_tile, mod_row_seq, p, ctx):
    b, t, _ = x.shape
    rope = _rope_tables(t) if ctx is not None else None
    kv_dtype = BF16 if ctx is not None else F32
    q, k, v, ga, gb, zr = _in_proj(x.reshape(b * t, D_MODEL), mod, p["g_norm1"], p["w_in"], t,
                                   mod_row_tile, rope, kv_dtype)
    seq = lambda arr: arr.reshape(b, t, arr.shape[-1])
    q, k, v, ga, gb, zr = seq(q), seq(k), seq(v), seq(ga), seq(gb), seq(zr)
    if ctx is None:
        attn = _attn_ctx(q, k, v, p["sink"])
        s0 = None
    else:
        k_ctx, v_ctx, s0 = ctx
        attn = _attn_lat(q, k, v, k_ctx, v_ctx, p["sink"])
    lw, kd, bb, a, r, vr, g, bonus = _rw_prep(zr, p)
    y, s_final = _rw_scan(s0, lw, kd, bb, a, r, vr)
    x1 = _post(x, mod, mod_row_seq, attn, y, bonus, g, ga, gb, p)
    out = _ffn(x1, mod, mod_row_seq, p)
    return out, k, v, s_final


def kernel(x_prompt, x_sample, c, cache_k, cache_v, state_rwkv, c_ctx, w_ada, b_ada, g_norm1, w_in, attn_sink, w_proj_a, w_proj_b, rwkv_mu, rwkv_w0, rwkv_w2, rwkv_a0, rwkv_a2, rwkv_k_k, rwkv_k_a, rwkv_r_k, rwkv_g2, rwkv_ln_g, rwkv_ln_b, w_out, g_norm2, w_ffn_up, ffn_conv_w, ffn_conv_b, w_ffn_down, g_final):
    depth = w_ada.shape[0]
    assert depth == 1, "single trunk layer"
    l = 0
    nb, seq, _ = x_prompt.shape
    db, dseq, _ = x_sample.shape
    past = cache_k.shape[2]

    head_id = jnp.arange(RWKV_WIDTH) // RWKV_HEAD_SIZE
    p = dict(
        g_norm1=g_norm1[l].reshape(1, D_MODEL),
        w_in=w_in[l].astype(BF16),
        sink=attn_sink[l],
        mu=rwkv_mu[l],
        w2=_block_diag2(rwkv_w2[l]).astype(BF16),
        w0=rwkv_w0[l].reshape(1, N_DIR * RWKV_WIDTH),
        a2=_block_diag2(rwkv_a2[l]).astype(BF16),
        a0=rwkv_a0[l].reshape(1, N_DIR * RWKV_WIDTH),
        g2=rwkv_g2[l].astype(BF16),
        k_k=rwkv_k_k[l].reshape(1, RWKV_WIDTH),
        k_a=rwkv_k_a[l].reshape(1, RWKV_WIDTH),
        r_k=rwkv_r_k[l].reshape(1, RWKV_WIDTH),
        bd=(head_id[:, None] == head_id[None, :]).astype(BF16),
        ln_g=rwkv_ln_g[l].reshape(1, RWKV_WIDTH),
        ln_b=rwkv_ln_b[l].reshape(1, RWKV_WIDTH),
        w_proj_a=w_proj_a[l].astype(BF16),
        w_proj_b=w_proj_b[l].astype(BF16),
        w_out=w_out[l].astype(BF16),
        g_norm2=g_norm2[l].reshape(1, D_MODEL),
        w_ffn_up=w_ffn_up[l].astype(BF16),
        conv_w=ffn_conv_w[l],
        conv_b=ffn_conv_b[l].reshape(1, 2 * D_FF),
        w_ffn_down=w_ffn_down[l].astype(BF16),
        g_final=g_final.reshape(1, D_MODEL),
    )

    ctx_row = db
    mod_rows = 2 * SUBLANES
    cond = jnp.zeros((mod_rows, D_MODEL), F32).at[:db].set(c).at[ctx_row].set(c_ctx)
    mod = _modulation(cond, w_ada[l], b_ada[l]).reshape(mod_rows, 6, D_MODEL)

    y_prompt, kc, vc, sc = _trunk(x_prompt, mod, lambda i: ctx_row, lambda i: ctx_row, p, None)

    lat_tiles = max(dseq // 512, 1)
    ctx = (cache_k[:, l].reshape(db, past, KV_WIDTH), cache_v[:, l].reshape(db, past, KV_WIDTH), state_rwkv[:, l])
    y_sample, _, _, _ = _trunk(x_sample, mod, lambda i: i // lat_tiles, lambda i: i, p, ctx)

    new_cache_k = kc.reshape(nb, 1, seq, N_KV_HEADS, HEAD_DIM)
    new_cache_v = vc.reshape(nb, 1, seq, N_KV_HEADS, HEAD_DIM)
    new_state = sc.reshape(nb, 1, N_DIR, RWKV_HEADS, RWKV_HEAD_SIZE, RWKV_HEAD_SIZE)
    return (y_prompt, y_sample, new_cache_k, new_cache_v, new_state)
```

```python
import functools

import jax
import jax.numpy as jnp
from jax import lax
from jax.experimental import pallas as pl
from jax.experimental.pallas import tpu as pltpu

F32 = jnp.float32
BF16 = jnp.bfloat16

D_MODEL = 1024
GRID_W = 64
HEAD_DIM = 64
N_Q_HEADS = 8
N_KV_HEADS = 2
GQA_GROUP = N_Q_HEADS // N_KV_HEADS
ATTN_WIDTH = N_Q_HEADS * HEAD_DIM
KV_WIDTH = N_KV_HEADS * HEAD_DIM
WINDOW = 128
BLOCK = 128
ROPE_BASE = 10000.0
ROPE_PAIRS = HEAD_DIM // 4
RWKV_HEADS = 8
RWKV_HEAD_SIZE = 64
RWKV_WIDTH = RWKV_HEADS * RWKV_HEAD_SIZE
N_DIR = 2
DECAY_RANK = 64
ICLR_RANK = 64
GATE_RANK = 128
RW_SHIFT_COLS = 3 * RWKV_WIDTH + N_DIR * DECAY_RANK + N_DIR * ICLR_RANK + GATE_RANK
D_FF = 2816
NORM_EPS = 1e-6
GN_EPS = 64e-5
MASK_VALUE = -1e30

Q_OFF = 0
K_OFF = ATTN_WIDTH
V_OFF = K_OFF + KV_WIDTH
GA_OFF = V_OFF + KV_WIDTH
GB_OFF = GA_OFF + D_MODEL
ZR_OFF = GB_OFF + D_MODEL
W_IN_COLS = ZR_OFF + RW_SHIFT_COLS

ZW_OFF = 3 * RWKV_WIDTH
ZA_OFF = ZW_OFF + N_DIR * DECAY_RANK
ZG_OFF = ZA_OFF + N_DIR * ICLR_RANK

SCAN_CHUNK = 64
NEUMANN_STEPS = 6
SUBLANES = 8
FF_CHUNK = 256
VMEM_LIMIT = 56 * 1024 * 1024

_NT = (((1,), (1,)), ((), ()))
_TN = (((0,), (0,)), ((), ()))


def _params(*sem):
    return pltpu.CompilerParams(dimension_semantics=sem, vmem_limit_bytes=VMEM_LIMIT)


def _const_spec(shape):
    nd = len(shape)
    return pl.BlockSpec(shape, lambda *_: (0,) * nd, pipeline_mode=pl.Buffered(1))


def _dot(a, b):
    return jnp.dot(a, b, preferred_element_type=F32)


def _split_dot(x, m):
    hi = x.astype(BF16)
    lo = (x - hi.astype(F32)).astype(BF16)
    return _dot(hi, m) + _dot(lo, m)


def _sigmoid(x):
    return 1.0 / (1.0 + jnp.exp(-x))


def _mod_kernel(c_ref, w_ref, b_ref, o_ref):
    c = c_ref[...]
    s = (c * _sigmoid(c)).astype(BF16)
    o_ref[...] = _dot(s, w_ref[...].astype(BF16)) + b_ref[...]


def _modulation(cond, w_ada, b_ada):
    rows = cond.shape[0]
    n = w_ada.shape[1]
    tn = D_MODEL
    return pl.pallas_call(
        _mod_kernel,
        grid=(n // tn,),
        in_specs=[pl.BlockSpec((rows, D_MODEL), lambda j: (0, 0)),
                  pl.BlockSpec((D_MODEL, tn), lambda j: (0, j)),
                  pl.BlockSpec((1, tn), lambda j: (0, j))],
        out_specs=pl.BlockSpec((rows, tn), lambda j: (0, j)),
        out_shape=jax.ShapeDtypeStruct((rows, n), F32),
        compiler_params=_params("parallel"),
        name="modulation",
    )(cond, w_ada, b_ada.reshape(1, n))


def _swap16(x):
    w = x.shape[1]
    lane = lax.broadcasted_iota(jnp.int32, x.shape, 1)
    first = (lane % 32) < 16
    return jnp.where(first, pltpu.roll(x, w - 16, 1), pltpu.roll(x, 16, 1))


def _in_proj_kernel(*refs, rope):
    if rope:
        x_ref, mod_ref, g_ref, w_ref, cos_ref, sin_ref, q_ref, k_ref, v_ref, ga_ref, gb_ref, zr_ref = refs
    else:
        x_ref, mod_ref, g_ref, w_ref, q_ref, k_ref, v_ref, ga_ref, gb_ref, zr_ref = refs
    x = x_ref[...]
    m = mod_ref[0]
    ms = jnp.mean(x * x, axis=-1, keepdims=True)
    h = x * lax.rsqrt(ms + NORM_EPS) * g_ref[...]
    h = (h * (1.0 + m[1:2, :]) + m[0:1, :]).astype(BF16)

    q = _dot(h, w_ref[:, Q_OFF:K_OFF])
    k = _dot(h, w_ref[:, K_OFF:V_OFF])
    v = _dot(h, w_ref[:, V_OFF:GA_OFF])
    if rope:
        cos = cos_ref[...]
        sin = sin_ref[...]
        reps = ATTN_WIDTH // cos.shape[1]
        cos_q = jnp.concatenate([cos] * reps, axis=1)
        sin_q = jnp.concatenate([sin] * reps, axis=1)
        q = q * cos_q + _swap16(q) * sin_q
        k = k * cos + _swap16(k) * sin
    q_ref[...] = (q * (HEAD_DIM ** -0.5)).astype(q_ref.dtype)
    k_ref[...] = k.astype(k_ref.dtype)
    v_ref[...] = v.astype(v_ref.dtype)
    ga_ref[...] = _sigmoid(_dot(h, w_ref[:, GA_OFF:GB_OFF])).astype(ga_ref.dtype)
    gb_ref[...] = _sigmoid(_dot(h, w_ref[:, GB_OFF:ZR_OFF])).astype(gb_ref.dtype)
    zr_ref[...] = _dot(h, w_ref[:, ZR_OFF:W_IN_COLS])


def _in_proj(x, mod, g1, w_in, seq_len, mod_row, rope_tabs, kv_dtype):
    ntok = x.shape[0]
    tm = min(512, seq_len) if rope_tabs is not None else 512
    tiles_per_seq = max(seq_len // tm, 1)
    rope = rope_tabs is not None
    tok = lambda width: pl.BlockSpec((tm, width), lambda i: (i, 0))
    in_specs = [tok(D_MODEL),
                pl.BlockSpec((1, 6, D_MODEL), lambda i: (mod_row(i), 0, 0)),
                _const_spec((1, D_MODEL)),
                _const_spec((D_MODEL, W_IN_COLS))]
    args = [x, mod, g1, w_in]
    if rope:
        in_specs += [pl.BlockSpec((tm, 2 * HEAD_DIM), lambda i: (i % tiles_per_seq, 0))] * 2
        args += list(rope_tabs)
    out_shape = [jax.ShapeDtypeStruct((ntok, ATTN_WIDTH), BF16),
                 jax.ShapeDtypeStruct((ntok, KV_WIDTH), kv_dtype),
                 jax.ShapeDtypeStruct((ntok, KV_WIDTH), kv_dtype),
                 jax.ShapeDtypeStruct((ntok, D_MODEL), BF16),
                 jax.ShapeDtypeStruct((ntok, D_MODEL), BF16),
                 jax.ShapeDtypeStruct((ntok, RW_SHIFT_COLS), F32)]
    out_specs = [tok(ATTN_WIDTH), tok(KV_WIDTH), tok(KV_WIDTH), tok(D_MODEL), tok(D_MODEL), tok(RW_SHIFT_COLS)]
    return pl.pallas_call(
        functools.partial(_in_proj_kernel, rope=rope),
        grid=(ntok // tm,),
        in_specs=in_specs, out_specs=out_specs, out_shape=out_shape,
        compiler_params=_params("parallel"),
        name="in_proj_rope" if rope else "in_proj",
    )(*args)


def _softmax_pv(s, sink_col, v):
    m = jnp.maximum(jnp.max(s, axis=-1, keepdims=True), sink_col)
    p = jnp.exp(s - m)
    denom = jnp.sum(p, axis=-1, keepdims=True) + jnp.exp(sink_col - m)
    return _dot(p.astype(BF16), v) / denom


def _group_rows(q, g):
    return jnp.concatenate(
        [q[:, (g * GQA_GROUP + i) * HEAD_DIM:(g * GQA_GROUP + i + 1) * HEAD_DIM] for i in range(GQA_GROUP)], axis=0)


def _sink_rows(sink_ref, g, t):
    return jnp.concatenate(
        [jnp.full((t, 1), sink_ref[g * GQA_GROUP + i], F32) for i in range(GQA_GROUP)], axis=0)


def _ungroup(outs, t):
    return jnp.concatenate([o[i * t:(i + 1) * t, :] for o in outs for i in range(GQA_GROUP)], axis=1)


def _attn_ctx_kernel(sink_ref, q_ref, k_ref, v_ref, o_ref):
    q = q_ref[0]
    k = k_ref[0].astype(BF16)
    v = v_ref[0].astype(BF16)
    t = q.shape[0]
    outs = []
    for g in range(N_KV_HEADS):
        kg = k[:, g * HEAD_DIM:(g + 1) * HEAD_DIM]
        vg = v[:, g * HEAD_DIM:(g + 1) * HEAD_DIM]
        s = lax.dot_general(_group_rows(q, g), kg, _NT, preferred_element_type=F32)
        outs.append(_softmax_pv(s, _sink_rows(sink_ref, g, t), vg))
    o_ref[0] = _ungroup(outs, t).astype(o_ref.dtype)


def _attn_ctx(q, k, v, sink):
    b, t, _ = q.shape
    seq = lambda width: pl.BlockSpec((1, t, width), lambda i: (i, 0, 0))
    return pl.pallas_call(
        _attn_ctx_kernel,
        grid=(b,),
        in_specs=[pl.BlockSpec(memory_space=pltpu.SMEM), seq(ATTN_WIDTH), seq(KV_WIDTH), seq(KV_WIDTH)],
        out_specs=seq(ATTN_WIDTH),
        out_shape=jax.ShapeDtypeStruct((b, t, ATTN_WIDTH), BF16),
        compiler_params=_params("parallel"),
        name="attn_ctx",
    )(sink, q, k, v)


def _attn_lat_kernel(sink_ref, q_ref, kp_ref, kc_ref, kn_ref, vp_ref, vc_ref, vn_ref, kx_ref, vx_ref, o_ref):
    n = pl.program_id(1)
    nb = pl.num_programs(1)
    q = q_ref[0]
    k = jnp.concatenate([kp_ref[0], kc_ref[0], kn_ref[0], kx_ref[0].astype(BF16)], axis=0)
    v = jnp.concatenate([vp_ref[0], vc_ref[0], vn_ref[0], vx_ref[0].astype(BF16)], axis=0)
    nkeys = k.shape[0]
    rows = GQA_GROUP * BLOCK
    rq = lax.broadcasted_iota(jnp.int32, (rows, nkeys), 0) % BLOCK
    col = lax.broadcasted_iota(jnp.int32, (rows, nkeys), 1)
    rel = col - BLOCK - rq
    lo = jnp.where(n > 0, 0, BLOCK)
    hi = jnp.where(n < nb - 1, 3 * BLOCK, 2 * BLOCK)
    valid = ((jnp.abs(rel) <= WINDOW) & (col >= lo) & (col < hi)) | (col >= 3 * BLOCK)
    outs = []
    for g in range(N_KV_HEADS):
        kg = k[:, g * HEAD_DIM:(g + 1) * HEAD_DIM]
        vg = v[:, g * HEAD_DIM:(g + 1) * HEAD_DIM]
        s = lax.dot_general(_group_rows(q, g), kg, _NT, preferred_element_type=F32)
        s = jnp.where(valid, s, MASK_VALUE)
        outs.append(_softmax_pv(s, _sink_rows(sink_ref, g, BLOCK), vg))
    o_ref[0] = _ungroup(outs, BLOCK).astype(o_ref.dtype)


def _attn_lat(q, k, v, k_ctx, v_ctx, sink):
    b, t, _ = q.shape
    nb = t // BLOCK
    p = k_ctx.shape[1]
    blk = lambda width, f: pl.BlockSpec((1, BLOCK, width), lambda i, n: (i, f(n), 0))
    prev = lambda n: jnp.maximum(n - 1, 0)
    cur = lambda n: n
    nxt = lambda n: jnp.minimum(n + 1, nb - 1)
    ctx = pl.BlockSpec((1, p, KV_WIDTH), lambda i, n: (i, 0, 0))
    return pl.pallas_call(
        _attn_lat_kernel,
        grid=(b, nb),
        in_specs=[pl.BlockSpec(memory_space=pltpu.SMEM), blk(ATTN_WIDTH, cur),
                  blk(KV_WIDTH, prev), blk(KV_WIDTH, cur), blk(KV_WIDTH, nxt),
                  blk(KV_WIDTH, prev), blk(KV_WIDTH, cur), blk(KV_WIDTH, nxt), ctx, ctx],
        out_specs=blk(ATTN_WIDTH, cur),
        out_shape=jax.ShapeDtypeStruct((b, t, ATTN_WIDTH), BF16),
        compiler_params=_params("parallel", "parallel"),
        name="attn_lat",
    )(sink, q, k, k, k, v, v, v, k_ctx, v_ctx)


def _rw_prep_kernel(z_ref, zp_ref, zn_ref, mu_ref, w2_ref, w0_ref, a2_ref, a0_ref, g2_ref, kk_ref, ka_ref,
                    rk_ref, bd_ref, lw_ref, kd_ref, bb_ref, a_ref, r_ref, v_ref, g_ref, bonus_ref):
    i = pl.program_id(1)
    nt = pl.num_programs(1)
    z = z_ref[0]
    tm = z.shape[0]
    row = lax.broadcasted_iota(jnp.int32, z.shape, 0)
    halo_prev = jnp.where(i > 0, zp_ref[0, SUBLANES - 1:SUBLANES, :], 0.0)
    halo_next = jnp.where(i < nt - 1, zn_ref[0, 0:1, :], 0.0)
    prev = jnp.where(row == 0, halo_prev, pltpu.roll(z, 1, 0))
    nxt = jnp.where(row == tm - 1, halo_next, pltpu.roll(z, tm - 1, 0))
    mu = mu_ref[...]
    z = z + mu[0:1, :] * (prev - z) + mu[1:2, :] * (nxt - z)

    r = z[:, 0:RWKV_WIDTH]
    k = z[:, RWKV_WIDTH:2 * RWKV_WIDTH]
    v = z[:, 2 * RWKV_WIDTH:3 * RWKV_WIDTH]
    zw = z[:, ZW_OFF:ZA_OFF]
    za = z[:, ZA_OFF:ZG_OFF]
    zg = z[:, ZG_OFF:RW_SHIFT_COLS]

    logit = w0_ref[...] + _dot(jnp.tanh(zw).astype(BF16), w2_ref[...])
    lw = -jnp.exp(F32(-0.5)) * _sigmoid(logit)
    a = _sigmoid(a0_ref[...] + _dot(za.astype(BF16), a2_ref[...]))
    g_ref[0] = _dot(_sigmoid(zg).astype(BF16), g2_ref[...])

    bd = bd_ref[...]
    kk = k * kk_ref[...]
    kk = kk * lax.rsqrt(jnp.maximum(_split_dot(kk * kk, bd), 1e-24))
    ka = ka_ref[...]
    ksum = jnp.zeros_like(k)
    for d in range(N_DIR):
        ad = a[:, d * RWKV_WIDTH:(d + 1) * RWKV_WIDTH]
        kd = k * (1.0 + (ad - 1.0) * ka)
        lw_ref[d, 0] = lw[:, d * RWKV_WIDTH:(d + 1) * RWKV_WIDTH]
        kd_ref[d, 0] = kd
        bb_ref[d, 0] = ad * kk
        ksum = ksum + kd
    a_ref[0] = -kk
    r_ref[0] = r
    v_ref[0] = v
    bonus_ref[0] = _split_dot(r * ksum * rk_ref[...], bd) * v


def _rw_prep(zr, p):
    b, t, _ = zr.shape
    tm = 256
    nt = t // tm
    hb = tm // SUBLANES
    last_hb = t // SUBLANES - 1
    tok = lambda: pl.BlockSpec((1, tm, RWKV_WIDTH), lambda i, j: (i, j, 0))
    tok2 = lambda: pl.BlockSpec((N_DIR, 1, tm, RWKV_WIDTH), lambda i, j: (0, i, j, 0))
    one = jax.ShapeDtypeStruct((b, t, RWKV_WIDTH), F32)
    two = jax.ShapeDtypeStruct((N_DIR, b, t, RWKV_WIDTH), F32)
    in_specs = [pl.BlockSpec((1, tm, RW_SHIFT_COLS), lambda i, j: (i, j, 0)),
                pl.BlockSpec((1, SUBLANES, RW_SHIFT_COLS), lambda i, j: (i, jnp.maximum(j * hb - 1, 0), 0)),
                pl.BlockSpec((1, SUBLANES, RW_SHIFT_COLS), lambda i, j: (i, jnp.minimum((j + 1) * hb, last_hb), 0)),
                _const_spec((2, RW_SHIFT_COLS)),
                _const_spec((N_DIR * DECAY_RANK, N_DIR * RWKV_WIDTH)), _const_spec((1, N_DIR * RWKV_WIDTH)),
                _const_spec((N_DIR * ICLR_RANK, N_DIR * RWKV_WIDTH)), _const_spec((1, N_DIR * RWKV_WIDTH)),
                _const_spec((GATE_RANK, RWKV_WIDTH)),
                _const_spec((1, RWKV_WIDTH)), _const_spec((1, RWKV_WIDTH)), _const_spec((1, RWKV_WIDTH)),
                _const_spec((RWKV_WIDTH, RWKV_WIDTH))]
    return pl.pallas_call(
        _rw_prep_kernel,
        grid=(b, nt),
        in_specs=in_specs,
        out_specs=[tok2(), tok2(), tok2(), tok(), tok(), tok(), tok(), tok()],
        out_shape=[two, two, two, one, one, one, one, one],
        compiler_params=_params("parallel", "parallel"),
        name="rw_prep",
    )(zr, zr, zr, p["mu"], p["w2"], p["w0"], p["a2"], p["a0"], p["g2"], p["k_k"], p["k_a"], p["r_k"], p["bd"])


def _rw_scan_kernel(*refs, has_s0, nchunks):
    if has_s0:
        s0_ref, refs = refs[0], refs[1:]
    (lw0_ref, kd0_ref, bb0_ref, a0_ref, r0_ref, v0_ref, lw1_ref, kd1_ref, bb1_ref, a1_ref, r1_ref, v1_ref,
     y0_ref, y1_ref, sf_ref, s_scr) = refs
    dir_refs = ((lw0_ref, kd0_ref, bb0_ref, a0_ref, r0_ref, v0_ref, y0_ref),
                (lw1_ref, kd1_ref, bb1_ref, a1_ref, r1_ref, v1_ref, y1_ref))
    i = pl.program_id(1)
    c = SCAN_CHUNK
    n = RWKV_HEAD_SIZE

    @pl.when(i == 0)
    def _():
        if has_s0:
            s_scr[...] = s0_ref[0]
        else:
            s_scr[...] = jnp.zeros_like(s_scr)

    diff = lax.broadcasted_iota(jnp.int32, (c, c), 0) - lax.broadcasted_iota(jnp.int32, (c, c), 1)
    incl = (diff >= 0, diff <= 0)
    strict = (diff > 0, diff < 0)
    chains = [(d, h) for d in range(N_DIR) for h in range(RWKV_HEADS)]
    nch = len(chains)
    dirs = [d for d, _ in chains]
    each = lambda f: [f(d, slice(h * n, (h + 1) * n)) for d, h in chains]

    def chunk(jj, carry):
        pre = []
        for d in range(N_DIR):
            lw_ref, kd_ref, bb_ref, a_ref, r_ref, v_ref, _ = dir_refs[d]
            j = jj if d == 0 else nchunks - 1 - jj
            rows = pl.ds(pl.multiple_of(j * c, c), c)
            lw = lw_ref[0, 0, rows, :]
            kd = kd_ref[0, 0, rows, :]
            bb = bb_ref[0, 0, rows, :]
            cum = jnp.dot(incl[d].astype(F32), lw, preferred_element_type=F32, precision=lax.Precision.HIGHEST)
            ctot = jnp.sum(lw, axis=0, keepdims=True)
            en = jnp.exp(-cum)
            eh = jnp.exp(ctot - cum)
            pre.append(dict(
                rows=rows,
                at=a_ref[0, rows, :] * jnp.exp(cum - lw),
                rt=r_ref[0, rows, :] * jnp.exp(cum),
                bt=(bb * en).astype(BF16), kt=(kd * en).astype(BF16),
                bh=(bb * eh).astype(BF16), kh=(kd * eh).astype(BF16),
                gam=jnp.exp(ctot), v=v_ref[0, rows, :].astype(BF16)))

        at = each(lambda d, hs: pre[d]["at"][:, hs])
        rt = each(lambda d, hs: pre[d]["rt"][:, hs])
        vh = each(lambda d, hs: pre[d]["v"][:, hs])
        bh = each(lambda d, hs: pre[d]["bh"][:, hs])
        kh = each(lambda d, hs: pre[d]["kh"][:, hs])
        gam = each(lambda d, hs: pre[d]["gam"][:, hs])
        amat = each(lambda d, hs: lax.dot_general(
            jnp.concatenate([pre[d]["at"][:, hs], pre[d]["rt"][:, hs]], axis=0).astype(BF16),
            jnp.concatenate([pre[d]["bt"][:, hs], pre[d]["kt"][:, hs]], axis=0), _NT, preferred_element_type=F32))
        a_ak = [jnp.where(strict[dirs[q]], amat[q][:c, c:], 0.0).astype(BF16) for q in range(nch)]
        a_rb = [jnp.where(incl[dirs[q]], amat[q][c:, :c], 0.0).astype(BF16) for q in range(nch)]
        a_rk = [jnp.where(incl[dirs[q]], amat[q][c:, c:], 0.0).astype(BF16) for q in range(nch)]
        x = [jnp.where(strict[dirs[q]], amat[q][:c, :c], 0.0) for q in range(nch)]
        z = [jnp.concatenate([at[q], _dot(a_ak[q], vh[q])], axis=1) for q in range(nch)]
        for step in range(NEUMANN_STEPS):
            xb = [x[q].astype(BF16) for q in range(nch)]
            if step < NEUMANN_STEPS - 1:
                prod = [_dot(xb[q], jnp.concatenate([z[q].astype(BF16), xb[q]], axis=1)) for q in range(nch)]
                z = [z[q] + prod[q][:, :2 * n] for q in range(nch)]
                x = [prod[q][:, 2 * n:] for q in range(nch)]
            else:
                z = [z[q] + _dot(xb[q], z[q].astype(BF16)) for q in range(nch)]
        w = [z[q][:, :n].astype(BF16) for q in range(nch)]
        u0 = [z[q][:, n:].astype(BF16) for q in range(nch)]
        qm = [(rt[q] + _dot(a_rb[q], w[q])).astype(BF16) for q in range(nch)]
        y0 = [_dot(a_rb[q], u0[q]) + _dot(a_rk[q], vh[q]) for q in range(nch)]
        mt = [lax.dot_general(w[q], bh[q], _TN, preferred_element_type=F32).astype(BF16) for q in range(nch)]
        n0t = [lax.dot_general(u0[q], bh[q], _TN, preferred_element_type=F32)
               + lax.dot_general(vh[q], kh[q], _TN, preferred_element_type=F32) for q in range(nch)]
        s = [s_scr[d, h] for d, h in chains]
        sb = [s[q].astype(BF16) for q in range(nch)]
        ys = [lax.dot_general(qm[q], sb[q], _NT, preferred_element_type=F32) + y0[q] for q in range(nch)]
        for q, (d, h) in enumerate(chains):
            s_scr[d, h] = s[q] * gam[q] + _dot(sb[q], mt[q]) + n0t[q]
        for d in range(N_DIR):
            y_ref = dir_refs[d][-1]
            y_ref[0, pre[d]["rows"], :] = jnp.concatenate(ys[d * RWKV_HEADS:(d + 1) * RWKV_HEADS], axis=1)
        return carry

    lax.fori_loop(0, nchunks, chunk, 0)

    @pl.when(i == pl.num_programs(1) - 1)
    def _():
        sf_ref[0] = s_scr[...]


def _rw_scan(s0, lw, kd, bb, a, r, v):
    _, b, t, _ = lw.shape
    tb = min(t, 512)
    nblk = t // tb
    blk = (lambda i: i, lambda i: nblk - 1 - i)
    dir_spec = lambda d: pl.BlockSpec((1, 1, tb, RWKV_WIDTH), lambda bi, i: (d, bi, blk[d](i), 0))
    tok_spec = lambda d: pl.BlockSpec((1, tb, RWKV_WIDTH), lambda bi, i: (bi, blk[d](i), 0))
    st_spec = pl.BlockSpec((1, N_DIR, RWKV_HEADS, RWKV_HEAD_SIZE, RWKV_HEAD_SIZE), lambda bi, i: (bi, 0, 0, 0, 0))
    has_s0 = s0 is not None
    in_specs, args = [], []
    if has_s0:
        in_specs.append(st_spec)
        args.append(s0)
    for d in range(N_DIR):
        in_specs += [dir_spec(d), dir_spec(d), dir_spec(d), tok_spec(d), tok_spec(d), tok_spec(d)]
        args += [lw, kd, bb, a, r, v]
    y_shape = jax.ShapeDtypeStruct((b, t, RWKV_WIDTH), F32)
    return pl.pallas_call(
        functools.partial(_rw_scan_kernel, has_s0=has_s0, nchunks=tb // SCAN_CHUNK),
        grid=(b, nblk),
        in_specs=in_specs,
        out_specs=[tok_spec(0), tok_spec(1), st_spec],
        out_shape=[y_shape, y_shape,
                   jax.ShapeDtypeStruct((b, N_DIR, RWKV_HEADS, RWKV_HEAD_SIZE, RWKV_HEAD_SIZE), F32)],
        scratch_shapes=[pltpu.VMEM((N_DIR, RWKV_HEADS, RWKV_HEAD_SIZE, RWKV_HEAD_SIZE), F32)],
        compiler_params=_params("parallel", "arbitrary"),
        name="rw_scan",
    )(*args)


def _post_kernel(x_ref, mod_ref, attn_ref, y0_ref, y1_ref, bonus_ref, g_ref, ga_ref, gb_ref, lng_ref, lnb_ref,
                 bd_ref, wa_ref, wb_ref, wo_ref, o_ref):
    m = mod_ref[0]
    y = y0_ref[0] + y1_ref[0]
    bd = bd_ref[...]
    inv_n = 1.0 / RWKV_HEAD_SIZE
    mean = _split_dot(y, bd) * inv_n
    yc = y - mean
    var = _split_dot(yc * yc, bd) * inv_n
    yn = yc * lax.rsqrt(var + GN_EPS) * lng_ref[...] + lnb_ref[...]
    rw = ((yn + bonus_ref[0]) * g_ref[0]).astype(BF16)
    merged = (ga_ref[0].astype(F32) * _dot(attn_ref[0], wa_ref[...])
              + gb_ref[0].astype(F32) * _dot(rw, wb_ref[...]))
    o_ref[0] = x_ref[0] + m[2:3, :] * _dot(merged.astype(BF16), wo_ref[...])


def _post(x, mod, mod_row, attn, y0, y1, bonus, g, ga, gb, p):
    b, t, _ = x.shape
    tm = min(t, 512)
    tok = lambda width: pl.BlockSpec((1, tm, width), lambda i, j: (i, j, 0))
    in_specs = [tok(D_MODEL),
                pl.BlockSpec((1, 6, D_MODEL), lambda i, j: (mod_row(i), 0, 0)),
                tok(ATTN_WIDTH),
                tok(RWKV_WIDTH), tok(RWKV_WIDTH), tok(RWKV_WIDTH), tok(RWKV_WIDTH), tok(D_MODEL), tok(D_MODEL),
                _const_spec((1, RWKV_WIDTH)), _const_spec((1, RWKV_WIDTH)),
                _const_spec((RWKV_WIDTH, RWKV_WIDTH)),
                _const_spec((ATTN_WIDTH, D_MODEL)), _const_spec((RWKV_WIDTH, D_MODEL)),
                _const_spec((D_MODEL, D_MODEL))]
    return pl.pallas_call(
        _post_kernel,
        grid=(b, t // tm),
        in_specs=in_specs,
        out_specs=tok(D_MODEL),
        out_shape=jax.ShapeDtypeStruct((b, t, D_MODEL), F32),
        compiler_params=_params("parallel", "parallel"),
        name="post",
    )(x, mod, attn, y0, y1, bonus, g, ga, gb, p["ln_g"], p["ln_b"], p["bd"], p["w_proj_a"], p["w_proj_b"], p["w_out"])


def _ffn_kernel(x_ref, xp_ref, xn_ref, mod_ref, g2_ref, wup_ref, cw_ref, cb_ref, wdn_ref, gf_ref, o_ref):
    i = pl.program_id(1)
    nt = pl.num_programs(1)
    m = mod_ref[0]
    g2 = g2_ref[...]

    def norm_mod(x):
        ms = jnp.mean(x * x, axis=-1, keepdims=True)
        return x * lax.rsqrt(ms + NORM_EPS) * g2 * (1.0 + m[4:5, :]) + m[3:4, :]

    x = x_ref[0]
    tm = x.shape[0]
    hp = jnp.where(i > 0, norm_mod(xp_ref[0]), 0.0)
    hn = jnp.where(i < nt - 1, norm_mod(xn_ref[0]), 0.0)
    h = jnp.concatenate([hp, norm_mod(x), hn], axis=0).astype(BF16)
    rows = tm + 2 * SUBLANES
    cw = cw_ref[...]
    cb = cb_ref[...]

    def conv(u, lo):
        w = cw[:, lo:lo + FF_CHUNK]
        out = (pltpu.roll(u, 1, 0) * w[0:1, :] + u * w[1:2, :] + pltpu.roll(u, rows - 1, 0) * w[2:3, :]
               + cb[:, lo:lo + FF_CHUNK])
        return out[SUBLANES:SUBLANES + tm, :]

    acc = jnp.zeros((tm, D_MODEL), F32)
    for j in range(D_FF // FF_CHUNK):
        lo = j * FF_CHUNK
        val = conv(_dot(h, wup_ref[:, lo:lo + FF_CHUNK]), lo)
        gate = conv(_dot(h, wup_ref[:, D_FF + lo:D_FF + lo + FF_CHUNK]), D_FF + lo)
        act = (gate * _sigmoid(gate) * val).astype(BF16)
        acc = acc + _dot(act, wdn_ref[lo:lo + FF_CHUNK, :])
    x2 = x + m[5:6, :] * acc
    ms = jnp.mean(x2 * x2, axis=-1, keepdims=True)
    o_ref[0] = x2 * lax.rsqrt(ms + NORM_EPS) * gf_ref[...]


def _ffn(x, mod, mod_row, p):
    b, t, _ = x.shape
    tm = min(t, 512)
    hb = tm // SUBLANES
    last_hb = t // SUBLANES - 1
    in_specs = [pl.BlockSpec((1, tm, D_MODEL), lambda i, j: (i, j, 0)),
                pl.BlockSpec((1, SUBLANES, D_MODEL), lambda i, j: (i, jnp.maximum(j * hb - 1, 0), 0)),
                pl.BlockSpec((1, SUBLANES, D_MODEL), lambda i, j: (i, jnp.minimum((j + 1) * hb, last_hb), 0)),
                pl.BlockSpec((1, 6, D_MODEL), lambda i, j: (mod_row(i), 0, 0)),
                _const_spec((1, D_MODEL)),
                _const_spec((D_MODEL, 2 * D_FF)), _const_spec((3, 2 * D_FF)), _const_spec((1, 2 * D_FF)),
                _const_spec((D_FF, D_MODEL)), _const_spec((1, D_MODEL))]
    return pl.pallas_call(
        _ffn_kernel,
        grid=(b, t // tm),
        in_specs=in_specs,
        out_specs=pl.BlockSpec((1, tm, D_MODEL), lambda i, j: (i, j, 0)),
        out_shape=jax.ShapeDtypeStruct((b, t, D_MODEL), F32),
        compiler_params=_params("parallel", "parallel"),
        name="ffn",
    )(x, x, x, mod, p["g_norm2"], p["w_ffn_up"], p["conv_w"], p["conv_b"], p["w_ffn_down"], p["g_final"])


def _rope_tables(t):
    rows = t // GRID_W
    row = jnp.repeat(jnp.arange(rows), GRID_W).astype(F32)
    col = jnp.tile(jnp.arange(GRID_W), rows).astype(F32)
    freqs = ROPE_BASE ** (-jnp.arange(ROPE_PAIRS, dtype=F32) / ROPE_PAIRS)
    ar = row[:, None] * freqs
    ac = col[:, None] * freqs
    cos = jnp.concatenate([jnp.cos(ar), jnp.cos(ar), jnp.cos(ac), jnp.cos(ac)], axis=1)
    sin = jnp.concatenate([-jnp.sin(ar), jnp.sin(ar), -jnp.sin(ac), jnp.sin(ac)], axis=1)
    return jnp.tile(cos, (1, 2)), jnp.tile(sin, (1, 2))


def _block_diag2(w):
    z = jnp.zeros_like(w[0])
    return jnp.concatenate([jnp.concatenate([w[0], z], axis=1), jnp.concatenate([z, w[1]], axis=1)], axis=0)


def _trunk(x, mod, mod_row_tile, mod_row_seq, p, ctx):
    b, t, _ = x.shape
    rope = _rope_tables(t) if ctx is not None else None
    kv_dtype = BF16 if ctx is not None else F32
    q, k, v, ga, gb, zr = _in_proj(x.reshape(b * t, D_MODEL), mod, p["g_norm1"], p["w_in"], t,
                                   mod_row_tile, rope, kv_dtype)
    seq = lambda arr: arr.reshape(b, t, arr.shape[-1])
    q, k, v, ga, gb, zr = seq(q), seq(k), seq(v), seq(ga), seq(gb), seq(zr)
    if ctx is None:
        attn = _attn_ctx(q, k, v, p["sink"])
        s0 = None
    else:
        k_ctx, v_ctx, s0 = ctx
        attn = _attn_lat(q, k, v, k_ctx, v_ctx, p["sink"])
    lw, kd, bb, a, r, vr, g, bonus = _rw_prep(zr, p)
    y0, y1, s_final = _rw_scan(s0, lw, kd, bb, a, r, vr)
    x1 = _post(x, mod, mod_row_seq, attn, y0, y1, bonus, g, ga, gb, p)
    out = _ffn(x1, mod, mod_row_seq, p)
    return out, k, v, s_final


def kernel(x_prompt, x_sample, c, cache_k, cache_v, state_rwkv, c_ctx, w_ada, b_ada, g_norm1, w_in, attn_sink, w_proj_a, w_proj_b, rwkv_mu, rwkv_w0, rwkv_w2, rwkv_a0, rwkv_a2, rwkv_k_k, rwkv_k_a, rwkv_r_k, rwkv_g2, rwkv_ln_g, rwkv_ln_b, w_out, g_norm2, w_ffn_up, ffn_conv_w, ffn_conv_b, w_ffn_down, g_final):
    depth = w_ada.shape[0]
    assert depth == 1, "single trunk layer"
    l = 0
    nb, seq, _ = x_prompt.shape
    db, dseq, _ = x_sample.shape
    past = cache_k.shape[2]

    head_id = jnp.arange(RWKV_WIDTH) // RWKV_HEAD_SIZE
    p = dict(
        g_norm1=g_norm1[l].reshape(1, D_MODEL),
        w_in=w_in[l].astype(BF16),
        sink=attn_sink[l],
        mu=rwkv_mu[l],
        w2=_block_diag2(rwkv_w2[l]).astype(BF16),
        w0=rwkv_w0[l].reshape(1, N_DIR * RWKV_WIDTH),
        a2=_block_diag2(rwkv_a2[l]).astype(BF16),
        a0=rwkv_a0[l].reshape(1, N_DIR * RWKV_WIDTH),
        g2=rwkv_g2[l].astype(BF16),
        k_k=rwkv_k_k[l].reshape(1, RWKV_WIDTH),
        k_a=rwkv_k_a[l].reshape(1, RWKV_WIDTH),
        r_k=rwkv_r_k[l].reshape(1, RWKV_WIDTH),
        bd=(head_id[:, None] == head_id[None, :]).astype(BF16),
        ln_g=rwkv_ln_g[l].reshape(1, RWKV_WIDTH),
        ln_b=rwkv_ln_b[l].reshape(1, RWKV_WIDTH),
        w_proj_a=w_proj_a[l].astype(BF16),
        w_proj_b=w_proj_b[l].astype(BF16),
        w_out=w_out[l].astype(BF16),
        g_norm2=g_norm2[l].reshape(1, D_MODEL),
        w_ffn_up=w_ffn_up[l].astype(BF16),
        conv_w=ffn_conv_w[l],
        conv_b=ffn_conv_b[l].reshape(1, 2 * D_FF),
        w_ffn_down=w_ffn_down[l].astype(BF16),
        g_final=g_final.reshape(1, D_MODEL),
    )

    ctx_row = db
    mod_rows = 2 * SUBLANES
    cond = jnp.zeros((mod_rows, D_MODEL), F32).at[:db].set(c).at[ctx_row].set(c_ctx)
    mod = _modulation(cond, w_ada[l], b_ada[l]).reshape(mod_rows, 6, D_MODEL)

    y_prompt, kc, vc, sc = _trunk(x_prompt, mod, lambda i: ctx_row, lambda i: ctx_row, p, None)

    lat_tiles = max(dseq // 512, 1)
    ctx = (cache_k[:, l].reshape(db, past, KV_WIDTH), cache_v[:, l].reshape(db, past, KV_WIDTH), state_rwkv[:, l])
    y_sample, _, _, _ = _trunk(x_sample, mod, lambda i: i // lat_tiles, lambda i: i, p, ctx)

    new_cache_k = kc.reshape(nb, 1, seq, N_KV_HEADS, HEAD_DIM)
    new_cache_v = vc.reshape(nb, 1, seq, N_KV_HEADS, HEAD_DIM)
    new_state = sc.reshape(nb, 1, N_DIR, RWKV_HEADS, RWKV_HEAD_SIZE, RWKV_HEAD_SIZE)
    return (y_prompt, y_sample, new_cache_k, new_cache_v, new_state)
```

```python
import functools

import jax
import jax.numpy as jnp
from jax import lax
from jax.experimental import pallas as pl
from jax.experimental.pallas import tpu as pltpu

F32 = jnp.float32
BF16 = jnp.bfloat16

D_MODEL = 1024
GRID_W = 64
HEAD_DIM = 64
N_Q_HEADS = 8
N_KV_HEADS = 2
GQA_GROUP = N_Q_HEADS // N_KV_HEADS
ATTN_WIDTH = N_Q_HEADS * HEAD_DIM
KV_WIDTH = N_KV_HEADS * HEAD_DIM
WINDOW = 128
BLOCK = 128
ROPE_BASE = 10000.0
ROPE_PAIRS = HEAD_DIM // 4
RWKV_HEADS = 8
RWKV_HEAD_SIZE = 64
RWKV_WIDTH = RWKV_HEADS * RWKV_HEAD_SIZE
N_DIR = 2
DECAY_RANK = 64
ICLR_RANK = 64
GATE_RANK = 128
RW_SHIFT_COLS = 3 * RWKV_WIDTH + N_DIR * DECAY_RANK + N_DIR * ICLR_RANK + GATE_RANK
D_FF = 2816
NORM_EPS = 1e-6
GN_EPS = 64e-5
MASK_VALUE = -1e30

Q_OFF = 0
K_OFF = ATTN_WIDTH
V_OFF = K_OFF + KV_WIDTH
GA_OFF = V_OFF + KV_WIDTH
GB_OFF = GA_OFF + D_MODEL
ZR_OFF = GB_OFF + D_MODEL
W_IN_COLS = ZR_OFF + RW_SHIFT_COLS

ZW_OFF = 3 * RWKV_WIDTH
ZA_OFF = ZW_OFF + N_DIR * DECAY_RANK
ZG_OFF = ZA_OFF + N_DIR * ICLR_RANK

SCAN_CHUNK = 64
NEUMANN_STEPS = 6
SUBLANES = 8
FF_CHUNK = 256
VMEM_LIMIT = 56 * 1024 * 1024

_NT = (((1,), (1,)), ((), ()))
_TN = (((0,), (0,)), ((), ()))


def _params(*sem):
    return pltpu.CompilerParams(dimension_semantics=sem, vmem_limit_bytes=VMEM_LIMIT)


def _const_spec(shape):
    nd = len(shape)
    return pl.BlockSpec(shape, lambda *_: (0,) * nd, pipeline_mode=pl.Buffered(1))


def _dot(a, b):
    return jnp.dot(a, b, preferred_element_type=F32)


def _split_dot(x, m):
    hi = x.astype(BF16)
    lo = (x - hi.astype(F32)).astype(BF16)
    return _dot(hi, m) + _dot(lo, m)


def _sigmoid(x):
    return 1.0 / (1.0 + jnp.exp(-x))


def _mod_kernel(c_ref, w_ref, b_ref, o_ref):
    c = c_ref[...]
    s = (c * _sigmoid(c)).astype(BF16)
    o_ref[...] = _dot(s, w_ref[...].astype(BF16)) + b_ref[...]


def _modulation(cond, w_ada, b_ada):
    rows = cond.shape[0]
    n = w_ada.shape[1]
    tn = D_MODEL
    return pl.pallas_call(
        _mod_kernel,
        grid=(n // tn,),
        in_specs=[pl.BlockSpec((rows, D_MODEL), lambda j: (0, 0)),
                  pl.BlockSpec((D_MODEL, tn), lambda j: (0, j)),
                  pl.BlockSpec((1, tn), lambda j: (0, j))],
        out_specs=pl.BlockSpec((rows, tn), lambda j: (0, j)),
        out_shape=jax.ShapeDtypeStruct((rows, n), F32),
        compiler_params=_params("parallel"),
        name="modulation",
    )(cond, w_ada, b_ada.reshape(1, n))


def _swap16(x):
    w = x.shape[1]
    lane = lax.broadcasted_iota(jnp.int32, x.shape, 1)
    first = (lane % 32) < 16
    return jnp.where(first, pltpu.roll(x, w - 16, 1), pltpu.roll(x, 16, 1))


def _in_proj_kernel(*refs, rope):
    if rope:
        x_ref, mod_ref, g_ref, w_ref, cos_ref, sin_ref, q_ref, k_ref, v_ref, ga_ref, gb_ref, zr_ref = refs
    else:
        x_ref, mod_ref, g_ref, w_ref, q_ref, k_ref, v_ref, ga_ref, gb_ref, zr_ref = refs
    x = x_ref[...]
    m = mod_ref[0]
    ms = jnp.mean(x * x, axis=-1, keepdims=True)
    h = x * lax.rsqrt(ms + NORM_EPS) * g_ref[...]
    h = (h * (1.0 + m[1:2, :]) + m[0:1, :]).astype(BF16)

    q = _dot(h, w_ref[:, Q_OFF:K_OFF])
    k = _dot(h, w_ref[:, K_OFF:V_OFF])
    v = _dot(h, w_ref[:, V_OFF:GA_OFF])
    if rope:
        cos = cos_ref[...]
        sin = sin_ref[...]
        reps = ATTN_WIDTH // cos.shape[1]
        cos_q = jnp.concatenate([cos] * reps, axis=1)
        sin_q = jnp.concatenate([sin] * reps, axis=1)
        q = q * cos_q + _swap16(q) * sin_q
        k = k * cos + _swap16(k) * sin
    q_ref[...] = (q * (HEAD_DIM ** -0.5)).astype(q_ref.dtype)
    k_ref[...] = k.astype(k_ref.dtype)
    v_ref[...] = v.astype(v_ref.dtype)
    ga_ref[...] = _sigmoid(_dot(h, w_ref[:, GA_OFF:GB_OFF])).astype(ga_ref.dtype)
    gb_ref[...] = _sigmoid(_dot(h, w_ref[:, GB_OFF:ZR_OFF])).astype(gb_ref.dtype)
    zr_ref[...] = _dot(h, w_ref[:, ZR_OFF:W_IN_COLS])


def _in_proj(x, mod, g1, w_in, seq_len, mod_row, rope_tabs, kv_dtype):
    ntok = x.shape[0]
    tm = min(512, seq_len) if rope_tabs is not None else 512
    tiles_per_seq = max(seq_len // tm, 1)
    rope = rope_tabs is not None
    tok = lambda width: pl.BlockSpec((tm, width), lambda i: (i, 0))
    in_specs = [tok(D_MODEL),
                pl.BlockSpec((1, 6, D_MODEL), lambda i: (mod_row(i), 0, 0)),
                _const_spec((1, D_MODEL)),
                _const_spec((D_MODEL, W_IN_COLS))]
    args = [x, mod, g1, w_in]
    if rope:
        in_specs += [pl.BlockSpec((tm, 2 * HEAD_DIM), lambda i: (i % tiles_per_seq, 0))] * 2
        args += list(rope_tabs)
    out_shape = [jax.ShapeDtypeStruct((ntok, ATTN_WIDTH), BF16),
                 jax.ShapeDtypeStruct((ntok, KV_WIDTH), kv_dtype),
                 jax.ShapeDtypeStruct((ntok, KV_WIDTH), kv_dtype),
                 jax.ShapeDtypeStruct((ntok, D_MODEL), BF16),
                 jax.ShapeDtypeStruct((ntok, D_MODEL), BF16),
                 jax.ShapeDtypeStruct((ntok, RW_SHIFT_COLS), F32)]
    out_specs = [tok(ATTN_WIDTH), tok(KV_WIDTH), tok(KV_WIDTH), tok(D_MODEL), tok(D_MODEL), tok(RW_SHIFT_COLS)]
    return pl.pallas_call(
        functools.partial(_in_proj_kernel, rope=rope),
        grid=(ntok // tm,),
        in_specs=in_specs, out_specs=out_specs, out_shape=out_shape,
        compiler_params=_params("parallel"),
        name="in_proj_rope" if rope else "in_proj",
    )(*args)


def _softmax_pv(s, sink_col, v):
    m = jnp.maximum(jnp.max(s, axis=-1, keepdims=True), sink_col)
    p = jnp.exp(s - m)
    denom = jnp.sum(p, axis=-1, keepdims=True) + jnp.exp(sink_col - m)
    return _dot(p.astype(BF16), v) / denom


def _group_rows(q, g):
    return jnp.concatenate(
        [q[:, (g * GQA_GROUP + i) * HEAD_DIM:(g * GQA_GROUP + i + 1) * HEAD_DIM] for i in range(GQA_GROUP)], axis=0)


def _sink_rows(sink_ref, g, t):
    return jnp.concatenate(
        [jnp.full((t, 1), sink_ref[g * GQA_GROUP + i], F32) for i in range(GQA_GROUP)], axis=0)


def _ungroup(outs, t):
    return jnp.concatenate([o[i * t:(i + 1) * t, :] for o in outs for i in range(GQA_GROUP)], axis=1)


def _attn_ctx_kernel(sink_ref, q_ref, k_ref, v_ref, o_ref):
    q = q_ref[0]
    k = k_ref[0].astype(BF16)
    v = v_ref[0].astype(BF16)
    t = q.shape[0]
    outs = []
    for g in range(N_KV_HEADS):
        kg = k[:, g * HEAD_DIM:(g + 1) * HEAD_DIM]
        vg = v[:, g * HEAD_DIM:(g + 1) * HEAD_DIM]
        s = lax.dot_general(_group_rows(q, g), kg, _NT, preferred_element_type=F32)
        outs.append(_softmax_pv(s, _sink_rows(sink_ref, g, t), vg))
    o_ref[0] = _ungroup(outs, t).astype(o_ref.dtype)


def _attn_ctx(q, k, v, sink):
    b, t, _ = q.shape
    seq = lambda width: pl.BlockSpec((1, t, width), lambda i: (i, 0, 0))
    return pl.pallas_call(
        _attn_ctx_kernel,
        grid=(b,),
        in_specs=[pl.BlockSpec(memory_space=pltpu.SMEM), seq(ATTN_WIDTH), seq(KV_WIDTH), seq(KV_WIDTH)],
        out_specs=seq(ATTN_WIDTH),
        out_shape=jax.ShapeDtypeStruct((b, t, ATTN_WIDTH), BF16),
        compiler_params=_params("parallel"),
        name="attn_ctx",
    )(sink, q, k, v)


def _attn_lat_kernel(sink_ref, q_ref, kp_ref, kc_ref, kn_ref, vp_ref, vc_ref, vn_ref, kx_ref, vx_ref, o_ref):
    n = pl.program_id(1)
    nb = pl.num_programs(1)
    q = q_ref[0]
    k = jnp.concatenate([kp_ref[0], kc_ref[0], kn_ref[0], kx_ref[0].astype(BF16)], axis=0)
    v = jnp.concatenate([vp_ref[0], vc_ref[0], vn_ref[0], vx_ref[0].astype(BF16)], axis=0)
    rows = GQA_GROUP * BLOCK
    rq = lax.broadcasted_iota(jnp.int32, (rows, BLOCK), 0) % BLOCK
    col = lax.broadcasted_iota(jnp.int32, (rows, BLOCK), 1)
    keep_prev = col >= rq + jnp.where(n > 0, 0, BLOCK)
    keep_next = col <= rq - jnp.where(n < nb - 1, 0, BLOCK)
    outs = []
    for g in range(N_KV_HEADS):
        kg = k[:, g * HEAD_DIM:(g + 1) * HEAD_DIM]
        vg = v[:, g * HEAD_DIM:(g + 1) * HEAD_DIM]
        s = lax.dot_general(_group_rows(q, g), kg, _NT, preferred_element_type=F32)
        s = jnp.concatenate([jnp.where(keep_prev, s[:, :BLOCK], MASK_VALUE), s[:, BLOCK:2 * BLOCK],
                             jnp.where(keep_next, s[:, 2 * BLOCK:3 * BLOCK], MASK_VALUE), s[:, 3 * BLOCK:]], axis=1)
        outs.append(_softmax_pv(s, _sink_rows(sink_ref, g, BLOCK), vg))
    o_ref[0] = _ungroup(outs, BLOCK).astype(o_ref.dtype)


def _attn_lat(q, k, v, k_ctx, v_ctx, sink):
    assert WINDOW == BLOCK, "the block-triangular band masks assume one block of reach on either side"
    b, t, _ = q.shape
    nb = t // BLOCK
    p = k_ctx.shape[1]
    blk = lambda width, f: pl.BlockSpec((1, BLOCK, width), lambda i, n: (i, f(n), 0))
    prev = lambda n: jnp.maximum(n - 1, 0)
    cur = lambda n: n
    nxt = lambda n: jnp.minimum(n + 1, nb - 1)
    ctx = pl.BlockSpec((1, p, KV_WIDTH), lambda i, n: (i, 0, 0))
    return pl.pallas_call(
        _attn_lat_kernel,
        grid=(b, nb),
        in_specs=[pl.BlockSpec(memory_space=pltpu.SMEM), blk(ATTN_WIDTH, cur),
                  blk(KV_WIDTH, prev), blk(KV_WIDTH, cur), blk(KV_WIDTH, nxt),
                  blk(KV_WIDTH, prev), blk(KV_WIDTH, cur), blk(KV_WIDTH, nxt), ctx, ctx],
        out_specs=blk(ATTN_WIDTH, cur),
        out_shape=jax.ShapeDtypeStruct((b, t, ATTN_WIDTH), BF16),
        compiler_params=_params("parallel", "parallel"),
        name="attn_lat",
    )(sink, q, k, k, k, v, v, v, k_ctx, v_ctx)


def _rw_prep_kernel(z_ref, zp_ref, zn_ref, mu_ref, w2_ref, w0_ref, a2_ref, a0_ref, g2_ref, kk_ref, ka_ref,
                    rk_ref, bd_ref, lw_ref, kd_ref, bb_ref, a_ref, r_ref, v_ref, g_ref, bonus_ref):
    i = pl.program_id(1)
    nt = pl.num_programs(1)
    z = z_ref[0]
    tm = z.shape[0]
    row = lax.broadcasted_iota(jnp.int32, z.shape, 0)
    halo_prev = jnp.where(i > 0, zp_ref[0, SUBLANES - 1:SUBLANES, :], 0.0)
    halo_next = jnp.where(i < nt - 1, zn_ref[0, 0:1, :], 0.0)
    prev = jnp.where(row == 0, halo_prev, pltpu.roll(z, 1, 0))
    nxt = jnp.where(row == tm - 1, halo_next, pltpu.roll(z, tm - 1, 0))
    mu = mu_ref[...]
    z = z + mu[0:1, :] * (prev - z) + mu[1:2, :] * (nxt - z)

    r = z[:, 0:RWKV_WIDTH]
    k = z[:, RWKV_WIDTH:2 * RWKV_WIDTH]
    v = z[:, 2 * RWKV_WIDTH:3 * RWKV_WIDTH]
    zw = z[:, ZW_OFF:ZA_OFF]
    za = z[:, ZA_OFF:ZG_OFF]
    zg = z[:, ZG_OFF:RW_SHIFT_COLS]

    logit = w0_ref[...] + _dot(jnp.tanh(zw).astype(BF16), w2_ref[...])
    lw = -jnp.exp(F32(-0.5)) * _sigmoid(logit)
    a = _sigmoid(a0_ref[...] + _dot(za.astype(BF16), a2_ref[...]))
    g_ref[0] = _dot(_sigmoid(zg).astype(BF16), g2_ref[...])

    bd = bd_ref[...]
    kk = k * kk_ref[...]
    kk = kk * lax.rsqrt(jnp.maximum(_split_dot(kk * kk, bd), 1e-24))
    ka = ka_ref[...]
    ksum = jnp.zeros_like(k)
    for d in range(N_DIR):
        ad = a[:, d * RWKV_WIDTH:(d + 1) * RWKV_WIDTH]
        kd = k * (1.0 + (ad - 1.0) * ka)
        lw_ref[d, 0] = lw[:, d * RWKV_WIDTH:(d + 1) * RWKV_WIDTH]
        kd_ref[d, 0] = kd.astype(kd_ref.dtype)
        bb_ref[d, 0] = (ad * kk).astype(bb_ref.dtype)
        ksum = ksum + kd
    a_ref[0] = (-kk).astype(a_ref.dtype)
    r_ref[0] = r.astype(r_ref.dtype)
    v_ref[0] = v.astype(v_ref.dtype)
    bonus_ref[0] = _split_dot(r * ksum * rk_ref[...], bd) * v


def _rw_prep(zr, p):
    b, t, _ = zr.shape
    tm = 256
    nt = t // tm
    hb = tm // SUBLANES
    last_hb = t // SUBLANES - 1
    tok = lambda: pl.BlockSpec((1, tm, RWKV_WIDTH), lambda i, j: (i, j, 0))
    tok2 = lambda: pl.BlockSpec((N_DIR, 1, tm, RWKV_WIDTH), lambda i, j: (0, i, j, 0))
    one = jax.ShapeDtypeStruct((b, t, RWKV_WIDTH), F32)
    two = jax.ShapeDtypeStruct((N_DIR, b, t, RWKV_WIDTH), F32)
    one_bf = jax.ShapeDtypeStruct((b, t, RWKV_WIDTH), BF16)
    two_bf = jax.ShapeDtypeStruct((N_DIR, b, t, RWKV_WIDTH), BF16)
    in_specs = [pl.BlockSpec((1, tm, RW_SHIFT_COLS), lambda i, j: (i, j, 0)),
                pl.BlockSpec((1, SUBLANES, RW_SHIFT_COLS), lambda i, j: (i, jnp.maximum(j * hb - 1, 0), 0)),
                pl.BlockSpec((1, SUBLANES, RW_SHIFT_COLS), lambda i, j: (i, jnp.minimum((j + 1) * hb, last_hb), 0)),
                _const_spec((2, RW_SHIFT_COLS)),
                _const_spec((N_DIR * DECAY_RANK, N_DIR * RWKV_WIDTH)), _const_spec((1, N_DIR * RWKV_WIDTH)),
                _const_spec((N_DIR * ICLR_RANK, N_DIR * RWKV_WIDTH)), _const_spec((1, N_DIR * RWKV_WIDTH)),
                _const_spec((GATE_RANK, RWKV_WIDTH)),
                _const_spec((1, RWKV_WIDTH)), _const_spec((1, RWKV_WIDTH)), _const_spec((1, RWKV_WIDTH)),
                _const_spec((RWKV_WIDTH, RWKV_WIDTH))]
    return pl.pallas_call(
        _rw_prep_kernel,
        grid=(b, nt),
        in_specs=in_specs,
        out_specs=[tok2(), tok2(), tok2(), tok(), tok(), tok(), tok(), tok()],
        out_shape=[two, two_bf, two_bf, one_bf, one_bf, one_bf, one, one],
        compiler_params=_params("parallel", "parallel"),
        name="rw_prep",
    )(zr, zr, zr, p["mu"], p["w2"], p["w0"], p["a2"], p["a0"], p["g2"], p["k_k"], p["k_a"], p["r_k"], p["bd"])


def _rw_scan_kernel(*refs, has_s0, nchunks):
    if has_s0:
        s0_ref, refs = refs[0], refs[1:]
    (lw0_ref, kd0_ref, bb0_ref, a0_ref, r0_ref, v0_ref, lw1_ref, kd1_ref, bb1_ref, a1_ref, r1_ref, v1_ref,
     y0_ref, y1_ref, sf_ref, s_scr) = refs
    dir_refs = ((lw0_ref, kd0_ref, bb0_ref, a0_ref, r0_ref, v0_ref, y0_ref),
                (lw1_ref, kd1_ref, bb1_ref, a1_ref, r1_ref, v1_ref, y1_ref))
    i = pl.program_id(1)
    c = SCAN_CHUNK
    n = RWKV_HEAD_SIZE

    @pl.when(i == 0)
    def _():
        if has_s0:
            s_scr[...] = s0_ref[0]
        else:
            s_scr[...] = jnp.zeros_like(s_scr)

    diff = lax.broadcasted_iota(jnp.int32, (c, c), 0) - lax.broadcasted_iota(jnp.int32, (c, c), 1)
    incl = (diff >= 0, diff <= 0)
    strict = (diff > 0, diff < 0)
    diff2 = (lax.broadcasted_iota(jnp.int32, (c, 2 * c), 0) - lax.broadcasted_iota(jnp.int32, (c, 2 * c), 1) % c)
    incl2 = (diff2 >= 0, diff2 <= 0)
    chains = [(d, h) for d in range(N_DIR) for h in range(RWKV_HEADS)]
    nch = len(chains)
    dirs = [d for d, _ in chains]
    each = lambda f: [f(d, slice(h * n, (h + 1) * n)) for d, h in chains]

    def chunk(jj, carry):
        pre = []
        for d in range(N_DIR):
            lw_ref, kd_ref, bb_ref, a_ref, r_ref, v_ref, _ = dir_refs[d]
            j = jj if d == 0 else nchunks - 1 - jj
            rows = pl.ds(pl.multiple_of(j * c, c), c)
            lw = lw_ref[0, 0, rows, :]
            kd = kd_ref[0, 0, rows, :].astype(F32)
            bb = bb_ref[0, 0, rows, :].astype(F32)
            l1 = lw.astype(BF16)
            rem = lw - l1.astype(F32)
            l2 = rem.astype(BF16)
            l3 = (rem - l2.astype(F32)).astype(BF16)
            tri = incl[d].astype(BF16)
            cum = _dot(jnp.concatenate([tri, tri, tri], axis=1), jnp.concatenate([l1, l2, l3], axis=0))
            ctot = jnp.sum(lw, axis=0, keepdims=True)
            en = jnp.exp(-cum)
            eh = jnp.exp(ctot - cum)
            pre.append(dict(
                rows=rows,
                at=a_ref[0, rows, :].astype(F32) * jnp.exp(cum - lw),
                rt=r_ref[0, rows, :].astype(F32) * jnp.exp(cum),
                bt=(bb * en).astype(BF16), kt=(kd * en).astype(BF16),
                bh=(bb * eh).astype(BF16), kh=(kd * eh).astype(BF16),
                gam=jnp.exp(ctot), v=v_ref[0, rows, :]))

        at = each(lambda d, hs: pre[d]["at"][:, hs])
        rt = each(lambda d, hs: pre[d]["rt"][:, hs])
        vh = each(lambda d, hs: pre[d]["v"][:, hs])
        bh = each(lambda d, hs: pre[d]["bh"][:, hs])
        kh = each(lambda d, hs: pre[d]["kh"][:, hs])
        gam = each(lambda d, hs: pre[d]["gam"][:, hs])
        amat = each(lambda d, hs: lax.dot_general(
            jnp.concatenate([pre[d]["at"][:, hs], pre[d]["rt"][:, hs]], axis=0).astype(BF16),
            jnp.concatenate([pre[d]["bt"][:, hs], pre[d]["kt"][:, hs]], axis=0), _NT, preferred_element_type=F32))
        a_ak = [jnp.where(strict[dirs[q]], amat[q][:c, c:], 0.0).astype(BF16) for q in range(nch)]
        a_r = [jnp.where(incl2[dirs[q]], amat[q][c:, :], 0.0).astype(BF16) for q in range(nch)]
        x = [jnp.where(strict[dirs[q]], amat[q][:c, :c], 0.0) for q in range(nch)]
        z = [jnp.concatenate([at[q], _dot(a_ak[q], vh[q])], axis=1) for q in range(nch)]
        for step in range(NEUMANN_STEPS):
            xb = [x[q].astype(BF16) for q in range(nch)]
            if step < NEUMANN_STEPS - 1:
                prod = [_dot(xb[q], jnp.concatenate([z[q].astype(BF16), xb[q]], axis=1)) for q in range(nch)]
                z = [z[q] + prod[q][:, :2 * n] for q in range(nch)]
                x = [prod[q][:, 2 * n:] for q in range(nch)]
            else:
                z = [z[q] + _dot(xb[q], z[q].astype(BF16)) for q in range(nch)]
        zero = jnp.zeros((c, n), BF16)
        rmat = [jnp.concatenate([z[q].astype(BF16), jnp.concatenate([zero, vh[q]], axis=1)], axis=0)
                for q in range(nch)]
        qy = [_dot(a_r[q], rmat[q]) for q in range(nch)]
        qm = [(rt[q] + qy[q][:, :n]).astype(BF16) for q in range(nch)]
        mn = [lax.dot_general(rmat[q], jnp.concatenate([bh[q], kh[q]], axis=0), _TN, preferred_element_type=F32)
              for q in range(nch)]
        s = [s_scr[d, h] for d, h in chains]
        sb = [s[q].astype(BF16) for q in range(nch)]
        ys = [lax.dot_general(qm[q], sb[q], _NT, preferred_element_type=F32) + qy[q][:, n:] for q in range(nch)]
        for q, (d, h) in enumerate(chains):
            s_scr[d, h] = s[q] * gam[q] + _dot(sb[q], mn[q][:n, :].astype(BF16)) + mn[q][n:, :]
        for d in range(N_DIR):
            y_ref = dir_refs[d][-1]
            y_ref[0, pre[d]["rows"], :] = jnp.concatenate(ys[d * RWKV_HEADS:(d + 1) * RWKV_HEADS], axis=1)
        return carry

    lax.fori_loop(0, nchunks, chunk, 0)

    @pl.when(i == pl.num_programs(1) - 1)
    def _():
        sf_ref[0] = s_scr[...]


def _rw_scan(s0, lw, kd, bb, a, r, v):
    _, b, t, _ = lw.shape
    tb = min(t, 512)
    nblk = t // tb
    blk = (lambda i: i, lambda i: nblk - 1 - i)
    dir_spec = lambda d: pl.BlockSpec((1, 1, tb, RWKV_WIDTH), lambda bi, i: (d, bi, blk[d](i), 0))
    tok_spec = lambda d: pl.BlockSpec((1, tb, RWKV_WIDTH), lambda bi, i: (bi, blk[d](i), 0))
    st_spec = pl.BlockSpec((1, N_DIR, RWKV_HEADS, RWKV_HEAD_SIZE, RWKV_HEAD_SIZE), lambda bi, i: (bi, 0, 0, 0, 0))
    has_s0 = s0 is not None
    in_specs, args = [], []
    if has_s0:
        in_specs.append(st_spec)
        args.append(s0)
    for d in range(N_DIR):
        in_specs += [dir_spec(d), dir_spec(d), dir_spec(d), tok_spec(d), tok_spec(d), tok_spec(d)]
        args += [lw, kd, bb, a, r, v]
    y_shape = jax.ShapeDtypeStruct((b, t, RWKV_WIDTH), F32)
    return pl.pallas_call(
        functools.partial(_rw_scan_kernel, has_s0=has_s0, nchunks=tb // SCAN_CHUNK),
        grid=(b, nblk),
        in_specs=in_specs,
        out_specs=[tok_spec(0), tok_spec(1), st_spec],
        out_shape=[y_shape, y_shape,
                   jax.ShapeDtypeStruct((b, N_DIR, RWKV_HEADS, RWKV_HEAD_SIZE, RWKV_HEAD_SIZE), F32)],
        scratch_shapes=[pltpu.VMEM((N_DIR, RWKV_HEADS, RWKV_HEAD_SIZE, RWKV_HEAD_SIZE), F32)],
        compiler_params=_params("parallel", "arbitrary"),
        name="rw_scan",
    )(*args)


def _post_kernel(x_ref, mod_ref, attn_ref, y0_ref, y1_ref, bonus_ref, g_ref, ga_ref, gb_ref, lng_ref, lnb_ref,
                 bd_ref, wa_ref, wb_ref, wo_ref, o_ref):
    m = mod_ref[0]
    y = y0_ref[0] + y1_ref[0]
    bd = bd_ref[...]
    inv_n = 1.0 / RWKV_HEAD_SIZE
    mean = _split_dot(y, bd) * inv_n
    yc = y - mean
    var = _split_dot(yc * yc, bd) * inv_n
    yn = yc * lax.rsqrt(var + GN_EPS) * lng_ref[...] + lnb_ref[...]
    rw = ((yn + bonus_ref[0]) * g_ref[0]).astype(BF16)
    merged = (ga_ref[0].astype(F32) * _dot(attn_ref[0], wa_ref[...])
              + gb_ref[0].astype(F32) * _dot(rw, wb_ref[...]))
    o_ref[0] = x_ref[0] + m[2:3, :] * _dot(merged.astype(BF16), wo_ref[...])


def _post(x, mod, mod_row, attn, y0, y1, bonus, g, ga, gb, p):
    b, t, _ = x.shape
    tm = min(t, 512)
    tok = lambda width: pl.BlockSpec((1, tm, width), lambda i, j: (i, j, 0))
    in_specs = [tok(D_MODEL),
                pl.BlockSpec((1, 6, D_MODEL), lambda i, j: (mod_row(i), 0, 0)),
                tok(ATTN_WIDTH),
                tok(RWKV_WIDTH), tok(RWKV_WIDTH), tok(RWKV_WIDTH), tok(RWKV_WIDTH), tok(D_MODEL), tok(D_MODEL),
                _const_spec((1, RWKV_WIDTH)), _const_spec((1, RWKV_WIDTH)),
                _const_spec((RWKV_WIDTH, RWKV_WIDTH)),
                _const_spec((ATTN_WIDTH, D_MODEL)), _const_spec((RWKV_WIDTH, D_MODEL)),
                _const_spec((D_MODEL, D_MODEL))]
    return pl.pallas_call(
        _post_kernel,
        grid=(b, t // tm),
        in_specs=in_specs,
        out_specs=tok(D_MODEL),
        out_shape=jax.ShapeDtypeStruct((b, t, D_MODEL), F32),
        compiler_params=_params("parallel", "parallel"),
        name="post",
    )(x, mod, attn, y0, y1, bonus, g, ga, gb, p["ln_g"], p["ln_b"], p["bd"], p["w_proj_a"], p["w_proj_b"], p["w_out"])


def _ffn_kernel(x_ref, xp_ref, xn_ref, mod_ref, g2_ref, wup_ref, cw_ref, cb_ref, wdn_ref, gf_ref, o_ref):
    i = pl.program_id(1)
    nt = pl.num_programs(1)
    m = mod_ref[0]
    g2 = g2_ref[...]

    def norm_mod(x):
        ms = jnp.mean(x * x, axis=-1, keepdims=True)
        return x * lax.rsqrt(ms + NORM_EPS) * g2 * (1.0 + m[4:5, :]) + m[3:4, :]

    x = x_ref[0]
    tm = x.shape[0]
    hp = jnp.where(i > 0, norm_mod(xp_ref[0]), 0.0)
    hn = jnp.where(i < nt - 1, norm_mod(xn_ref[0]), 0.0)
    h = jnp.concatenate([hp, norm_mod(x), hn], axis=0).astype(BF16)
    rows = tm + 2 * SUBLANES
    cw = cw_ref[...]
    cb = cb_ref[...]

    def conv(u, lo):
        w = cw[:, lo:lo + FF_CHUNK]
        out = (pltpu.roll(u, 1, 0) * w[0:1, :] + u * w[1:2, :] + pltpu.roll(u, rows - 1, 0) * w[2:3, :]
               + cb[:, lo:lo + FF_CHUNK])
        return out[SUBLANES:SUBLANES + tm, :]

    def up(j):
        lo = j * FF_CHUNK
        return _dot(h, wup_ref[:, lo:lo + FF_CHUNK]), _dot(h, wup_ref[:, D_FF + lo:D_FF + lo + FF_CHUNK])

    nchunks = D_FF // FF_CHUNK
    acc = jnp.zeros((tm, D_MODEL), F32)
    ahead = up(0)
    for j in range(nchunks):
        lo = j * FF_CHUNK
        u_val, u_gate = ahead
        if j + 1 < nchunks:
            ahead = up(j + 1)
        val = conv(u_val, lo)
        gate = conv(u_gate, D_FF + lo)
        act = (gate * _sigmoid(gate) * val).astype(BF16)
        acc = acc + _dot(act, wdn_ref[lo:lo + FF_CHUNK, :])
    x2 = x + m[5:6, :] * acc
    ms = jnp.mean(x2 * x2, axis=-1, keepdims=True)
    o_ref[0] = x2 * lax.rsqrt(ms + NORM_EPS) * gf_ref[...]


def _ffn(x, mod, mod_row, p):
    b, t, _ = x.shape
    tm = min(t, 512)
    hb = tm // SUBLANES
    last_hb = t // SUBLANES - 1
    in_specs = [pl.BlockSpec((1, tm, D_MODEL), lambda i, j: (i, j, 0)),
                pl.BlockSpec((1, SUBLANES, D_MODEL), lambda i, j: (i, jnp.maximum(j * hb - 1, 0), 0)),
                pl.BlockSpec((1, SUBLANES, D_MODEL), lambda i, j: (i, jnp.minimum((j + 1) * hb, last_hb), 0)),
                pl.BlockSpec((1, 6, D_MODEL), lambda i, j: (mod_row(i), 0, 0)),
                _const_spec((1, D_MODEL)),
                _const_spec((D_MODEL, 2 * D_FF)), _const_spec((3, 2 * D_FF)), _const_spec((1, 2 * D_FF)),
                _const_spec((D_FF, D_MODEL)), _const_spec((1, D_MODEL))]
    return pl.pallas_call(
        _ffn_kernel,
        grid=(b, t // tm),
        in_specs=in_specs,
        out_specs=pl.BlockSpec((1, tm, D_MODEL), lambda i, j: (i, j, 0)),
        out_shape=jax.ShapeDtypeStruct((b, t, D_MODEL), F32),
        compiler_params=_params("parallel", "parallel"),
        name="ffn",
    )(x, x, x, mod, p["g_norm2"], p["w_ffn_up"], p["conv_w"], p["conv_b"], p["w_ffn_down"], p["g_final"])


def _rope_tables(t):
    rows = t // GRID_W
    row = jnp.repeat(jnp.arange(rows), GRID_W).astype(F32)
    col = jnp.tile(jnp.arange(GRID_W), rows).astype(F32)
    freqs = ROPE_BASE ** (-jnp.arange(ROPE_PAIRS, dtype=F32) / ROPE_PAIRS)
    ar = row[:, None] * freqs
    ac = col[:, None] * freqs
    cos = jnp.concatenate([jnp.cos(ar), jnp.cos(ar), jnp.cos(ac), jnp.cos(ac)], axis=1)
    sin = jnp.concatenate([-jnp.sin(ar), jnp.sin(ar), -jnp.sin(ac), jnp.sin(ac)], axis=1)
    return jnp.tile(cos, (1, 2)), jnp.tile(sin, (1, 2))


def _block_diag2(w):
    z = jnp.zeros_like(w[0])
    return jnp.concatenate([jnp.concatenate([w[0], z], axis=1), jnp.concatenate([z, w[1]], axis=1)], axis=0)


def _trunk(x, mod, mod_row_tile, mod_row_seq, p, ctx):
    b, t, _ = x.shape
    rope = _rope_tables(t) if ctx is not None else None
    kv_dtype = BF16 if ctx is not None else F32
    q, k, v, ga, gb, zr = _in_proj(x.reshape(b * t, D_MODEL), mod, p["g_norm1"], p["w_in"], t,
                                   mod_row_tile, rope, kv_dtype)
    seq = lambda arr: arr.reshape(b, t, arr.shape[-1])
    q, k, v, ga, gb, zr = seq(q), seq(k), seq(v), seq(ga), seq(gb), seq(zr)
    if ctx is None:
        attn = _attn_ctx(q, k, v, p["sink"])
        s0 = None
    else:
        k_ctx, v_ctx, s0 = ctx
        attn = _attn_lat(q, k, v, k_ctx, v_ctx, p["sink"])
    lw, kd, bb, a, r, vr, g, bonus = _rw_prep(zr, p)
    y0, y1, s_final = _rw_scan(s0, lw, kd, bb, a, r, vr)
    x1 = _post(x, mod, mod_row_seq, attn, y0, y1, bonus, g, ga, gb, p)
    out = _ffn(x1, mod, mod_row_seq, p)
    return out, k, v, s_final


def kernel(x_prompt, x_sample, c, cache_k, cache_v, state_rwkv, c_ctx, w_ada, b_ada, g_norm1, w_in, attn_sink, w_proj_a, w_proj_b, rwkv_mu, rwkv_w0, rwkv_w2, rwkv_a0, rwkv_a2, rwkv_k_k, rwkv_k_a, rwkv_r_k, rwkv_g2, rwkv_ln_g, rwkv_ln_b, w_out, g_norm2, w_ffn_up, ffn_conv_w, ffn_conv_b, w_ffn_down, g_final):
    depth = w_ada.shape[0]
    assert depth == 1, "single trunk layer"
    l = 0
    nb, seq, _ = x_prompt.shape
    db, dseq, _ = x_sample.shape
    past = cache_k.shape[2]

    head_id = jnp.arange(RWKV_WIDTH) // RWKV_HEAD_SIZE
    p = dict(
        g_norm1=g_norm1[l].reshape(1, D_MODEL),
        w_in=w_in[l].astype(BF16),
        sink=attn_sink[l],
        mu=rwkv_mu[l],
        w2=_block_diag2(rwkv_w2[l]).astype(BF16),
        w0=rwkv_w0[l].reshape(1, N_DIR * RWKV_WIDTH),
        a2=_block_diag2(rwkv_a2[l]).astype(BF16),
        a0=rwkv_a0[l].reshape(1, N_DIR * RWKV_WIDTH),
        g2=rwkv_g2[l].astype(BF16),
        k_k=rwkv_k_k[l].reshape(1, RWKV_WIDTH),
        k_a=rwkv_k_a[l].reshape(1, RWKV_WIDTH),
        r_k=rwkv_r_k[l].reshape(1, RWKV_WIDTH),
        bd=(head_id[:, None] == head_id[None, :]).astype(BF16),
        ln_g=rwkv_ln_g[l].reshape(1, RWKV_WIDTH),
        ln_b=rwkv_ln_b[l].reshape(1, RWKV_WIDTH),
        w_proj_a=w_proj_a[l].astype(BF16),
        w_proj_b=w_proj_b[l].astype(BF16),
        w_out=w_out[l].astype(BF16),
        g_norm2=g_norm2[l].reshape(1, D_MODEL),
        w_ffn_up=w_ffn_up[l].astype(BF16),
        conv_w=ffn_conv_w[l],
        conv_b=ffn_conv_b[l].reshape(1, 2 * D_FF),
        w_ffn_down=w_ffn_down[l].astype(BF16),
        g_final=g_final.reshape(1, D_MODEL),
    )

    ctx_row = db
    mod_rows = 2 * SUBLANES
    cond = jnp.zeros((mod_rows, D_MODEL), F32).at[:db].set(c).at[ctx_row].set(c_ctx)
    mod = _modulation(cond, w_ada[l], b_ada[l]).reshape(mod_rows, 6, D_MODEL)

    y_prompt, kc, vc, sc = _trunk(x_prompt, mod, lambda i: ctx_row, lambda i: ctx_row, p, None)

    lat_tiles = max(dseq // 512, 1)
    ctx = (cache_k[:, l].reshape(db, past, KV_WIDTH), cache_v[:, l].reshape(db, past, KV_WIDTH), state_rwkv[:, l])
    y_sample, _, _, _ = _trunk(x_sample, mod, lambda i: i // lat_tiles, lambda i: i, p, ctx)

    new_cache_k = kc.reshape(nb, 1, seq, N_KV_HEADS, HEAD_DIM)
    new_cache_v = vc.reshape(nb, 1, seq, N_KV_HEADS, HEAD_DIM)
    new_state = sc.reshape(nb, 1, N_DIR, RWKV_HEADS, RWKV_HEAD_SIZE, RWKV_HEAD_SIZE)
    return (y_prompt, y_sample, new_cache_k, new_cache_v, new_state)
```

```python
import functools

import jax
import jax.numpy as jnp
from jax import lax
from jax.experimental import pallas as pl
from jax.experimental.pallas import tpu as pltpu

F32 = jnp.float32
BF16 = jnp.bfloat16

D_MODEL = 1024
GRID_W = 64
HEAD_DIM = 64
N_Q_HEADS = 8
N_KV_HEADS = 2
GQA_GROUP = N_Q_HEADS // N_KV_HEADS
ATTN_WIDTH = N_Q_HEADS * HEAD_DIM
KV_WIDTH = N_KV_HEADS * HEAD_DIM
WINDOW = 128
BLOCK = 128
ROPE_BASE = 10000.0
ROPE_PAIRS = HEAD_DIM // 4
RWKV_HEADS = 8
RWKV_HEAD_SIZE = 64
RWKV_WIDTH = RWKV_HEADS * RWKV_HEAD_SIZE
N_DIR = 2
DECAY_RANK = 64
ICLR_RANK = 64
GATE_RANK = 128
RW_SHIFT_COLS = 3 * RWKV_WIDTH + N_DIR * DECAY_RANK + N_DIR * ICLR_RANK + GATE_RANK
D_FF = 2816
NORM_EPS = 1e-6
GN_EPS = 64e-5
MASK_VALUE = -1e30

Q_OFF = 0
K_OFF = ATTN_WIDTH
V_OFF = K_OFF + KV_WIDTH
GA_OFF = V_OFF + KV_WIDTH
GB_OFF = GA_OFF + D_MODEL
ZR_OFF = GB_OFF + D_MODEL
W_IN_COLS = ZR_OFF + RW_SHIFT_COLS

ZW_OFF = 3 * RWKV_WIDTH
ZA_OFF = ZW_OFF + N_DIR * DECAY_RANK
ZG_OFF = ZA_OFF + N_DIR * ICLR_RANK

SCAN_CHUNK = 64
NEUMANN_STEPS = 6
SUBLANES = 8
FF_CHUNK = 256
FF_ROWS = 128
VMEM_LIMIT = 56 * 1024 * 1024

_NT = (((1,), (1,)), ((), ()))
_TN = (((0,), (0,)), ((), ()))


def _params(*sem):
    return pltpu.CompilerParams(dimension_semantics=sem, vmem_limit_bytes=VMEM_LIMIT)


def _const_spec(shape):
    nd = len(shape)
    return pl.BlockSpec(shape, lambda *_: (0,) * nd, pipeline_mode=pl.Buffered(1))


def _dot(a, b):
    return jnp.dot(a, b, preferred_element_type=F32)


def _split_dot(x, m):
    hi = x.astype(BF16)
    lo = (x - hi.astype(F32)).astype(BF16)
    return _dot(hi, m) + _dot(lo, m)


def _sigmoid(x):
    return 1.0 / (1.0 + jnp.exp(-x))


def _mod_kernel(c_ref, w_ref, b_ref, o_ref):
    c = c_ref[...]
    s = (c * _sigmoid(c)).astype(BF16)
    o_ref[...] = _dot(s, w_ref[...].astype(BF16)) + b_ref[...]


def _modulation(cond, w_ada, b_ada):
    rows = cond.shape[0]
    n = w_ada.shape[1]
    tn = D_MODEL
    return pl.pallas_call(
        _mod_kernel,
        grid=(n // tn,),
        in_specs=[pl.BlockSpec((rows, D_MODEL), lambda j: (0, 0)),
                  pl.BlockSpec((D_MODEL, tn), lambda j: (0, j)),
                  pl.BlockSpec((1, tn), lambda j: (0, j))],
        out_specs=pl.BlockSpec((rows, tn), lambda j: (0, j)),
        out_shape=jax.ShapeDtypeStruct((rows, n), F32),
        compiler_params=_params("parallel"),
        name="modulation",
    )(cond, w_ada, b_ada.reshape(1, n))


def _swap16(x):
    w = x.shape[1]
    lane = lax.broadcasted_iota(jnp.int32, x.shape, 1)
    first = (lane % 32) < 16
    return jnp.where(first, pltpu.roll(x, w - 16, 1), pltpu.roll(x, 16, 1))


def _in_proj_kernel(*refs, rope):
    if rope:
        x_ref, mod_ref, g_ref, w_ref, cos_ref, sin_ref, q_ref, k_ref, v_ref, ga_ref, gb_ref, zr_ref = refs
    else:
        x_ref, mod_ref, g_ref, w_ref, q_ref, k_ref, v_ref, ga_ref, gb_ref, zr_ref = refs
    x = x_ref[...]
    m = mod_ref[0]
    ms = jnp.mean(x * x, axis=-1, keepdims=True)
    h = x * lax.rsqrt(ms + NORM_EPS) * g_ref[...]
    h = (h * (1.0 + m[1:2, :]) + m[0:1, :]).astype(BF16)

    q = _dot(h, w_ref[:, Q_OFF:K_OFF])
    k = _dot(h, w_ref[:, K_OFF:V_OFF])
    v = _dot(h, w_ref[:, V_OFF:GA_OFF])
    if rope:
        cos = cos_ref[...]
        sin = sin_ref[...]
        reps = ATTN_WIDTH // cos.shape[1]
        cos_q = jnp.concatenate([cos] * reps, axis=1)
        sin_q = jnp.concatenate([sin] * reps, axis=1)
        q = q * cos_q + _swap16(q) * sin_q
        k = k * cos + _swap16(k) * sin
    q_ref[...] = (q * (HEAD_DIM ** -0.5)).astype(q_ref.dtype)
    k_ref[...] = k.astype(k_ref.dtype)
    v_ref[...] = v.astype(v_ref.dtype)
    ga_ref[...] = _sigmoid(_dot(h, w_ref[:, GA_OFF:GB_OFF])).astype(ga_ref.dtype)
    gb_ref[...] = _sigmoid(_dot(h, w_ref[:, GB_OFF:ZR_OFF])).astype(gb_ref.dtype)
    zr_ref[...] = _dot(h, w_ref[:, ZR_OFF:W_IN_COLS])


def _in_proj(x, mod, g1, w_in, seq_len, mod_row, rope_tabs, kv_dtype):
    ntok = x.shape[0]
    tm = min(512, seq_len) if rope_tabs is not None else 512
    tiles_per_seq = max(seq_len // tm, 1)
    rope = rope_tabs is not None
    tok = lambda width: pl.BlockSpec((tm, width), lambda i: (i, 0))
    in_specs = [tok(D_MODEL),
                pl.BlockSpec((1, 6, D_MODEL), lambda i: (mod_row(i), 0, 0)),
                _const_spec((1, D_MODEL)),
                _const_spec((D_MODEL, W_IN_COLS))]
    args = [x, mod, g1, w_in]
    if rope:
        in_specs += [pl.BlockSpec((tm, 2 * HEAD_DIM), lambda i: (i % tiles_per_seq, 0))] * 2
        args += list(rope_tabs)
    out_shape = [jax.ShapeDtypeStruct((ntok, ATTN_WIDTH), BF16),
                 jax.ShapeDtypeStruct((ntok, KV_WIDTH), kv_dtype),
                 jax.ShapeDtypeStruct((ntok, KV_WIDTH), kv_dtype),
                 jax.ShapeDtypeStruct((ntok, D_MODEL), BF16),
                 jax.ShapeDtypeStruct((ntok, D_MODEL), BF16),
                 jax.ShapeDtypeStruct((ntok, RW_SHIFT_COLS), F32)]
    out_specs = [tok(ATTN_WIDTH), tok(KV_WIDTH), tok(KV_WIDTH), tok(D_MODEL), tok(D_MODEL), tok(RW_SHIFT_COLS)]
    return pl.pallas_call(
        functools.partial(_in_proj_kernel, rope=rope),
        grid=(ntok // tm,),
        in_specs=in_specs, out_specs=out_specs, out_shape=out_shape,
        compiler_params=_params("parallel"),
        name="in_proj_rope" if rope else "in_proj",
    )(*args)


def _softmax_pv(s, sink_col, v):
    m = jnp.maximum(jnp.max(s, axis=-1, keepdims=True), sink_col)
    p = jnp.exp(s - m)
    denom = jnp.sum(p, axis=-1, keepdims=True) + jnp.exp(sink_col - m)
    return _dot(p.astype(BF16), v) / denom


def _group_rows(q, g):
    return jnp.concatenate(
        [q[:, (g * GQA_GROUP + i) * HEAD_DIM:(g * GQA_GROUP + i + 1) * HEAD_DIM] for i in range(GQA_GROUP)], axis=0)


def _sink_rows(sink_ref, g, t):
    return jnp.concatenate(
        [jnp.full((t, 1), sink_ref[g * GQA_GROUP + i], F32) for i in range(GQA_GROUP)], axis=0)


def _ungroup(outs, t):
    return jnp.concatenate([o[i * t:(i + 1) * t, :] for o in outs for i in range(GQA_GROUP)], axis=1)


def _attn_ctx_kernel(sink_ref, q_ref, k_ref, v_ref, o_ref):
    q = q_ref[0]
    k = k_ref[0].astype(BF16)
    v = v_ref[0].astype(BF16)
    t = q.shape[0]
    outs = []
    for g in range(N_KV_HEADS):
        kg = k[:, g * HEAD_DIM:(g + 1) * HEAD_DIM]
        vg = v[:, g * HEAD_DIM:(g + 1) * HEAD_DIM]
        s = lax.dot_general(_group_rows(q, g), kg, _NT, preferred_element_type=F32)
        outs.append(_softmax_pv(s, _sink_rows(sink_ref, g, t), vg))
    o_ref[0] = _ungroup(outs, t).astype(o_ref.dtype)


def _attn_ctx(q, k, v, sink):
    b, t, _ = q.shape
    seq = lambda width: pl.BlockSpec((1, t, width), lambda i: (i, 0, 0))
    return pl.pallas_call(
        _attn_ctx_kernel,
        grid=(b,),
        in_specs=[pl.BlockSpec(memory_space=pltpu.SMEM), seq(ATTN_WIDTH), seq(KV_WIDTH), seq(KV_WIDTH)],
        out_specs=seq(ATTN_WIDTH),
        out_shape=jax.ShapeDtypeStruct((b, t, ATTN_WIDTH), BF16),
        compiler_params=_params("parallel"),
        name="attn_ctx",
    )(sink, q, k, v)


def _attn_lat_kernel(sink_ref, q_ref, kp_ref, kc_ref, kn_ref, vp_ref, vc_ref, vn_ref, kx_ref, vx_ref, o_ref):
    n = pl.program_id(1)
    nb = pl.num_programs(1)
    q = q_ref[0]
    k = jnp.concatenate([kp_ref[0], kc_ref[0], kn_ref[0], kx_ref[0].astype(BF16)], axis=0)
    v = jnp.concatenate([vp_ref[0], vc_ref[0], vn_ref[0], vx_ref[0].astype(BF16)], axis=0)
    rows = GQA_GROUP * BLOCK
    rq = lax.broadcasted_iota(jnp.int32, (rows, BLOCK), 0) % BLOCK
    col = lax.broadcasted_iota(jnp.int32, (rows, BLOCK), 1)
    keep_prev = col >= rq + jnp.where(n > 0, 0, BLOCK)
    keep_next = col <= rq - jnp.where(n < nb - 1, 0, BLOCK)
    outs = []
    for g in range(N_KV_HEADS):
        kg = k[:, g * HEAD_DIM:(g + 1) * HEAD_DIM]
        vg = v[:, g * HEAD_DIM:(g + 1) * HEAD_DIM]
        s = lax.dot_general(_group_rows(q, g), kg, _NT, preferred_element_type=F32)
        s = jnp.concatenate([jnp.where(keep_prev, s[:, :BLOCK], MASK_VALUE), s[:, BLOCK:2 * BLOCK],
                             jnp.where(keep_next, s[:, 2 * BLOCK:3 * BLOCK], MASK_VALUE), s[:, 3 * BLOCK:]], axis=1)
        outs.append(_softmax_pv(s, _sink_rows(sink_ref, g, BLOCK), vg))
    o_ref[0] = _ungroup(outs, BLOCK).astype(o_ref.dtype)


def _attn_lat(q, k, v, k_ctx, v_ctx, sink):
    assert WINDOW == BLOCK, "the block-triangular band masks assume one block of reach on either side"
    b, t, _ = q.shape
    nb = t // BLOCK
    p = k_ctx.shape[1]
    blk = lambda width, f: pl.BlockSpec((1, BLOCK, width), lambda i, n: (i, f(n), 0))
    prev = lambda n: jnp.maximum(n - 1, 0)
    cur = lambda n: n
    nxt = lambda n: jnp.minimum(n + 1, nb - 1)
    ctx = pl.BlockSpec((1, p, KV_WIDTH), lambda i, n: (i, 0, 0))
    return pl.pallas_call(
        _attn_lat_kernel,
        grid=(b, nb),
        in_specs=[pl.BlockSpec(memory_space=pltpu.SMEM), blk(ATTN_WIDTH, cur),
                  blk(KV_WIDTH, prev), blk(KV_WIDTH, cur), blk(KV_WIDTH, nxt),
                  blk(KV_WIDTH, prev), blk(KV_WIDTH, cur), blk(KV_WIDTH, nxt), ctx, ctx],
        out_specs=blk(ATTN_WIDTH, cur),
        out_shape=jax.ShapeDtypeStruct((b, t, ATTN_WIDTH), BF16),
        compiler_params=_params("parallel", "parallel"),
        name="attn_lat",
    )(sink, q, k, k, k, v, v, v, k_ctx, v_ctx)


def _rw_prep_kernel(z_ref, zp_ref, zn_ref, mu_ref, w2_ref, w0_ref, a2_ref, a0_ref, g2_ref, kk_ref, ka_ref,
                    rk_ref, bd_ref, lw_ref, kd_ref, bb_ref, a_ref, r_ref, v_ref, g_ref, bonus_ref):
    i = pl.program_id(1)
    nt = pl.num_programs(1)
    z = z_ref[0]
    tm = z.shape[0]
    row = lax.broadcasted_iota(jnp.int32, z.shape, 0)
    halo_prev = jnp.where(i > 0, zp_ref[0, SUBLANES - 1:SUBLANES, :], 0.0)
    halo_next = jnp.where(i < nt - 1, zn_ref[0, 0:1, :], 0.0)
    prev = jnp.where(row == 0, halo_prev, pltpu.roll(z, 1, 0))
    nxt = jnp.where(row == tm - 1, halo_next, pltpu.roll(z, tm - 1, 0))
    mu = mu_ref[...]
    z = z + mu[0:1, :] * (prev - z) + mu[1:2, :] * (nxt - z)

    r = z[:, 0:RWKV_WIDTH]
    k = z[:, RWKV_WIDTH:2 * RWKV_WIDTH]
    v = z[:, 2 * RWKV_WIDTH:3 * RWKV_WIDTH]
    zw = z[:, ZW_OFF:ZA_OFF]
    za = z[:, ZA_OFF:ZG_OFF]
    zg = z[:, ZG_OFF:RW_SHIFT_COLS]

    logit = w0_ref[...] + _dot(jnp.tanh(zw).astype(BF16), w2_ref[...])
    lw = -jnp.exp(F32(-0.5)) * _sigmoid(logit)
    a = _sigmoid(a0_ref[...] + _dot(za.astype(BF16), a2_ref[...]))
    g_ref[0] = _dot(_sigmoid(zg).astype(BF16), g2_ref[...])

    bd = bd_ref[...]
    kk = k * kk_ref[...]
    kk = kk * lax.rsqrt(jnp.maximum(_split_dot(kk * kk, bd), 1e-24))
    ka = ka_ref[...]
    ksum = jnp.zeros_like(k)
    for d in range(N_DIR):
        ad = a[:, d * RWKV_WIDTH:(d + 1) * RWKV_WIDTH]
        kd = k * (1.0 + (ad - 1.0) * ka)
        lw_ref[d, 0] = lw[:, d * RWKV_WIDTH:(d + 1) * RWKV_WIDTH]
        kd_ref[d, 0] = kd.astype(kd_ref.dtype)
        bb_ref[d, 0] = (ad * kk).astype(bb_ref.dtype)
        ksum = ksum + kd
    a_ref[0] = (-kk).astype(a_ref.dtype)
    r_ref[0] = r.astype(r_ref.dtype)
    v_ref[0] = v.astype(v_ref.dtype)
    bonus_ref[0] = _split_dot(r * ksum * rk_ref[...], bd) * v


def _rw_prep(zr, p):
    b, t, _ = zr.shape
    tm = 256
    nt = t // tm
    hb = tm // SUBLANES
    last_hb = t // SUBLANES - 1
    tok = lambda: pl.BlockSpec((1, tm, RWKV_WIDTH), lambda i, j: (i, j, 0))
    tok2 = lambda: pl.BlockSpec((N_DIR, 1, tm, RWKV_WIDTH), lambda i, j: (0, i, j, 0))
    one = jax.ShapeDtypeStruct((b, t, RWKV_WIDTH), F32)
    two = jax.ShapeDtypeStruct((N_DIR, b, t, RWKV_WIDTH), F32)
    one_bf = jax.ShapeDtypeStruct((b, t, RWKV_WIDTH), BF16)
    two_bf = jax.ShapeDtypeStruct((N_DIR, b, t, RWKV_WIDTH), BF16)
    in_specs = [pl.BlockSpec((1, tm, RW_SHIFT_COLS), lambda i, j: (i, j, 0)),
                pl.BlockSpec((1, SUBLANES, RW_SHIFT_COLS), lambda i, j: (i, jnp.maximum(j * hb - 1, 0), 0)),
                pl.BlockSpec((1, SUBLANES, RW_SHIFT_COLS), lambda i, j: (i, jnp.minimum((j + 1) * hb, last_hb), 0)),
                _const_spec((2, RW_SHIFT_COLS)),
                _const_spec((N_DIR * DECAY_RANK, N_DIR * RWKV_WIDTH)), _const_spec((1, N_DIR * RWKV_WIDTH)),
                _const_spec((N_DIR * ICLR_RANK, N_DIR * RWKV_WIDTH)), _const_spec((1, N_DIR * RWKV_WIDTH)),
                _const_spec((GATE_RANK, RWKV_WIDTH)),
                _const_spec((1, RWKV_WIDTH)), _const_spec((1, RWKV_WIDTH)), _const_spec((1, RWKV_WIDTH)),
                _const_spec((RWKV_WIDTH, RWKV_WIDTH))]
    return pl.pallas_call(
        _rw_prep_kernel,
        grid=(b, nt),
        in_specs=in_specs,
        out_specs=[tok2(), tok2(), tok2(), tok(), tok(), tok(), tok(), tok()],
        out_shape=[two, two_bf, two_bf, one_bf, one_bf, one_bf, one, one],
        compiler_params=_params("parallel", "parallel"),
        name="rw_prep",
    )(zr, zr, zr, p["mu"], p["w2"], p["w0"], p["a2"], p["a0"], p["g2"], p["k_k"], p["k_a"], p["r_k"], p["bd"])


def _rw_scan_kernel(*refs, has_s0, nchunks):
    if has_s0:
        s0_ref, refs = refs[0], refs[1:]
    (lw0_ref, kd0_ref, bb0_ref, a0_ref, r0_ref, v0_ref, lw1_ref, kd1_ref, bb1_ref, a1_ref, r1_ref, v1_ref,
     y0_ref, y1_ref, sf_ref, s_scr) = refs
    dir_refs = ((lw0_ref, kd0_ref, bb0_ref, a0_ref, r0_ref, v0_ref, y0_ref),
                (lw1_ref, kd1_ref, bb1_ref, a1_ref, r1_ref, v1_ref, y1_ref))
    i = pl.program_id(1)
    c = SCAN_CHUNK
    n = RWKV_HEAD_SIZE

    @pl.when(i == 0)
    def _():
        if has_s0:
            s_scr[...] = s0_ref[0]
        else:
            s_scr[...] = jnp.zeros_like(s_scr)

    diff = lax.broadcasted_iota(jnp.int32, (c, c), 0) - lax.broadcasted_iota(jnp.int32, (c, c), 1)
    incl = (diff >= 0, diff <= 0)
    strict = (diff > 0, diff < 0)
    diff2 = (lax.broadcasted_iota(jnp.int32, (c, 2 * c), 0) - lax.broadcasted_iota(jnp.int32, (c, 2 * c), 1) % c)
    incl2 = (diff2 >= 0, diff2 <= 0)
    chains = [(d, h) for d in range(N_DIR) for h in range(RWKV_HEADS)]
    nch = len(chains)
    dirs = [d for d, _ in chains]
    each = lambda f: [f(d, slice(h * n, (h + 1) * n)) for d, h in chains]

    def chunk(jj, carry):
        pre = []
        for d in range(N_DIR):
            lw_ref, kd_ref, bb_ref, a_ref, r_ref, v_ref, _ = dir_refs[d]
            j = jj if d == 0 else nchunks - 1 - jj
            rows = pl.ds(pl.multiple_of(j * c, c), c)
            lw = lw_ref[0, 0, rows, :]
            kd = kd_ref[0, 0, rows, :].astype(F32)
            bb = bb_ref[0, 0, rows, :].astype(F32)
            l1 = lw.astype(BF16)
            rem = lw - l1.astype(F32)
            l2 = rem.astype(BF16)
            l3 = (rem - l2.astype(F32)).astype(BF16)
            tri = incl[d].astype(BF16)
            cum = _dot(jnp.concatenate([tri, tri, tri], axis=1), jnp.concatenate([l1, l2, l3], axis=0))
            ctot = jnp.sum(lw, axis=0, keepdims=True)
            en = jnp.exp(-cum)
            eh = jnp.exp(ctot - cum)
            pre.append(dict(
                rows=rows,
                at=a_ref[0, rows, :].astype(F32) * jnp.exp(cum - lw),
                rt=r_ref[0, rows, :].astype(F32) * jnp.exp(cum),
                bt=(bb * en).astype(BF16), kt=(kd * en).astype(BF16),
                bh=(bb * eh).astype(BF16), kh=(kd * eh).astype(BF16),
                gam=jnp.exp(ctot), v=v_ref[0, rows, :]))

        at = each(lambda d, hs: pre[d]["at"][:, hs])
        rt = each(lambda d, hs: pre[d]["rt"][:, hs])
        vh = each(lambda d, hs: pre[d]["v"][:, hs])
        bh = each(lambda d, hs: pre[d]["bh"][:, hs])
        kh = each(lambda d, hs: pre[d]["kh"][:, hs])
        gam = each(lambda d, hs: pre[d]["gam"][:, hs])
        amat = each(lambda d, hs: lax.dot_general(
            jnp.concatenate([pre[d]["at"][:, hs], pre[d]["rt"][:, hs]], axis=0).astype(BF16),
            jnp.concatenate([pre[d]["bt"][:, hs], pre[d]["kt"][:, hs]], axis=0), _NT, preferred_element_type=F32))
        a_ak = [jnp.where(strict[dirs[q]], amat[q][:c, c:], 0.0).astype(BF16) for q in range(nch)]
        a_r = [jnp.where(incl2[dirs[q]], amat[q][c:, :], 0.0).astype(BF16) for q in range(nch)]
        x = [jnp.where(strict[dirs[q]], amat[q][:c, :c], 0.0) for q in range(nch)]
        z = [jnp.concatenate([at[q], _dot(a_ak[q], vh[q])], axis=1) for q in range(nch)]
        for step in range(NEUMANN_STEPS):
            xb = [x[q].astype(BF16) for q in range(nch)]
            if step < NEUMANN_STEPS - 1:
                prod = [_dot(xb[q], jnp.concatenate([z[q].astype(BF16), xb[q]], axis=1)) for q in range(nch)]
                z = [z[q] + prod[q][:, :2 * n] for q in range(nch)]
                x = [prod[q][:, 2 * n:] for q in range(nch)]
            else:
                z = [z[q] + _dot(xb[q], z[q].astype(BF16)) for q in range(nch)]
        zero = jnp.zeros((c, n), BF16)
        rmat = [jnp.concatenate([z[q].astype(BF16), jnp.concatenate([zero, vh[q]], axis=1)], axis=0)
                for q in range(nch)]
        qy = [_dot(a_r[q], rmat[q]) for q in range(nch)]
        qm = [(rt[q] + qy[q][:, :n]).astype(BF16) for q in range(nch)]
        mn = [lax.dot_general(rmat[q], jnp.concatenate([bh[q], kh[q]], axis=0), _TN, preferred_element_type=F32)
              for q in range(nch)]
        s = [s_scr[d, h] for d, h in chains]
        sb = [s[q].astype(BF16) for q in range(nch)]
        ys = [lax.dot_general(qm[q], sb[q], _NT, preferred_element_type=F32) + qy[q][:, n:] for q in range(nch)]
        for q, (d, h) in enumerate(chains):
            s_scr[d, h] = s[q] * gam[q] + _dot(sb[q], mn[q][:n, :].astype(BF16)) + mn[q][n:, :]
        for d in range(N_DIR):
            y_ref = dir_refs[d][-1]
            y_ref[0, pre[d]["rows"], :] = jnp.concatenate(ys[d * RWKV_HEADS:(d + 1) * RWKV_HEADS], axis=1)
        return carry

    lax.fori_loop(0, nchunks, chunk, 0)

    @pl.when(i == pl.num_programs(1) - 1)
    def _():
        sf_ref[0] = s_scr[...]


def _rw_scan(s0, lw, kd, bb, a, r, v):
    _, b, t, _ = lw.shape
    tb = min(t, 512)
    nblk = t // tb
    blk = (lambda i: i, lambda i: nblk - 1 - i)
    dir_spec = lambda d: pl.BlockSpec((1, 1, tb, RWKV_WIDTH), lambda bi, i: (d, bi, blk[d](i), 0))
    tok_spec = lambda d: pl.BlockSpec((1, tb, RWKV_WIDTH), lambda bi, i: (bi, blk[d](i), 0))
    st_spec = pl.BlockSpec((1, N_DIR, RWKV_HEADS, RWKV_HEAD_SIZE, RWKV_HEAD_SIZE), lambda bi, i: (bi, 0, 0, 0, 0))
    has_s0 = s0 is not None
    in_specs, args = [], []
    if has_s0:
        in_specs.append(st_spec)
        args.append(s0)
    for d in range(N_DIR):
        in_specs += [dir_spec(d), dir_spec(d), dir_spec(d), tok_spec(d), tok_spec(d), tok_spec(d)]
        args += [lw, kd, bb, a, r, v]
    y_shape = jax.ShapeDtypeStruct((b, t, RWKV_WIDTH), F32)
    return pl.pallas_call(
        functools.partial(_rw_scan_kernel, has_s0=has_s0, nchunks=tb // SCAN_CHUNK),
        grid=(b, nblk),
        in_specs=in_specs,
        out_specs=[tok_spec(0), tok_spec(1), st_spec],
        out_shape=[y_shape, y_shape,
                   jax.ShapeDtypeStruct((b, N_DIR, RWKV_HEADS, RWKV_HEAD_SIZE, RWKV_HEAD_SIZE), F32)],
        scratch_shapes=[pltpu.VMEM((N_DIR, RWKV_HEADS, RWKV_HEAD_SIZE, RWKV_HEAD_SIZE), F32)],
        compiler_params=_params("parallel", "arbitrary"),
        name="rw_scan",
    )(*args)


def _post_kernel(x_ref, mod_ref, attn_ref, y0_ref, y1_ref, bonus_ref, g_ref, ga_ref, gb_ref, lng_ref, lnb_ref,
                 bd_ref, wa_ref, wb_ref, wo_ref, o_ref):
    m = mod_ref[0]
    y = y0_ref[0] + y1_ref[0]
    bd = bd_ref[...]
    inv_n = 1.0 / RWKV_HEAD_SIZE
    mean = _split_dot(y, bd) * inv_n
    yc = y - mean
    var = _split_dot(yc * yc, bd) * inv_n
    yn = yc * lax.rsqrt(var + GN_EPS) * lng_ref[...] + lnb_ref[...]
    rw = ((yn + bonus_ref[0]) * g_ref[0]).astype(BF16)
    merged = (ga_ref[0].astype(F32) * _dot(attn_ref[0], wa_ref[...])
              + gb_ref[0].astype(F32) * _dot(rw, wb_ref[...]))
    o_ref[0] = x_ref[0] + m[2:3, :] * _dot(merged.astype(BF16), wo_ref[...])


def _post(x, mod, mod_row, attn, y0, y1, bonus, g, ga, gb, p):
    b, t, _ = x.shape
    tm = min(t, 512)
    tok = lambda width: pl.BlockSpec((1, tm, width), lambda i, j: (i, j, 0))
    in_specs = [tok(D_MODEL),
                pl.BlockSpec((1, 6, D_MODEL), lambda i, j: (mod_row(i), 0, 0)),
                tok(ATTN_WIDTH),
                tok(RWKV_WIDTH), tok(RWKV_WIDTH), tok(RWKV_WIDTH), tok(RWKV_WIDTH), tok(D_MODEL), tok(D_MODEL),
                _const_spec((1, RWKV_WIDTH)), _const_spec((1, RWKV_WIDTH)),
                _const_spec((RWKV_WIDTH, RWKV_WIDTH)),
                _const_spec((ATTN_WIDTH, D_MODEL)), _const_spec((RWKV_WIDTH, D_MODEL)),
                _const_spec((D_MODEL, D_MODEL))]
    return pl.pallas_call(
        _post_kernel,
        grid=(b, t // tm),
        in_specs=in_specs,
        out_specs=tok(D_MODEL),
        out_shape=jax.ShapeDtypeStruct((b, t, D_MODEL), F32),
        compiler_params=_params("parallel", "parallel"),
        name="post",
    )(x, mod, attn, y0, y1, bonus, g, ga, gb, p["ln_g"], p["ln_b"], p["bd"], p["w_proj_a"], p["w_proj_b"], p["w_out"])


def _ffn_kernel(x_ref, xp_ref, xn_ref, mod_ref, g2_ref, wup_ref, cw_ref, cb_ref, wdn_ref, gf_ref, o_ref):
    i = pl.program_id(1)
    nt = pl.num_programs(1)
    m = mod_ref[0]
    g2 = g2_ref[...]

    def norm_mod(x):
        ms = jnp.mean(x * x, axis=-1, keepdims=True)
        return x * lax.rsqrt(ms + NORM_EPS) * g2 * (1.0 + m[4:5, :]) + m[3:4, :]

    x = x_ref[0]
    tm = x.shape[0]
    hp = jnp.where(i > 0, norm_mod(xp_ref[0]), 0.0)
    hn = jnp.where(i < nt - 1, norm_mod(xn_ref[0]), 0.0)
    h = jnp.concatenate([hp, norm_mod(x), hn], axis=0).astype(BF16)
    cw = cw_ref[...]
    cb = cb_ref[...]
    halo = 2 * SUBLANES
    nrb = tm // FF_ROWS

    def up_piece(j, r):
        lo = j * FF_CHUNK
        hr = h[r * FF_ROWS:(r + 1) * FF_ROWS + (halo if r == nrb - 1 else 0), :]
        return _dot(hr, wup_ref[:, lo:lo + FF_CHUNK]), _dot(hr, wup_ref[:, D_FF + lo:D_FF + lo + FF_CHUNK])

    def conv(pieces, half, r, lo):
        if r == nrb - 1:
            u = pieces[r][half]
        else:
            u = jnp.concatenate([pieces[r][half], pieces[r + 1][half][:halo, :]], axis=0)
        w = cw[:, lo:lo + FF_CHUNK]
        out = (pltpu.roll(u, 1, 0) * w[0:1, :] + u * w[1:2, :] + pltpu.roll(u, FF_ROWS + halo - 1, 0) * w[2:3, :]
               + cb[:, lo:lo + FF_CHUNK])
        return out[SUBLANES:SUBLANES + FF_ROWS, :]

    nchunks = D_FF // FF_CHUNK
    acc = [jnp.zeros((FF_ROWS, D_MODEL), F32) for _ in range(nrb)]
    cur = [up_piece(0, r) for r in range(nrb)]
    for j in range(nchunks):
        lo = j * FF_CHUNK
        nxt = []
        for r in range(nrb):
            val = conv(cur, 0, r, lo)
            gate = conv(cur, 1, r, D_FF + lo)
            act = (gate * _sigmoid(gate) * val).astype(BF16)
            acc[r] = acc[r] + _dot(act, wdn_ref[lo:lo + FF_CHUNK, :])
            if j + 1 < nchunks:
                nxt.append(up_piece(j + 1, r))
        cur = nxt
    x2 = x + m[5:6, :] * jnp.concatenate(acc, axis=0)
    ms = jnp.mean(x2 * x2, axis=-1, keepdims=True)
    o_ref[0] = x2 * lax.rsqrt(ms + NORM_EPS) * gf_ref[...]


def _ffn(x, mod, mod_row, p):
    b, t, _ = x.shape
    tm = min(t, 512)
    hb = tm // SUBLANES
    last_hb = t // SUBLANES - 1
    in_specs = [pl.BlockSpec((1, tm, D_MODEL), lambda i, j: (i, j, 0)),
                pl.BlockSpec((1, SUBLANES, D_MODEL), lambda i, j: (i, jnp.maximum(j * hb - 1, 0), 0)),
                pl.BlockSpec((1, SUBLANES, D_MODEL), lambda i, j: (i, jnp.minimum((j + 1) * hb, last_hb), 0)),
                pl.BlockSpec((1, 6, D_MODEL), lambda i, j: (mod_row(i), 0, 0)),
                _const_spec((1, D_MODEL)),
                _const_spec((D_MODEL, 2 * D_FF)), _const_spec((3, 2 * D_FF)), _const_spec((1, 2 * D_FF)),
                _const_spec((D_FF, D_MODEL)), _const_spec((1, D_MODEL))]
    return pl.pallas_call(
        _ffn_kernel,
        grid=(b, t // tm),
        in_specs=in_specs,
        out_specs=pl.BlockSpec((1, tm, D_MODEL), lambda i, j: (i, j, 0)),
        out_shape=jax.ShapeDtypeStruct((b, t, D_MODEL), F32),
        compiler_params=_params("parallel", "parallel"),
        name="ffn",
    )(x, x, x, mod, p["g_norm2"], p["w_ffn_up"], p["conv_w"], p["conv_b"], p["w_ffn_down"], p["g_final"])


def _rope_tables(t):
    rows = t // GRID_W
    row = jnp.repeat(jnp.arange(rows), GRID_W).astype(F32)
    col = jnp.tile(jnp.arange(GRID_W), rows).astype(F32)
    freqs = ROPE_BASE ** (-jnp.arange(ROPE_PAIRS, dtype=F32) / ROPE_PAIRS)
    ar = row[:, None] * freqs
    ac = col[:, None] * freqs
    cos = jnp.concatenate([jnp.cos(ar), jnp.cos(ar), jnp.cos(ac), jnp.cos(ac)], axis=1)
    sin = jnp.concatenate([-jnp.sin(ar), jnp.sin(ar), -jnp.sin(ac), jnp.sin(ac)], axis=1)
    return jnp.tile(cos, (1, 2)), jnp.tile(sin, (1, 2))


def _block_diag2(w):
    z = jnp.zeros_like(w[0])
    return jnp.concatenate([jnp.concatenate([w[0], z], axis=1), jnp.concatenate([z, w[1]], axis=1)], axis=0)


def _trunk(x, mod, mod_row_tile, mod_row_seq, p, ctx):
    b, t, _ = x.shape
    rope = _rope_tables(t) if ctx is not None else None
    kv_dtype = BF16 if ctx is not None else F32
    q, k, v, ga, gb, zr = _in_proj(x.reshape(b * t, D_MODEL), mod, p["g_norm1"], p["w_in"], t,
                                   mod_row_tile, rope, kv_dtype)
    seq = lambda arr: arr.reshape(b, t, arr.shape[-1])
    q, k, v, ga, gb, zr = seq(q), seq(k), seq(v), seq(ga), seq(gb), seq(zr)
    if ctx is None:
        attn = _attn_ctx(q, k, v, p["sink"])
        s0 = None
    else:
        k_ctx, v_ctx, s0 = ctx
        attn = _attn_lat(q, k, v, k_ctx, v_ctx, p["sink"])
    lw, kd, bb, a, r, vr, g, bonus = _rw_prep(zr, p)
    y0, y1, s_final = _rw_scan(s0, lw, kd, bb, a, r, vr)
    x1 = _post(x, mod, mod_row_seq, attn, y0, y1, bonus, g, ga, gb, p)
    out = _ffn(x1, mod, mod_row_seq, p)
    return out, k, v, s_final


def kernel(x_prompt, x_sample, c, cache_k, cache_v, state_rwkv, c_ctx, w_ada, b_ada, g_norm1, w_in, attn_sink, w_proj_a, w_proj_b, rwkv_mu, rwkv_w0, rwkv_w2, rwkv_a0, rwkv_a2, rwkv_k_k, rwkv_k_a, rwkv_r_k, rwkv_g2, rwkv_ln_g, rwkv_ln_b, w_out, g_norm2, w_ffn_up, ffn_conv_w, ffn_conv_b, w_ffn_down, g_final):
    depth = w_ada.shape[0]
    assert depth == 1, "single trunk layer"
    l = 0
    nb, seq, _ = x_prompt.shape
    db, dseq, _ = x_sample.shape
    past = cache_k.shape[2]

    head_id = jnp.arange(RWKV_WIDTH) // RWKV_HEAD_SIZE
    p = dict(
        g_norm1=g_norm1[l].reshape(1, D_MODEL),
        w_in=w_in[l].astype(BF16),
        sink=attn_sink[l],
        mu=rwkv_mu[l],
        w2=_block_diag2(rwkv_w2[l]).astype(BF16),
        w0=rwkv_w0[l].reshape(1, N_DIR * RWKV_WIDTH),
        a2=_block_diag2(rwkv_a2[l]).astype(BF16),
        a0=rwkv_a0[l].reshape(1, N_DIR * RWKV_WIDTH),
        g2=rwkv_g2[l].astype(BF16),
        k_k=rwkv_k_k[l].reshape(1, RWKV_WIDTH),
        k_a=rwkv_k_a[l].reshape(1, RWKV_WIDTH),
        r_k=rwkv_r_k[l].reshape(1, RWKV_WIDTH),
        bd=(head_id[:, None] == head_id[None, :]).astype(BF16),
        ln_g=rwkv_ln_g[l].reshape(1, RWKV_WIDTH),
        ln_b=rwkv_ln_b[l].reshape(1, RWKV_WIDTH),
        w_proj_a=w_proj_a[l].astype(BF16),
        w_proj_b=w_proj_b[l].astype(BF16),
        w_out=w_out[l].astype(BF16),
        g_norm2=g_norm2[l].reshape(1, D_MODEL),
        w_ffn_up=w_ffn_up[l].astype(BF16),
        conv_w=ffn_conv_w[l],
        conv_b=ffn_conv_b[l].reshape(1, 2 * D_FF),
        w_ffn_down=w_ffn_down[l].astype(BF16),
        g_final=g_final.reshape(1, D_MODEL),
    )

    ctx_row = db
    mod_rows = 2 * SUBLANES
    cond = jnp.zeros((mod_rows, D_MODEL), F32).at[:db].set(c).at[ctx_row].set(c_ctx)
    mod = _modulation(cond, w_ada[l], b_ada[l]).reshape(mod_rows, 6, D_MODEL)

    y_prompt, kc, vc, sc = _trunk(x_prompt, mod, lambda i: ctx_row, lambda i: ctx_row, p, None)

    lat_tiles = max(dseq // 512, 1)
    ctx = (cache_k[:, l].reshape(db, past, KV_WIDTH), cache_v[:, l].reshape(db, past, KV_WIDTH), state_rwkv[:, l])
    y_sample, _, _, _ = _trunk(x_sample, mod, lambda i: i // lat_tiles, lambda i: i, p, ctx)

    new_cache_k = kc.reshape(nb, 1, seq, N_KV_HEADS, HEAD_DIM)
    new_cache_v = vc.reshape(nb, 1, seq, N_KV_HEADS, HEAD_DIM)
    new_state = sc.reshape(nb, 1, N_DIR, RWKV_HEADS, RWKV_HEAD_SIZE, RWKV_HEAD_SIZE)
    return (y_prompt, y_sample, new_cache_k, new_cache_v, new_state)
```

```python
import functools

import jax
import jax.numpy as jnp
from jax import lax
from jax.experimental import pallas as pl
from jax.experimental.pallas import tpu as pltpu

F32 = jnp.float32
BF16 = jnp.bfloat16

D_MODEL = 1024
GRID_W = 64
HEAD_DIM = 64
N_Q_HEADS = 8
N_KV_HEADS = 2
GQA_GROUP = N_Q_HEADS // N_KV_HEADS
ATTN_WIDTH = N_Q_HEADS * HEAD_DIM
KV_WIDTH = N_KV_HEADS * HEAD_DIM
WINDOW = 128
BLOCK = 128
ROPE_BASE = 10000.0
ROPE_PAIRS = HEAD_DIM // 4
RWKV_HEADS = 8
RWKV_HEAD_SIZE = 64
RWKV_WIDTH = RWKV_HEADS * RWKV_HEAD_SIZE
N_DIR = 2
DECAY_RANK = 64
ICLR_RANK = 64
GATE_RANK = 128
RW_SHIFT_COLS = 3 * RWKV_WIDTH + N_DIR * DECAY_RANK + N_DIR * ICLR_RANK + GATE_RANK
D_FF = 2816
NORM_EPS = 1e-6
GN_EPS = 64e-5
MASK_VALUE = -1e30

Q_OFF = 0
K_OFF = ATTN_WIDTH
V_OFF = K_OFF + KV_WIDTH
GA_OFF = V_OFF + KV_WIDTH
GB_OFF = GA_OFF + D_MODEL
ZR_OFF = GB_OFF + D_MODEL
W_IN_COLS = ZR_OFF + RW_SHIFT_COLS

ZW_OFF = 3 * RWKV_WIDTH
ZA_OFF = ZW_OFF + N_DIR * DECAY_RANK
ZG_OFF = ZA_OFF + N_DIR * ICLR_RANK

SCAN_CHUNK = 64
NEUMANN_STEPS = 6
SUBLANES = 8
FF_CHUNK = 256
FF_ROWS = 128
VMEM_LIMIT = 56 * 1024 * 1024

_NT = (((1,), (1,)), ((), ()))
_TN = (((0,), (0,)), ((), ()))


def _params(*sem):
    return pltpu.CompilerParams(dimension_semantics=sem, vmem_limit_bytes=VMEM_LIMIT)


def _const_spec(shape):
    nd = len(shape)
    return pl.BlockSpec(shape, lambda *_: (0,) * nd, pipeline_mode=pl.Buffered(1))


def _dot(a, b):
    return jnp.dot(a, b, preferred_element_type=F32)


def _split_dot(x, m):
    hi = x.astype(BF16)
    lo = (x - hi.astype(F32)).astype(BF16)
    return _dot(hi, m) + _dot(lo, m)


def _sigmoid(x):
    return 1.0 / (1.0 + jnp.exp(-x))


def _mod_kernel(c_ref, w_ref, b_ref, o_ref):
    c = c_ref[...]
    s = (c * _sigmoid(c)).astype(BF16)
    o_ref[...] = _dot(s, w_ref[...].astype(BF16)) + b_ref[...]


def _modulation(cond, w_ada, b_ada):
    rows = cond.shape[0]
    n = w_ada.shape[1]
    tn = D_MODEL
    return pl.pallas_call(
        _mod_kernel,
        grid=(n // tn,),
        in_specs=[pl.BlockSpec((rows, D_MODEL), lambda j: (0, 0)),
                  pl.BlockSpec((D_MODEL, tn), lambda j: (0, j)),
                  pl.BlockSpec((1, tn), lambda j: (0, j))],
        out_specs=pl.BlockSpec((rows, tn), lambda j: (0, j)),
        out_shape=jax.ShapeDtypeStruct((rows, n), F32),
        compiler_params=_params("parallel"),
        name="modulation",
    )(cond, w_ada, b_ada.reshape(1, n))


def _swap16(x):
    w = x.shape[1]
    lane = lax.broadcasted_iota(jnp.int32, x.shape, 1)
    first = (lane % 32) < 16
    return jnp.where(first, pltpu.roll(x, w - 16, 1), pltpu.roll(x, 16, 1))


def _in_proj_kernel(*refs, rope):
    if rope:
        x_ref, mod_ref, g_ref, w_ref, cos_ref, sin_ref, q_ref, k_ref, v_ref, ga_ref, gb_ref, zr_ref = refs
    else:
        x_ref, mod_ref, g_ref, w_ref, q_ref, k_ref, v_ref, ga_ref, gb_ref, zr_ref = refs
    x = x_ref[...]
    m = mod_ref[0]
    ms = jnp.mean(x * x, axis=-1, keepdims=True)
    h = x * lax.rsqrt(ms + NORM_EPS) * g_ref[...]
    h = (h * (1.0 + m[1:2, :]) + m[0:1, :]).astype(BF16)

    q = _dot(h, w_ref[:, Q_OFF:K_OFF])
    k = _dot(h, w_ref[:, K_OFF:V_OFF])
    v = _dot(h, w_ref[:, V_OFF:GA_OFF])
    if rope:
        cos = cos_ref[...]
        sin = sin_ref[...]
        reps = ATTN_WIDTH // cos.shape[1]
        cos_q = jnp.concatenate([cos] * reps, axis=1)
        sin_q = jnp.concatenate([sin] * reps, axis=1)
        q = q * cos_q + _swap16(q) * sin_q
        k = k * cos + _swap16(k) * sin
    q_ref[...] = (q * (HEAD_DIM ** -0.5)).astype(q_ref.dtype)
    k_ref[...] = k.astype(k_ref.dtype)
    v_ref[...] = v.astype(v_ref.dtype)
    ga_ref[...] = _sigmoid(_dot(h, w_ref[:, GA_OFF:GB_OFF])).astype(ga_ref.dtype)
    gb_ref[...] = _sigmoid(_dot(h, w_ref[:, GB_OFF:ZR_OFF])).astype(gb_ref.dtype)
    zr_ref[...] = _dot(h, w_ref[:, ZR_OFF:W_IN_COLS])


def _in_proj(x, mod, g1, w_in, seq_len, mod_row, rope_tabs, kv_dtype):
    ntok = x.shape[0]
    tm = min(512, seq_len) if rope_tabs is not None else 512
    tiles_per_seq = max(seq_len // tm, 1)
    rope = rope_tabs is not None
    tok = lambda width: pl.BlockSpec((tm, width), lambda i: (i, 0))
    in_specs = [tok(D_MODEL),
                pl.BlockSpec((1, 6, D_MODEL), lambda i: (mod_row(i), 0, 0)),
                _const_spec((1, D_MODEL)),
                _const_spec((D_MODEL, W_IN_COLS))]
    args = [x, mod, g1, w_in]
    if rope:
        in_specs += [pl.BlockSpec((tm, 2 * HEAD_DIM), lambda i: (i % tiles_per_seq, 0))] * 2
        args += list(rope_tabs)
    out_shape = [jax.ShapeDtypeStruct((ntok, ATTN_WIDTH), BF16),
                 jax.ShapeDtypeStruct((ntok, KV_WIDTH), kv_dtype),
                 jax.ShapeDtypeStruct((ntok, KV_WIDTH), kv_dtype),
                 jax.ShapeDtypeStruct((ntok, D_MODEL), BF16),
                 jax.ShapeDtypeStruct((ntok, D_MODEL), BF16),
                 jax.ShapeDtypeStruct((ntok, RW_SHIFT_COLS), F32)]
    out_specs = [tok(ATTN_WIDTH), tok(KV_WIDTH), tok(KV_WIDTH), tok(D_MODEL), tok(D_MODEL), tok(RW_SHIFT_COLS)]
    return pl.pallas_call(
        functools.partial(_in_proj_kernel, rope=rope),
        grid=(ntok // tm,),
        in_specs=in_specs, out_specs=out_specs, out_shape=out_shape,
        compiler_params=_params("parallel"),
        name="in_proj_rope" if rope else "in_proj",
    )(*args)


def _softmax_pv(s, sink_col, v):
    m = jnp.maximum(jnp.max(s, axis=-1, keepdims=True), sink_col)
    p = jnp.exp(s - m)
    denom = jnp.sum(p, axis=-1, keepdims=True) + jnp.exp(sink_col - m)
    return _dot(p.astype(BF16), v) / denom


def _group_rows(q, g):
    return jnp.concatenate(
        [q[:, (g * GQA_GROUP + i) * HEAD_DIM:(g * GQA_GROUP + i + 1) * HEAD_DIM] for i in range(GQA_GROUP)], axis=0)


def _sink_rows(sink_ref, g, t):
    return jnp.concatenate(
        [jnp.full((t, 1), sink_ref[g * GQA_GROUP + i], F32) for i in range(GQA_GROUP)], axis=0)


def _ungroup(outs, t):
    return jnp.concatenate([o[i * t:(i + 1) * t, :] for o in outs for i in range(GQA_GROUP)], axis=1)


def _attn_ctx_kernel(sink_ref, q_ref, k_ref, v_ref, o_ref):
    q = q_ref[0]
    k = k_ref[0].astype(BF16)
    v = v_ref[0].astype(BF16)
    t = q.shape[0]
    outs = []
    for g in range(N_KV_HEADS):
        kg = k[:, g * HEAD_DIM:(g + 1) * HEAD_DIM]
        vg = v[:, g * HEAD_DIM:(g + 1) * HEAD_DIM]
        s = lax.dot_general(_group_rows(q, g), kg, _NT, preferred_element_type=F32)
        outs.append(_softmax_pv(s, _sink_rows(sink_ref, g, t), vg))
    o_ref[0] = _ungroup(outs, t).astype(o_ref.dtype)


def _attn_ctx(q, k, v, sink):
    b, t, _ = q.shape
    seq = lambda width: pl.BlockSpec((1, t, width), lambda i: (i, 0, 0))
    return pl.pallas_call(
        _attn_ctx_kernel,
        grid=(b,),
        in_specs=[pl.BlockSpec(memory_space=pltpu.SMEM), seq(ATTN_WIDTH), seq(KV_WIDTH), seq(KV_WIDTH)],
        out_specs=seq(ATTN_WIDTH),
        out_shape=jax.ShapeDtypeStruct((b, t, ATTN_WIDTH), BF16),
        compiler_params=_params("parallel"),
        name="attn_ctx",
    )(sink, q, k, v)


def _attn_lat_kernel(sink_ref, q_ref, kp_ref, kc_ref, kn_ref, vp_ref, vc_ref, vn_ref, kx_ref, vx_ref, o_ref):
    n = pl.program_id(1)
    nb = pl.num_programs(1)
    q = q_ref[0]
    k = jnp.concatenate([kp_ref[0], kc_ref[0], kn_ref[0], kx_ref[0].astype(BF16)], axis=0)
    v = jnp.concatenate([vp_ref[0], vc_ref[0], vn_ref[0], vx_ref[0].astype(BF16)], axis=0)
    rows = GQA_GROUP * BLOCK
    rq = lax.broadcasted_iota(jnp.int32, (rows, BLOCK), 0) % BLOCK
    col = lax.broadcasted_iota(jnp.int32, (rows, BLOCK), 1)
    keep_prev = col >= rq + jnp.where(n > 0, 0, BLOCK)
    keep_next = col <= rq - jnp.where(n < nb - 1, 0, BLOCK)
    outs = []
    for g in range(N_KV_HEADS):
        kg = k[:, g * HEAD_DIM:(g + 1) * HEAD_DIM]
        vg = v[:, g * HEAD_DIM:(g + 1) * HEAD_DIM]
        s = lax.dot_general(_group_rows(q, g), kg, _NT, preferred_element_type=F32)
        s = jnp.concatenate([jnp.where(keep_prev, s[:, :BLOCK], MASK_VALUE), s[:, BLOCK:2 * BLOCK],
                             jnp.where(keep_next, s[:, 2 * BLOCK:3 * BLOCK], MASK_VALUE), s[:, 3 * BLOCK:]], axis=1)
        outs.append(_softmax_pv(s, _sink_rows(sink_ref, g, BLOCK), vg))
    o_ref[0] = _ungroup(outs, BLOCK).astype(o_ref.dtype)


def _attn_lat(q, k, v, k_ctx, v_ctx, sink):
    assert WINDOW == BLOCK, "the block-triangular band masks assume one block of reach on either side"
    b, t, _ = q.shape
    nb = t // BLOCK
    p = k_ctx.shape[1]
    blk = lambda width, f: pl.BlockSpec((1, BLOCK, width), lambda i, n: (i, f(n), 0))
    prev = lambda n: jnp.maximum(n - 1, 0)
    cur = lambda n: n
    nxt = lambda n: jnp.minimum(n + 1, nb - 1)
    ctx = pl.BlockSpec((1, p, KV_WIDTH), lambda i, n: (i, 0, 0))
    return pl.pallas_call(
        _attn_lat_kernel,
        grid=(b, nb),
        in_specs=[pl.BlockSpec(memory_space=pltpu.SMEM), blk(ATTN_WIDTH, cur),
                  blk(KV_WIDTH, prev), blk(KV_WIDTH, cur), blk(KV_WIDTH, nxt),
                  blk(KV_WIDTH, prev), blk(KV_WIDTH, cur), blk(KV_WIDTH, nxt), ctx, ctx],
        out_specs=blk(ATTN_WIDTH, cur),
        out_shape=jax.ShapeDtypeStruct((b, t, ATTN_WIDTH), BF16),
        compiler_params=_params("parallel", "parallel"),
        name="attn_lat",
    )(sink, q, k, k, k, v, v, v, k_ctx, v_ctx)


def _rw_prep_kernel(z_ref, zp_ref, zn_ref, mu_ref, w2_ref, w0_ref, a2_ref, a0_ref, g2_ref, kk_ref, ka_ref,
                    rk_ref, bd_ref, lw_ref, kd_ref, bb_ref, a_ref, r_ref, v_ref, g_ref, bonus_ref):
    i = pl.program_id(1)
    nt = pl.num_programs(1)
    z = z_ref[0]
    tm = z.shape[0]
    row = lax.broadcasted_iota(jnp.int32, z.shape, 0)
    halo_prev = jnp.where(i > 0, zp_ref[0, SUBLANES - 1:SUBLANES, :], 0.0)
    halo_next = jnp.where(i < nt - 1, zn_ref[0, 0:1, :], 0.0)
    prev = jnp.where(row == 0, halo_prev, pltpu.roll(z, 1, 0))
    nxt = jnp.where(row == tm - 1, halo_next, pltpu.roll(z, tm - 1, 0))
    mu = mu_ref[...]
    z = z + mu[0:1, :] * (prev - z) + mu[1:2, :] * (nxt - z)

    r = z[:, 0:RWKV_WIDTH]
    k = z[:, RWKV_WIDTH:2 * RWKV_WIDTH]
    v = z[:, 2 * RWKV_WIDTH:3 * RWKV_WIDTH]
    zw = z[:, ZW_OFF:ZA_OFF]
    za = z[:, ZA_OFF:ZG_OFF]
    zg = z[:, ZG_OFF:RW_SHIFT_COLS]

    logit = w0_ref[...] + _dot(jnp.tanh(zw).astype(BF16), w2_ref[...])
    lw = -jnp.exp(F32(-0.5)) * _sigmoid(logit)
    a = _sigmoid(a0_ref[...] + _dot(za.astype(BF16), a2_ref[...]))
    g_ref[0] = _dot(_sigmoid(zg).astype(BF16), g2_ref[...])

    bd = bd_ref[...]
    kk = k * kk_ref[...]
    kk = kk * lax.rsqrt(jnp.maximum(_split_dot(kk * kk, bd), 1e-24))
    ka = ka_ref[...]
    ksum = jnp.zeros_like(k)
    for d in range(N_DIR):
        ad = a[:, d * RWKV_WIDTH:(d + 1) * RWKV_WIDTH]
        kd = k * (1.0 + (ad - 1.0) * ka)
        lw_ref[d, 0] = lw[:, d * RWKV_WIDTH:(d + 1) * RWKV_WIDTH]
        kd_ref[d, 0] = kd.astype(kd_ref.dtype)
        bb_ref[d, 0] = (ad * kk).astype(bb_ref.dtype)
        ksum = ksum + kd
    a_ref[0] = (-kk).astype(a_ref.dtype)
    r_ref[0] = r.astype(r_ref.dtype)
    v_ref[0] = v.astype(v_ref.dtype)
    bonus_ref[0] = _split_dot(r * ksum * rk_ref[...], bd) * v


def _rw_prep(zr, p):
    b, t, _ = zr.shape
    tm = 256
    nt = t // tm
    hb = tm // SUBLANES
    last_hb = t // SUBLANES - 1
    tok = lambda: pl.BlockSpec((1, tm, RWKV_WIDTH), lambda i, j: (i, j, 0))
    tok2 = lambda: pl.BlockSpec((N_DIR, 1, tm, RWKV_WIDTH), lambda i, j: (0, i, j, 0))
    one = jax.ShapeDtypeStruct((b, t, RWKV_WIDTH), F32)
    two = jax.ShapeDtypeStruct((N_DIR, b, t, RWKV_WIDTH), F32)
    one_bf = jax.ShapeDtypeStruct((b, t, RWKV_WIDTH), BF16)
    two_bf = jax.ShapeDtypeStruct((N_DIR, b, t, RWKV_WIDTH), BF16)
    in_specs = [pl.BlockSpec((1, tm, RW_SHIFT_COLS), lambda i, j: (i, j, 0)),
                pl.BlockSpec((1, SUBLANES, RW_SHIFT_COLS), lambda i, j: (i, jnp.maximum(j * hb - 1, 0), 0)),
                pl.BlockSpec((1, SUBLANES, RW_SHIFT_COLS), lambda i, j: (i, jnp.minimum((j + 1) * hb, last_hb), 0)),
                _const_spec((2, RW_SHIFT_COLS)),
                _const_spec((N_DIR * DECAY_RANK, N_DIR * RWKV_WIDTH)), _const_spec((1, N_DIR * RWKV_WIDTH)),
                _const_spec((N_DIR * ICLR_RANK, N_DIR * RWKV_WIDTH)), _const_spec((1, N_DIR * RWKV_WIDTH)),
                _const_spec((GATE_RANK, RWKV_WIDTH)),
                _const_spec((1, RWKV_WIDTH)), _const_spec((1, RWKV_WIDTH)), _const_spec((1, RWKV_WIDTH)),
                _const_spec((RWKV_WIDTH, RWKV_WIDTH))]
    return pl.pallas_call(
        _rw_prep_kernel,
        grid=(b, nt),
        in_specs=in_specs,
        out_specs=[tok2(), tok2(), tok2(), tok(), tok(), tok(), tok(), tok()],
        out_shape=[two, two_bf, two_bf, one_bf, one_bf, one_bf, one, one],
        compiler_params=_params("parallel", "parallel"),
        name="rw_prep",
    )(zr, zr, zr, p["mu"], p["w2"], p["w0"], p["a2"], p["a0"], p["g2"], p["k_k"], p["k_a"], p["r_k"], p["bd"])


def _rw_scan_kernel(*refs, has_s0, nchunks):
    if has_s0:
        s0_ref, refs = refs[0], refs[1:]
    (lw0_ref, kd0_ref, bb0_ref, a0_ref, r0_ref, v0_ref, lw1_ref, kd1_ref, bb1_ref, a1_ref, r1_ref, v1_ref,
     y0_ref, y1_ref, sf_ref, s_scr) = refs
    dir_refs = ((lw0_ref, kd0_ref, bb0_ref, a0_ref, r0_ref, v0_ref, y0_ref),
                (lw1_ref, kd1_ref, bb1_ref, a1_ref, r1_ref, v1_ref, y1_ref))
    i = pl.program_id(1)
    c = SCAN_CHUNK
    n = RWKV_HEAD_SIZE

    @pl.when(i == 0)
    def _():
        if has_s0:
            s_scr[...] = s0_ref[0]
        else:
            s_scr[...] = jnp.zeros_like(s_scr)

    diff = lax.broadcasted_iota(jnp.int32, (c, c), 0) - lax.broadcasted_iota(jnp.int32, (c, c), 1)
    incl = (diff >= 0, diff <= 0)
    strict = (diff > 0, diff < 0)
    diff2 = (lax.broadcasted_iota(jnp.int32, (c, 2 * c), 0) - lax.broadcasted_iota(jnp.int32, (c, 2 * c), 1) % c)
    incl2 = (diff2 >= 0, diff2 <= 0)
    chains = [(d, h) for d in range(N_DIR) for h in range(RWKV_HEADS)]
    nch = len(chains)
    dirs = [d for d, _ in chains]
    each = lambda f: [f(d, slice(h * n, (h + 1) * n)) for d, h in chains]

    def chunk(jj, carry):
        pre = []
        for d in range(N_DIR):
            lw_ref, kd_ref, bb_ref, a_ref, r_ref, v_ref, _ = dir_refs[d]
            j = jj if d == 0 else nchunks - 1 - jj
            rows = pl.ds(pl.multiple_of(j * c, c), c)
            lw = lw_ref[0, 0, rows, :]
            kd = kd_ref[0, 0, rows, :].astype(F32)
            bb = bb_ref[0, 0, rows, :].astype(F32)
            l1 = lw.astype(BF16)
            rem = lw - l1.astype(F32)
            l2 = rem.astype(BF16)
            l3 = (rem - l2.astype(F32)).astype(BF16)
            tri = incl[d].astype(BF16)
            cum = _dot(jnp.concatenate([tri, tri, tri], axis=1), jnp.concatenate([l1, l2, l3], axis=0))
            ctot = jnp.sum(lw, axis=0, keepdims=True)
            en = jnp.exp(-cum)
            eh = jnp.exp(ctot - cum)
            pre.append(dict(
                rows=rows,
                at=a_ref[0, rows, :].astype(F32) * jnp.exp(cum - lw),
                rt=r_ref[0, rows, :].astype(F32) * jnp.exp(cum),
                bt=(bb * en).astype(BF16), kt=(kd * en).astype(BF16),
                bh=(bb * eh).astype(BF16), kh=(kd * eh).astype(BF16),
                gam=jnp.exp(ctot), v=v_ref[0, rows, :]))

        at = each(lambda d, hs: pre[d]["at"][:, hs])
        rt = each(lambda d, hs: pre[d]["rt"][:, hs])
        vh = each(lambda d, hs: pre[d]["v"][:, hs])
        bh = each(lambda d, hs: pre[d]["bh"][:, hs])
        kh = each(lambda d, hs: pre[d]["kh"][:, hs])
        gam = each(lambda d, hs: pre[d]["gam"][:, hs])
        amat = each(lambda d, hs: lax.dot_general(
            jnp.concatenate([pre[d]["at"][:, hs], pre[d]["rt"][:, hs]], axis=0).astype(BF16),
            jnp.concatenate([pre[d]["bt"][:, hs], pre[d]["kt"][:, hs]], axis=0), _NT, preferred_element_type=F32))
        a_ak = [jnp.where(strict[dirs[q]], amat[q][:c, c:], 0.0).astype(BF16) for q in range(nch)]
        a_r = [jnp.where(incl2[dirs[q]], amat[q][c:, :], 0.0).astype(BF16) for q in range(nch)]
        x = [jnp.where(strict[dirs[q]], amat[q][:c, :c], 0.0) for q in range(nch)]
        z = [jnp.concatenate([at[q], _dot(a_ak[q], vh[q])], axis=1) for q in range(nch)]
        for step in range(NEUMANN_STEPS):
            xb = [x[q].astype(BF16) for q in range(nch)]
            if step < NEUMANN_STEPS - 1:
                prod = [_dot(xb[q], jnp.concatenate([z[q].astype(BF16), xb[q]], axis=1)) for q in range(nch)]
                z = [z[q] + prod[q][:, :2 * n] for q in range(nch)]
                x = [prod[q][:, 2 * n:] for q in range(nch)]
            else:
                z = [z[q] + _dot(xb[q], z[q].astype(BF16)) for q in range(nch)]
        zero = jnp.zeros((c, n), BF16)
        rmat = [jnp.concatenate([z[q].astype(BF16), jnp.concatenate([zero, vh[q]], axis=1)], axis=0)
                for q in range(nch)]
        qy = [_dot(a_r[q], rmat[q]) for q in range(nch)]
        qm = [(rt[q] + qy[q][:, :n]).astype(BF16) for q in range(nch)]
        mn = [lax.dot_general(rmat[q], jnp.concatenate([bh[q], kh[q]], axis=0), _TN, preferred_element_type=F32)
              for q in range(nch)]
        s = [s_scr[d, h] for d, h in chains]
        sb = [s[q].astype(BF16) for q in range(nch)]
        ys = [lax.dot_general(qm[q], sb[q], _NT, preferred_element_type=F32) + qy[q][:, n:] for q in range(nch)]
        for q, (d, h) in enumerate(chains):
            s_scr[d, h] = s[q] * gam[q] + _dot(sb[q], mn[q][:n, :].astype(BF16)) + mn[q][n:, :]
        for d in range(N_DIR):
            y_ref = dir_refs[d][-1]
            y_ref[0, pre[d]["rows"], :] = jnp.concatenate(ys[d * RWKV_HEADS:(d + 1) * RWKV_HEADS], axis=1)
        return carry

    lax.fori_loop(0, nchunks, chunk, 0, unroll=2)

    @pl.when(i == pl.num_programs(1) - 1)
    def _():
        sf_ref[0] = s_scr[...]


def _rw_scan(s0, lw, kd, bb, a, r, v):
    _, b, t, _ = lw.shape
    tb = min(t, 512)
    nblk = t // tb
    blk = (lambda i: i, lambda i: nblk - 1 - i)
    dir_spec = lambda d: pl.BlockSpec((1, 1, tb, RWKV_WIDTH), lambda bi, i: (d, bi, blk[d](i), 0))
    tok_spec = lambda d: pl.BlockSpec((1, tb, RWKV_WIDTH), lambda bi, i: (bi, blk[d](i), 0))
    st_spec = pl.BlockSpec((1, N_DIR, RWKV_HEADS, RWKV_HEAD_SIZE, RWKV_HEAD_SIZE), lambda bi, i: (bi, 0, 0, 0, 0))
    has_s0 = s0 is not None
    in_specs, args = [], []
    if has_s0:
        in_specs.append(st_spec)
        args.append(s0)
    for d in range(N_DIR):
        in_specs += [dir_spec(d), dir_spec(d), dir_spec(d), tok_spec(d), tok_spec(d), tok_spec(d)]
        args += [lw, kd, bb, a, r, v]
    y_shape = jax.ShapeDtypeStruct((b, t, RWKV_WIDTH), F32)
    return pl.pallas_call(
        functools.partial(_rw_scan_kernel, has_s0=has_s0, nchunks=tb // SCAN_CHUNK),
        grid=(b, nblk),
        in_specs=in_specs,
        out_specs=[tok_spec(0), tok_spec(1), st_spec],
        out_shape=[y_shape, y_shape,
                   jax.ShapeDtypeStruct((b, N_DIR, RWKV_HEADS, RWKV_HEAD_SIZE, RWKV_HEAD_SIZE), F32)],
        scratch_shapes=[pltpu.VMEM((N_DIR, RWKV_HEADS, RWKV_HEAD_SIZE, RWKV_HEAD_SIZE), F32)],
        compiler_params=_params("parallel", "arbitrary"),
        name="rw_scan",
    )(*args)


def _post_kernel(x_ref, mod_ref, attn_ref, y0_ref, y1_ref, bonus_ref, g_ref, ga_ref, gb_ref, lng_ref, lnb_ref,
                 bd_ref, wa_ref, wb_ref, wo_ref, o_ref):
    m = mod_ref[0]
    y = y0_ref[0] + y1_ref[0]
    bd = bd_ref[...]
    inv_n = 1.0 / RWKV_HEAD_SIZE
    mean = _split_dot(y, bd) * inv_n
    yc = y - mean
    var = _split_dot(yc * yc, bd) * inv_n
    yn = yc * lax.rsqrt(var + GN_EPS) * lng_ref[...] + lnb_ref[...]
    rw = ((yn + bonus_ref[0]) * g_ref[0]).astype(BF16)
    merged = (ga_ref[0].astype(F32) * _dot(attn_ref[0], wa_ref[...])
              + gb_ref[0].astype(F32) * _dot(rw, wb_ref[...]))
    o_ref[0] = x_ref[0] + m[2:3, :] * _dot(merged.astype(BF16), wo_ref[...])


def _post(x, mod, mod_row, attn, y0, y1, bonus, g, ga, gb, p):
    b, t, _ = x.shape
    tm = min(t, 512)
    tok = lambda width: pl.BlockSpec((1, tm, width), lambda i, j: (i, j, 0))
    in_specs = [tok(D_MODEL),
                pl.BlockSpec((1, 6, D_MODEL), lambda i, j: (mod_row(i), 0, 0)),
                tok(ATTN_WIDTH),
                tok(RWKV_WIDTH), tok(RWKV_WIDTH), tok(RWKV_WIDTH), tok(RWKV_WIDTH), tok(D_MODEL), tok(D_MODEL),
                _const_spec((1, RWKV_WIDTH)), _const_spec((1, RWKV_WIDTH)),
                _const_spec((RWKV_WIDTH, RWKV_WIDTH)),
                _const_spec((ATTN_WIDTH, D_MODEL)), _const_spec((RWKV_WIDTH, D_MODEL)),
                _const_spec((D_MODEL, D_MODEL))]
    return pl.pallas_call(
        _post_kernel,
        grid=(b, t // tm),
        in_specs=in_specs,
        out_specs=tok(D_MODEL),
        out_shape=jax.ShapeDtypeStruct((b, t, D_MODEL), F32),
        compiler_params=_params("parallel", "parallel"),
        name="post",
    )(x, mod, attn, y0, y1, bonus, g, ga, gb, p["ln_g"], p["ln_b"], p["bd"], p["w_proj_a"], p["w_proj_b"], p["w_out"])


def _ffn_kernel(x_ref, xp_ref, xn_ref, mod_ref, g2_ref, wup_ref, cw_ref, cb_ref, wdn_ref, gf_ref, o_ref):
    i = pl.program_id(1)
    nt = pl.num_programs(1)
    m = mod_ref[0]
    g2 = g2_ref[...]

    def norm_mod(x):
        ms = jnp.mean(x * x, axis=-1, keepdims=True)
        return x * lax.rsqrt(ms + NORM_EPS) * g2 * (1.0 + m[4:5, :]) + m[3:4, :]

    x = x_ref[0]
    tm = x.shape[0]
    hp = jnp.where(i > 0, norm_mod(xp_ref[0]), 0.0)
    hn = jnp.where(i < nt - 1, norm_mod(xn_ref[0]), 0.0)
    h = jnp.concatenate([hp, norm_mod(x), hn], axis=0).astype(BF16)
    cw = cw_ref[...]
    cb = cb_ref[...]
    halo = 2 * SUBLANES
    interleave = tm // FF_ROWS >= 4
    nrb = tm // FF_ROWS if interleave else 1
    blk_rows = tm // nrb

    def up_piece(j, r):
        lo = j * FF_CHUNK
        hr = h[r * blk_rows:(r + 1) * blk_rows + (halo if r == nrb - 1 else 0), :]
        return _dot(hr, wup_ref[:, lo:lo + FF_CHUNK]), _dot(hr, wup_ref[:, D_FF + lo:D_FF + lo + FF_CHUNK])

    def conv(pieces, half, r, lo):
        if r == nrb - 1:
            u = pieces[r][half]
        else:
            u = jnp.concatenate([pieces[r][half], pieces[r + 1][half][:halo, :]], axis=0)
        w = cw[:, lo:lo + FF_CHUNK]
        out = (pltpu.roll(u, 1, 0) * w[0:1, :] + u * w[1:2, :] + pltpu.roll(u, blk_rows + halo - 1, 0) * w[2:3, :]
               + cb[:, lo:lo + FF_CHUNK])
        return out[SUBLANES:SUBLANES + blk_rows, :]

    nchunks = D_FF // FF_CHUNK
    acc = [jnp.zeros((blk_rows, D_MODEL), F32) for _ in range(nrb)]
    cur = [up_piece(0, r) for r in range(nrb)]
    for j in range(nchunks):
        lo = j * FF_CHUNK
        more = j + 1 < nchunks
        nxt = [up_piece(j + 1, r) for r in range(nrb)] if (more and not interleave) else []
        for r in range(nrb):
            val = conv(cur, 0, r, lo)
            gate = conv(cur, 1, r, D_FF + lo)
            act = (gate * _sigmoid(gate) * val).astype(BF16)
            acc[r] = acc[r] + _dot(act, wdn_ref[lo:lo + FF_CHUNK, :])
            if more and interleave:
                nxt.append(up_piece(j + 1, r))
        cur = nxt
    x2 = x + m[5:6, :] * jnp.concatenate(acc, axis=0)
    ms = jnp.mean(x2 * x2, axis=-1, keepdims=True)
    o_ref[0] = x2 * lax.rsqrt(ms + NORM_EPS) * gf_ref[...]


def _ffn(x, mod, mod_row, p):
    b, t, _ = x.shape
    tm = min(t, 512)
    hb = tm // SUBLANES
    last_hb = t // SUBLANES - 1
    in_specs = [pl.BlockSpec((1, tm, D_MODEL), lambda i, j: (i, j, 0)),
                pl.BlockSpec((1, SUBLANES, D_MODEL), lambda i, j: (i, jnp.maximum(j * hb - 1, 0), 0)),
                pl.BlockSpec((1, SUBLANES, D_MODEL), lambda i, j: (i, jnp.minimum((j + 1) * hb, last_hb), 0)),
                pl.BlockSpec((1, 6, D_MODEL), lambda i, j: (mod_row(i), 0, 0)),
                _const_spec((1, D_MODEL)),
                _const_spec((D_MODEL, 2 * D_FF)), _const_spec((3, 2 * D_FF)), _const_spec((1, 2 * D_FF)),
                _const_spec((D_FF, D_MODEL)), _const_spec((1, D_MODEL))]
    return pl.pallas_call(
        _ffn_kernel,
        grid=(b, t // tm),
        in_specs=in_specs,
        out_specs=pl.BlockSpec((1, tm, D_MODEL), lambda i, j: (i, j, 0)),
        out_shape=jax.ShapeDtypeStruct((b, t, D_MODEL), F32),
        compiler_params=_params("parallel", "parallel"),
        name="ffn",
    )(x, x, x, mod, p["g_norm2"], p["w_ffn_up"], p["conv_w"], p["conv_b"], p["w_ffn_down"], p["g_final"])


def _rope_tables(t):
    rows = t // GRID_W
    row = jnp.repeat(jnp.arange(rows), GRID_W).astype(F32)
    col = jnp.tile(jnp.arange(GRID_W), rows).astype(F32)
    freqs = ROPE_BASE ** (-jnp.arange(ROPE_PAIRS, dtype=F32) / ROPE_PAIRS)
    ar = row[:, None] * freqs
    ac = col[:, None] * freqs
    cos = jnp.concatenate([jnp.cos(ar), jnp.cos(ar), jnp.cos(ac), jnp.cos(ac)], axis=1)
    sin = jnp.concatenate([-jnp.sin(ar), jnp.sin(ar), -jnp.sin(ac), jnp.sin(ac)], axis=1)
    return jnp.tile(cos, (1, 2)), jnp.tile(sin, (1, 2))


def _block_diag2(w):
    z = jnp.zeros_like(w[0])
    return jnp.concatenate([jnp.concatenate([w[0], z], axis=1), jnp.concatenate([z, w[1]], axis=1)], axis=0)


def _trunk(x, mod, mod_row_tile, mod_row_seq, p, ctx):
    b, t, _ = x.shape
    rope = _rope_tables(t) if ctx is not None else None
    kv_dtype = BF16 if ctx is not None else F32
    q, k, v, ga, gb, zr = _in_proj(x.reshape(b * t, D_MODEL), mod, p["g_norm1"], p["w_in"], t,
                                   mod_row_tile, rope, kv_dtype)
    seq = lambda arr: arr.reshape(b, t, arr.shape[-1])
    q, k, v, ga, gb, zr = seq(q), seq(k), seq(v), seq(ga), seq(gb), seq(zr)
    if ctx is None:
        attn = _attn_ctx(q, k, v, p["sink"])
        s0 = None
    else:
        k_ctx, v_ctx, s0 = ctx
        attn = _attn_lat(q, k, v, k_ctx, v_ctx, p["sink"])
    lw, kd, bb, a, r, vr, g, bonus = _rw_prep(zr, p)
    y0, y1, s_final = _rw_scan(s0, lw, kd, bb, a, r, vr)
    x1 = _post(x, mod, mod_row_seq, attn, y0, y1, bonus, g, ga, gb, p)
    out = _ffn(x1, mod, mod_row_seq, p)
    return out, k, v, s_final


def kernel(x_prompt, x_sample, c, cache_k, cache_v, state_rwkv, c_ctx, w_ada, b_ada, g_norm1, w_in, attn_sink, w_proj_a, w_proj_b, rwkv_mu, rwkv_w0, rwkv_w2, rwkv_a0, rwkv_a2, rwkv_k_k, rwkv_k_a, rwkv_r_k, rwkv_g2, rwkv_ln_g, rwkv_ln_b, w_out, g_norm2, w_ffn_up, ffn_conv_w, ffn_conv_b, w_ffn_down, g_final):
    depth = w_ada.shape[0]
    assert depth == 1, "single trunk layer"
    l = 0
    nb, seq, _ = x_prompt.shape
    db, dseq, _ = x_sample.shape
    past = cache_k.shape[2]

    head_id = jnp.arange(RWKV_WIDTH) // RWKV_HEAD_SIZE
    p = dict(
        g_norm1=g_norm1[l].reshape(1, D_MODEL),
        w_in=w_in[l].astype(BF16),
        sink=attn_sink[l],
        mu=rwkv_mu[l],
        w2=_block_diag2(rwkv_w2[l]).astype(BF16),
        w0=rwkv_w0[l].reshape(1, N_DIR * RWKV_WIDTH),
        a2=_block_diag2(rwkv_a2[l]).astype(BF16),
        a0=rwkv_a0[l].reshape(1, N_DIR * RWKV_WIDTH),
        g2=rwkv_g2[l].astype(BF16),
        k_k=rwkv_k_k[l].reshape(1, RWKV_WIDTH),
        k_a=rwkv_k_a[l].reshape(1, RWKV_WIDTH),
        r_k=rwkv_r_k[l].reshape(1, RWKV_WIDTH),
        bd=(head_id[:, None] == head_id[None, :]).astype(BF16),
        ln_g=rwkv_ln_g[l].reshape(1, RWKV_WIDTH),
        ln_b=rwkv_ln_b[l].reshape(1, RWKV_WIDTH),
        w_proj_a=w_proj_a[l].astype(BF16),
        w_proj_b=w_proj_b[l].astype(BF16),
        w_out=w_out[l].astype(BF16),
        g_norm2=g_norm2[l].reshape(1, D_MODEL),
        w_ffn_up=w_ffn_up[l].astype(BF16),
        conv_w=ffn_conv_w[l],
        conv_b=ffn_conv_b[l].reshape(1, 2 * D_FF),
        w_ffn_down=w_ffn_down[l].astype(BF16),
        g_final=g_final.reshape(1, D_MODEL),
    )

    ctx_row = db
    mod_rows = 2 * SUBLANES
    cond = jnp.zeros((mod_rows, D_MODEL), F32).at[:db].set(c).at[ctx_row].set(c_ctx)
    mod = _modulation(cond, w_ada[l], b_ada[l]).reshape(mod_rows, 6, D_MODEL)

    y_prompt, kc, vc, sc = _trunk(x_prompt, mod, lambda i: ctx_row, lambda i: ctx_row, p, None)

    lat_tiles = max(dseq // 512, 1)
    ctx = (cache_k[:, l].reshape(db, past, KV_WIDTH), cache_v[:, l].reshape(db, past, KV_WIDTH), state_rwkv[:, l])
    y_sample, _, _, _ = _trunk(x_sample, mod, lambda i: i // lat_tiles, lambda i: i, p, ctx)

    new_cache_k = kc.reshape(nb, 1, seq, N_KV_HEADS, HEAD_DIM)
    new_cache_v = vc.reshape(nb, 1, seq, N_KV_HEADS, HEAD_DIM)
    new_state = sc.reshape(nb, 1, N_DIR, RWKV_HEADS, RWKV_HEAD_SIZE, RWKV_HEAD_SIZE)
    return (y_prompt, y_sample, new_cache_k, new_cache_v, new_state)
```

```python
import functools

import jax
import jax.numpy as jnp
from jax import lax
from jax.experimental import pallas as pl
from jax.experimental.pallas import tpu as pltpu

F32 = jnp.float32
BF16 = jnp.bfloat16

D_MODEL = 1024
GRID_W = 64
HEAD_DIM = 64
N_Q_HEADS = 8
N_KV_HEADS = 2
GQA_GROUP = N_Q_HEADS // N_KV_HEADS
ATTN_WIDTH = N_Q_HEADS * HEAD_DIM
KV_WIDTH = N_KV_HEADS * HEAD_DIM
WINDOW = 128
BLOCK = 128
ROPE_BASE = 10000.0
ROPE_PAIRS = HEAD_DIM // 4
RWKV_HEADS = 8
RWKV_HEAD_SIZE = 64
RWKV_WIDTH = RWKV_HEADS * RWKV_HEAD_SIZE
N_DIR = 2
DECAY_RANK = 64
ICLR_RANK = 64
GATE_RANK = 128
RW_SHIFT_COLS = 3 * RWKV_WIDTH + N_DIR * DECAY_RANK + N_DIR * ICLR_RANK + GATE_RANK
D_FF = 2816
NORM_EPS = 1e-6
GN_EPS = 64e-5
MASK_VALUE = -1e30

Q_OFF = 0
K_OFF = ATTN_WIDTH
V_OFF = K_OFF + KV_WIDTH
GA_OFF = V_OFF + KV_WIDTH
GB_OFF = GA_OFF + D_MODEL
ZR_OFF = GB_OFF + D_MODEL
W_IN_COLS = ZR_OFF + RW_SHIFT_COLS

ZW_OFF = 3 * RWKV_WIDTH
ZA_OFF = ZW_OFF + N_DIR * DECAY_RANK
ZG_OFF = ZA_OFF + N_DIR * ICLR_RANK

SCAN_CHUNK = 64
NEUMANN_STEPS = 6
SUBLANES = 8
ATTN_UNIT = 2
FF_CHUNK = 256
FF_ROWS = 128
VMEM_LIMIT = 56 * 1024 * 1024

_NT = (((1,), (1,)), ((), ()))
_TN = (((0,), (0,)), ((), ()))


def _params(*sem):
    return pltpu.CompilerParams(dimension_semantics=sem, vmem_limit_bytes=VMEM_LIMIT)


def _const_spec(shape):
    nd = len(shape)
    return pl.BlockSpec(shape, lambda *_: (0,) * nd, pipeline_mode=pl.Buffered(1))


def _dot(a, b):
    return jnp.dot(a, b, preferred_element_type=F32)


def _split_dot(x, m):
    hi = x.astype(BF16)
    lo = (x - hi.astype(F32)).astype(BF16)
    return _dot(hi, m) + _dot(lo, m)


def _sigmoid(x):
    return 1.0 / (1.0 + jnp.exp(-x))


def _mod_kernel(c_ref, w_ref, b_ref, o_ref):
    c = c_ref[...]
    s = (c * _sigmoid(c)).astype(BF16)
    o_ref[...] = _dot(s, w_ref[...].astype(BF16)) + b_ref[...]


def _modulation(cond, w_ada, b_ada):
    rows = cond.shape[0]
    n = w_ada.shape[1]
    tn = D_MODEL
    return pl.pallas_call(
        _mod_kernel,
        grid=(n // tn,),
        in_specs=[pl.BlockSpec((rows, D_MODEL), lambda j: (0, 0)),
                  pl.BlockSpec((D_MODEL, tn), lambda j: (0, j)),
                  pl.BlockSpec((1, tn), lambda j: (0, j))],
        out_specs=pl.BlockSpec((rows, tn), lambda j: (0, j)),
        out_shape=jax.ShapeDtypeStruct((rows, n), F32),
        compiler_params=_params("parallel"),
        name="modulation",
    )(cond, w_ada, b_ada.reshape(1, n))


def _swap16(x):
    w = x.shape[1]
    lane = lax.broadcasted_iota(jnp.int32, x.shape, 1)
    first = (lane % 32) < 16
    return jnp.where(first, pltpu.roll(x, w - 16, 1), pltpu.roll(x, 16, 1))


def _in_proj_kernel(*refs, rope):
    if rope:
        x_ref, mod_ref, g_ref, w_ref, cos_ref, sin_ref, q_ref, k_ref, v_ref, ga_ref, gb_ref, zr_ref = refs
    else:
        x_ref, mod_ref, g_ref, w_ref, q_ref, k_ref, v_ref, ga_ref, gb_ref, zr_ref = refs
    x = x_ref[...]
    m = mod_ref[0]
    ms = jnp.mean(x * x, axis=-1, keepdims=True)
    h = x * lax.rsqrt(ms + NORM_EPS) * g_ref[...]
    h = (h * (1.0 + m[1:2, :]) + m[0:1, :]).astype(BF16)

    q = _dot(h, w_ref[:, Q_OFF:K_OFF])
    k = _dot(h, w_ref[:, K_OFF:V_OFF])
    v = _dot(h, w_ref[:, V_OFF:GA_OFF])
    if rope:
        cos = cos_ref[...]
        sin = sin_ref[...]
        reps = ATTN_WIDTH // cos.shape[1]
        cos_q = jnp.concatenate([cos] * reps, axis=1)
        sin_q = jnp.concatenate([sin] * reps, axis=1)
        q = q * cos_q + _swap16(q) * sin_q
        k = k * cos + _swap16(k) * sin
    q_ref[...] = (q * (HEAD_DIM ** -0.5)).astype(q_ref.dtype)
    k_ref[...] = k.astype(k_ref.dtype)
    v_ref[...] = v.astype(v_ref.dtype)
    ga_ref[...] = _sigmoid(_dot(h, w_ref[:, GA_OFF:GB_OFF])).astype(ga_ref.dtype)
    gb_ref[...] = _sigmoid(_dot(h, w_ref[:, GB_OFF:ZR_OFF])).astype(gb_ref.dtype)
    zr_ref[...] = _dot(h, w_ref[:, ZR_OFF:W_IN_COLS])


def _in_proj(x, mod, g1, w_in, seq_len, mod_row, rope_tabs, kv_dtype):
    ntok = x.shape[0]
    tm = min(512, seq_len) if rope_tabs is not None else 512
    tiles_per_seq = max(seq_len // tm, 1)
    rope = rope_tabs is not None
    tok = lambda width: pl.BlockSpec((tm, width), lambda i: (i, 0))
    in_specs = [tok(D_MODEL),
                pl.BlockSpec((1, 6, D_MODEL), lambda i: (mod_row(i), 0, 0)),
                _const_spec((1, D_MODEL)),
                _const_spec((D_MODEL, W_IN_COLS))]
    args = [x, mod, g1, w_in]
    if rope:
        in_specs += [pl.BlockSpec((tm, 2 * HEAD_DIM), lambda i: (i % tiles_per_seq, 0))] * 2
        args += list(rope_tabs)
    out_shape = [jax.ShapeDtypeStruct((ntok, ATTN_WIDTH), BF16),
                 jax.ShapeDtypeStruct((ntok, KV_WIDTH), kv_dtype),
                 jax.ShapeDtypeStruct((ntok, KV_WIDTH), kv_dtype),
                 jax.ShapeDtypeStruct((ntok, D_MODEL), BF16),
                 jax.ShapeDtypeStruct((ntok, D_MODEL), BF16),
                 jax.ShapeDtypeStruct((ntok, RW_SHIFT_COLS), F32)]
    out_specs = [tok(ATTN_WIDTH), tok(KV_WIDTH), tok(KV_WIDTH), tok(D_MODEL), tok(D_MODEL), tok(RW_SHIFT_COLS)]
    return pl.pallas_call(
        functools.partial(_in_proj_kernel, rope=rope),
        grid=(ntok // tm,),
        in_specs=in_specs, out_specs=out_specs, out_shape=out_shape,
        compiler_params=_params("parallel"),
        name="in_proj_rope" if rope else "in_proj",
    )(*args)


def _softmax_pv(s, sink_col, v):
    m = jnp.maximum(jnp.max(s, axis=-1, keepdims=True), sink_col)
    p = jnp.exp(s - m)
    denom = jnp.sum(p, axis=-1, keepdims=True) + jnp.exp(sink_col - m)
    return _dot(p.astype(BF16), v) / denom


def _attend(q, k, v, sink_ref, mask):
    t = q.shape[0]
    units = [list(range(u, u + ATTN_UNIT)) for u in range(0, N_Q_HEADS, ATTN_UNIT)]
    kv_of = lambda heads: slice((heads[0] // GQA_GROUP) * HEAD_DIM, (heads[0] // GQA_GROUP + 1) * HEAD_DIM)
    scores = []
    for heads in units:
        qu = jnp.concatenate([q[:, h * HEAD_DIM:(h + 1) * HEAD_DIM] for h in heads], axis=0)
        scores.append(mask(lax.dot_general(qu, k[:, kv_of(heads)], _NT, preferred_element_type=F32)))
    outs = []
    for heads, s in zip(units, scores):
        sink_col = jnp.concatenate([jnp.full((t, 1), sink_ref[h], F32) for h in heads], axis=0)
        o = _softmax_pv(s, sink_col, v[:, kv_of(heads)])
        outs += [o[i * t:(i + 1) * t, :] for i in range(ATTN_UNIT)]
    return jnp.concatenate(outs, axis=1)


def _attn_ctx_kernel(sink_ref, q_ref, k_ref, v_ref, o_ref):
    out = _attend(q_ref[0], k_ref[0].astype(BF16), v_ref[0].astype(BF16), sink_ref, lambda s: s)
    o_ref[0] = out.astype(o_ref.dtype)


def _attn_ctx(q, k, v, sink):
    b, t, _ = q.shape
    seq = lambda width: pl.BlockSpec((1, t, width), lambda i: (i, 0, 0))
    return pl.pallas_call(
        _attn_ctx_kernel,
        grid=(b,),
        in_specs=[pl.BlockSpec(memory_space=pltpu.SMEM), seq(ATTN_WIDTH), seq(KV_WIDTH), seq(KV_WIDTH)],
        out_specs=seq(ATTN_WIDTH),
        out_shape=jax.ShapeDtypeStruct((b, t, ATTN_WIDTH), BF16),
        compiler_params=_params("parallel"),
        name="attn_ctx",
    )(sink, q, k, v)


def _attn_lat_kernel(sink_ref, q_ref, kp_ref, kc_ref, kn_ref, vp_ref, vc_ref, vn_ref, kx_ref, vx_ref, o_ref):
    n = pl.program_id(1)
    nb = pl.num_programs(1)
    q = q_ref[0]
    k = jnp.concatenate([kp_ref[0], kc_ref[0], kn_ref[0], kx_ref[0].astype(BF16)], axis=0)
    v = jnp.concatenate([vp_ref[0], vc_ref[0], vn_ref[0], vx_ref[0].astype(BF16)], axis=0)
    rows = ATTN_UNIT * BLOCK
    rq = lax.broadcasted_iota(jnp.int32, (rows, BLOCK), 0) % BLOCK
    col = lax.broadcasted_iota(jnp.int32, (rows, BLOCK), 1)
    keep_prev = col >= rq + jnp.where(n > 0, 0, BLOCK)
    keep_next = col <= rq - jnp.where(n < nb - 1, 0, BLOCK)

    def mask(s):
        return jnp.concatenate([jnp.where(keep_prev, s[:, :BLOCK], MASK_VALUE), s[:, BLOCK:2 * BLOCK],
                                jnp.where(keep_next, s[:, 2 * BLOCK:3 * BLOCK], MASK_VALUE), s[:, 3 * BLOCK:]], axis=1)

    o_ref[0] = _attend(q, k, v, sink_ref, mask).astype(o_ref.dtype)


def _attn_lat(q, k, v, k_ctx, v_ctx, sink):
    assert WINDOW == BLOCK, "the block-triangular band masks assume one block of reach on either side"
    b, t, _ = q.shape
    nb = t // BLOCK
    p = k_ctx.shape[1]
    blk = lambda width, f: pl.BlockSpec((1, BLOCK, width), lambda i, n: (i, f(n), 0))
    prev = lambda n: jnp.maximum(n - 1, 0)
    cur = lambda n: n
    nxt = lambda n: jnp.minimum(n + 1, nb - 1)
    ctx = pl.BlockSpec((1, p, KV_WIDTH), lambda i, n: (i, 0, 0))
    return pl.pallas_call(
        _attn_lat_kernel,
        grid=(b, nb),
        in_specs=[pl.BlockSpec(memory_space=pltpu.SMEM), blk(ATTN_WIDTH, cur),
                  blk(KV_WIDTH, prev), blk(KV_WIDTH, cur), blk(KV_WIDTH, nxt),
                  blk(KV_WIDTH, prev), blk(KV_WIDTH, cur), blk(KV_WIDTH, nxt), ctx, ctx],
        out_specs=blk(ATTN_WIDTH, cur),
        out_shape=jax.ShapeDtypeStruct((b, t, ATTN_WIDTH), BF16),
        compiler_params=_params("parallel", "parallel"),
        name="attn_lat",
    )(sink, q, k, k, k, v, v, v, k_ctx, v_ctx)


def _rw_prep_kernel(z_ref, zp_ref, zn_ref, mu_ref, w2_ref, w0_ref, a2_ref, a0_ref, g2_ref, kk_ref, ka_ref,
                    rk_ref, bd_ref, lw_ref, kd_ref, bb_ref, a_ref, r_ref, v_ref, g_ref, bonus_ref):
    i = pl.program_id(1)
    nt = pl.num_programs(1)
    z = z_ref[0]
    tm = z.shape[0]
    row = lax.broadcasted_iota(jnp.int32, z.shape, 0)
    halo_prev = jnp.where(i > 0, zp_ref[0, SUBLANES - 1:SUBLANES, :], 0.0)
    halo_next = jnp.where(i < nt - 1, zn_ref[0, 0:1, :], 0.0)
    prev = jnp.where(row == 0, halo_prev, pltpu.roll(z, 1, 0))
    nxt = jnp.where(row == tm - 1, halo_next, pltpu.roll(z, tm - 1, 0))
    mu = mu_ref[...]
    z = z + mu[0:1, :] * (prev - z) + mu[1:2, :] * (nxt - z)

    r = z[:, 0:RWKV_WIDTH]
    k = z[:, RWKV_WIDTH:2 * RWKV_WIDTH]
    v = z[:, 2 * RWKV_WIDTH:3 * RWKV_WIDTH]
    zw = z[:, ZW_OFF:ZA_OFF]
    za = z[:, ZA_OFF:ZG_OFF]
    zg = z[:, ZG_OFF:RW_SHIFT_COLS]

    logit = w0_ref[...] + _dot(jnp.tanh(zw).astype(BF16), w2_ref[...])
    lw = -jnp.exp(F32(-0.5)) * _sigmoid(logit)
    a = _sigmoid(a0_ref[...] + _dot(za.astype(BF16), a2_ref[...]))
    g_ref[0] = _dot(_sigmoid(zg).astype(BF16), g2_ref[...])

    bd = bd_ref[...]
    kk = k * kk_ref[...]
    kk = kk * lax.rsqrt(jnp.maximum(_split_dot(kk * kk, bd), 1e-24))
    ka = ka_ref[...]
    ksum = jnp.zeros_like(k)
    for d in range(N_DIR):
        ad = a[:, d * RWKV_WIDTH:(d + 1) * RWKV_WIDTH]
        kd = k * (1.0 + (ad - 1.0) * ka)
        lw_ref[d, 0] = lw[:, d * RWKV_WIDTH:(d + 1) * RWKV_WIDTH]
        kd_ref[d, 0] = kd.astype(kd_ref.dtype)
        bb_ref[d, 0] = (ad * kk).astype(bb_ref.dtype)
        ksum = ksum + kd
    a_ref[0] = (-kk).astype(a_ref.dtype)
    r_ref[0] = r.astype(r_ref.dtype)
    v_ref[0] = v.astype(v_ref.dtype)
    bonus_ref[0] = _split_dot(r * ksum * rk_ref[...], bd) * v


def _rw_prep(zr, p):
    b, t, _ = zr.shape
    tm = 256
    nt = t // tm
    hb = tm // SUBLANES
    last_hb = t // SUBLANES - 1
    tok = lambda: pl.BlockSpec((1, tm, RWKV_WIDTH), lambda i, j: (i, j, 0))
    tok2 = lambda: pl.BlockSpec((N_DIR, 1, tm, RWKV_WIDTH), lambda i, j: (0, i, j, 0))
    one = jax.ShapeDtypeStruct((b, t, RWKV_WIDTH), F32)
    two = jax.ShapeDtypeStruct((N_DIR, b, t, RWKV_WIDTH), F32)
    one_bf = jax.ShapeDtypeStruct((b, t, RWKV_WIDTH), BF16)
    two_bf = jax.ShapeDtypeStruct((N_DIR, b, t, RWKV_WIDTH), BF16)
    in_specs = [pl.BlockSpec((1, tm, RW_SHIFT_COLS), lambda i, j: (i, j, 0)),
                pl.BlockSpec((1, SUBLANES, RW_SHIFT_COLS), lambda i, j: (i, jnp.maximum(j * hb - 1, 0), 0)),
                pl.BlockSpec((1, SUBLANES, RW_SHIFT_COLS), lambda i, j: (i, jnp.minimum((j + 1) * hb, last_hb), 0)),
                _const_spec((2, RW_SHIFT_COLS)),
                _const_spec((N_DIR * DECAY_RANK, N_DIR * RWKV_WIDTH)), _const_spec((1, N_DIR * RWKV_WIDTH)),
                _const_spec((N_DIR * ICLR_RANK, N_DIR * RWKV_WIDTH)), _const_spec((1, N_DIR * RWKV_WIDTH)),
                _const_spec((GATE_RANK, RWKV_WIDTH)),
                _const_spec((1, RWKV_WIDTH)), _const_spec((1, RWKV_WIDTH)), _const_spec((1, RWKV_WIDTH)),
                _const_spec((RWKV_WIDTH, RWKV_WIDTH))]
    return pl.pallas_call(
        _rw_prep_kernel,
        grid=(b, nt),
        in_specs=in_specs,
        out_specs=[tok2(), tok2(), tok2(), tok(), tok(), tok(), tok(), tok()],
        out_shape=[two, two_bf, two_bf, one_bf, one_bf, one_bf, one, one],
        compiler_params=_params("parallel", "parallel"),
        name="rw_prep",
    )(zr, zr, zr, p["mu"], p["w2"], p["w0"], p["a2"], p["a0"], p["g2"], p["k_k"], p["k_a"], p["r_k"], p["bd"])


def _rw_scan_kernel(*refs, has_s0, nchunks):
    if has_s0:
        s0_ref, refs = refs[0], refs[1:]
    (lw0_ref, kd0_ref, bb0_ref, a0_ref, r0_ref, v0_ref, lw1_ref, kd1_ref, bb1_ref, a1_ref, r1_ref, v1_ref,
     y0_ref, y1_ref, sf_ref, s_scr) = refs
    dir_refs = ((lw0_ref, kd0_ref, bb0_ref, a0_ref, r0_ref, v0_ref, y0_ref),
                (lw1_ref, kd1_ref, bb1_ref, a1_ref, r1_ref, v1_ref, y1_ref))
    i = pl.program_id(1)
    c = SCAN_CHUNK
    n = RWKV_HEAD_SIZE

    @pl.when(i == 0)
    def _():
        if has_s0:
            s_scr[...] = s0_ref[0]
        else:
            s_scr[...] = jnp.zeros_like(s_scr)

    diff = lax.broadcasted_iota(jnp.int32, (c, c), 0) - lax.broadcasted_iota(jnp.int32, (c, c), 1)
    incl = (diff >= 0, diff <= 0)
    strict = (diff > 0, diff < 0)
    diff2 = (lax.broadcasted_iota(jnp.int32, (c, 2 * c), 0) - lax.broadcasted_iota(jnp.int32, (c, 2 * c), 1) % c)
    incl2 = (diff2 >= 0, diff2 <= 0)
    chains = [(d, h) for d in range(N_DIR) for h in range(RWKV_HEADS)]
    nch = len(chains)
    dirs = [d for d, _ in chains]
    each = lambda f: [f(d, slice(h * n, (h + 1) * n)) for d, h in chains]

    def chunk(jj, carry):
        pre = []
        for d in range(N_DIR):
            lw_ref, kd_ref, bb_ref, a_ref, r_ref, v_ref, _ = dir_refs[d]
            j = jj if d == 0 else nchunks - 1 - jj
            rows = pl.ds(pl.multiple_of(j * c, c), c)
            lw = lw_ref[0, 0, rows, :]
            kd = kd_ref[0, 0, rows, :].astype(F32)
            bb = bb_ref[0, 0, rows, :].astype(F32)
            l1 = lw.astype(BF16)
            rem = lw - l1.astype(F32)
            l2 = rem.astype(BF16)
            l3 = (rem - l2.astype(F32)).astype(BF16)
            tri = incl[d].astype(BF16)
            cum = _dot(jnp.concatenate([tri, tri, tri], axis=1), jnp.concatenate([l1, l2, l3], axis=0))
            ctot = jnp.sum(lw, axis=0, keepdims=True)
            en = jnp.exp(-cum)
            eh = jnp.exp(ctot - cum)
            pre.append(dict(
                rows=rows,
                at=a_ref[0, rows, :].astype(F32) * jnp.exp(cum - lw),
                rt=r_ref[0, rows, :].astype(F32) * jnp.exp(cum),
                bt=(bb * en).astype(BF16), kt=(kd * en).astype(BF16),
                bh=(bb * eh).astype(BF16), kh=(kd * eh).astype(BF16),
                gam=jnp.exp(ctot), v=v_ref[0, rows, :]))

        at = each(lambda d, hs: pre[d]["at"][:, hs])
        rt = each(lambda d, hs: pre[d]["rt"][:, hs])
        vh = each(lambda d, hs: pre[d]["v"][:, hs])
        bh = each(lambda d, hs: pre[d]["bh"][:, hs])
        kh = each(lambda d, hs: pre[d]["kh"][:, hs])
        gam = each(lambda d, hs: pre[d]["gam"][:, hs])
        amat = each(lambda d, hs: lax.dot_general(
            jnp.concatenate([pre[d]["at"][:, hs], pre[d]["rt"][:, hs]], axis=0).astype(BF16),
            jnp.concatenate([pre[d]["bt"][:, hs], pre[d]["kt"][:, hs]], axis=0), _NT, preferred_element_type=F32))
        a_ak = [jnp.where(strict[dirs[q]], amat[q][:c, c:], 0.0).astype(BF16) for q in range(nch)]
        a_r = [jnp.where(incl2[dirs[q]], amat[q][c:, :], 0.0).astype(BF16) for q in range(nch)]
        x = [jnp.where(strict[dirs[q]], amat[q][:c, :c], 0.0) for q in range(nch)]
        z = [jnp.concatenate([at[q], _dot(a_ak[q], vh[q])], axis=1) for q in range(nch)]
        for step in range(NEUMANN_STEPS):
            xb = [x[q].astype(BF16) for q in range(nch)]
            if step < NEUMANN_STEPS - 1:
                prod = [_dot(xb[q], jnp.concatenate([z[q].astype(BF16), xb[q]], axis=1)) for q in range(nch)]
                z = [z[q] + prod[q][:, :2 * n] for q in range(nch)]
                x = [prod[q][:, 2 * n:] for q in range(nch)]
            else:
                z = [z[q] + _dot(xb[q], z[q].astype(BF16)) for q in range(nch)]
        zero = jnp.zeros((c, n), BF16)
        rmat = [jnp.concatenate([z[q].astype(BF16), jnp.concatenate([zero, vh[q]], axis=1)], axis=0)
                for q in range(nch)]
        qy = [_dot(a_r[q], rmat[q]) for q in range(nch)]
        qm = [(rt[q] + qy[q][:, :n]).astype(BF16) for q in range(nch)]
        mn = [lax.dot_general(rmat[q], jnp.concatenate([bh[q], kh[q]], axis=0), _TN, preferred_element_type=F32)
              for q in range(nch)]
        s = [s_scr[d, h] for d, h in chains]
        sb = [s[q].astype(BF16) for q in range(nch)]
        ys = [lax.dot_general(qm[q], sb[q], _NT, preferred_element_type=F32) + qy[q][:, n:] for q in range(nch)]
        for q, (d, h) in enumerate(chains):
            s_scr[d, h] = s[q] * gam[q] + _dot(sb[q], mn[q][:n, :].astype(BF16)) + mn[q][n:, :]
        for d in range(N_DIR):
            y_ref = dir_refs[d][-1]
            y_ref[0, pre[d]["rows"], :] = jnp.concatenate(ys[d * RWKV_HEADS:(d + 1) * RWKV_HEADS], axis=1)
        return carry

    lax.fori_loop(0, nchunks, chunk, 0, unroll=2)

    @pl.when(i == pl.num_programs(1) - 1)
    def _():
        sf_ref[0] = s_scr[...]


def _rw_scan(s0, lw, kd, bb, a, r, v):
    _, b, t, _ = lw.shape
    tb = min(t, 512)
    nblk = t // tb
    blk = (lambda i: i, lambda i: nblk - 1 - i)
    dir_spec = lambda d: pl.BlockSpec((1, 1, tb, RWKV_WIDTH), lambda bi, i: (d, bi, blk[d](i), 0))
    tok_spec = lambda d: pl.BlockSpec((1, tb, RWKV_WIDTH), lambda bi, i: (bi, blk[d](i), 0))
    st_spec = pl.BlockSpec((1, N_DIR, RWKV_HEADS, RWKV_HEAD_SIZE, RWKV_HEAD_SIZE), lambda bi, i: (bi, 0, 0, 0, 0))
    has_s0 = s0 is not None
    in_specs, args = [], []
    if has_s0:
        in_specs.append(st_spec)
        args.append(s0)
    for d in range(N_DIR):
        in_specs += [dir_spec(d), dir_spec(d), dir_spec(d), tok_spec(d), tok_spec(d), tok_spec(d)]
        args += [lw, kd, bb, a, r, v]
    y_shape = jax.ShapeDtypeStruct((b, t, RWKV_WIDTH), F32)
    return pl.pallas_call(
        functools.partial(_rw_scan_kernel, has_s0=has_s0, nchunks=tb // SCAN_CHUNK),
        grid=(b, nblk),
        in_specs=in_specs,
        out_specs=[tok_spec(0), tok_spec(1), st_spec],
        out_shape=[y_shape, y_shape,
                   jax.ShapeDtypeStruct((b, N_DIR, RWKV_HEADS, RWKV_HEAD_SIZE, RWKV_HEAD_SIZE), F32)],
        scratch_shapes=[pltpu.VMEM((N_DIR, RWKV_HEADS, RWKV_HEAD_SIZE, RWKV_HEAD_SIZE), F32)],
        compiler_params=_params("parallel", "arbitrary"),
        name="rw_scan",
    )(*args)


def _post_kernel(x_ref, mod_ref, attn_ref, y0_ref, y1_ref, bonus_ref, g_ref, ga_ref, gb_ref, lng_ref, lnb_ref,
                 bd_ref, wa_ref, wb_ref, wo_ref, o_ref):
    m = mod_ref[0]
    y = y0_ref[0] + y1_ref[0]
    bd = bd_ref[...]
    inv_n = 1.0 / RWKV_HEAD_SIZE
    mean = _split_dot(y, bd) * inv_n
    yc = y - mean
    var = _split_dot(yc * yc, bd) * inv_n
    yn = yc * lax.rsqrt(var + GN_EPS) * lng_ref[...] + lnb_ref[...]
    rw = ((yn + bonus_ref[0]) * g_ref[0]).astype(BF16)
    merged = (ga_ref[0].astype(F32) * _dot(attn_ref[0], wa_ref[...])
              + gb_ref[0].astype(F32) * _dot(rw, wb_ref[...]))
    o_ref[0] = x_ref[0] + m[2:3, :] * _dot(merged.astype(BF16), wo_ref[...])


def _post(x, mod, mod_row, attn, y0, y1, bonus, g, ga, gb, p):
    b, t, _ = x.shape
    tm = min(t, 512)
    tok = lambda width: pl.BlockSpec((1, tm, width), lambda i, j: (i, j, 0))
    in_specs = [tok(D_MODEL),
                pl.BlockSpec((1, 6, D_MODEL), lambda i, j: (mod_row(i), 0, 0)),
                tok(ATTN_WIDTH),
                tok(RWKV_WIDTH), tok(RWKV_WIDTH), tok(RWKV_WIDTH), tok(RWKV_WIDTH), tok(D_MODEL), tok(D_MODEL),
                _const_spec((1, RWKV_WIDTH)), _const_spec((1, RWKV_WIDTH)),
                _const_spec((RWKV_WIDTH, RWKV_WIDTH)),
                _const_spec((ATTN_WIDTH, D_MODEL)), _const_spec((RWKV_WIDTH, D_MODEL)),
                _const_spec((D_MODEL, D_MODEL))]
    return pl.pallas_call(
        _post_kernel,
        grid=(b, t // tm),
        in_specs=in_specs,
        out_specs=tok(D_MODEL),
        out_shape=jax.ShapeDtypeStruct((b, t, D_MODEL), F32),
        compiler_params=_params("parallel", "parallel"),
        name="post",
    )(x, mod, attn, y0, y1, bonus, g, ga, gb, p["ln_g"], p["ln_b"], p["bd"], p["w_proj_a"], p["w_proj_b"], p["w_out"])


def _ffn_kernel(x_ref, xp_ref, xn_ref, mod_ref, g2_ref, wup_ref, cw_ref, cb_ref, wdn_ref, gf_ref, o_ref):
    i = pl.program_id(1)
    nt = pl.num_programs(1)
    m = mod_ref[0]
    g2 = g2_ref[...]

    def norm_mod(x):
        ms = jnp.mean(x * x, axis=-1, keepdims=True)
        return x * lax.rsqrt(ms + NORM_EPS) * g2 * (1.0 + m[4:5, :]) + m[3:4, :]

    x = x_ref[0]
    tm = x.shape[0]
    hp = jnp.where(i > 0, norm_mod(xp_ref[0]), 0.0)
    hn = jnp.where(i < nt - 1, norm_mod(xn_ref[0]), 0.0)
    h = jnp.concatenate([hp, norm_mod(x), hn], axis=0).astype(BF16)
    cw = cw_ref[...]
    cb = cb_ref[...]
    halo = 2 * SUBLANES
    interleave = tm // FF_ROWS >= 4
    nrb = tm // FF_ROWS if interleave else 1
    blk_rows = tm // nrb

    def up_piece(j, r):
        lo = j * FF_CHUNK
        hr = h[r * blk_rows:(r + 1) * blk_rows + (halo if r == nrb - 1 else 0), :]
        return _dot(hr, wup_ref[:, lo:lo + FF_CHUNK]), _dot(hr, wup_ref[:, D_FF + lo:D_FF + lo + FF_CHUNK])

    def conv(pieces, half, r, lo):
        if r == nrb - 1:
            u = pieces[r][half]
        else:
            u = jnp.concatenate([pieces[r][half], pieces[r + 1][half][:halo, :]], axis=0)
        w = cw[:, lo:lo + FF_CHUNK]
        out = (pltpu.roll(u, 1, 0) * w[0:1, :] + u * w[1:2, :] + pltpu.roll(u, blk_rows + halo - 1, 0) * w[2:3, :]
               + cb[:, lo:lo + FF_CHUNK])
        return out[SUBLANES:SUBLANES + blk_rows, :]

    nchunks = D_FF // FF_CHUNK
    acc = [jnp.zeros((blk_rows, D_MODEL), F32) for _ in range(nrb)]
    cur = [up_piece(0, r) for r in range(nrb)]
    for j in range(nchunks):
        lo = j * FF_CHUNK
        more = j + 1 < nchunks
        nxt = [up_piece(j + 1, r) for r in range(nrb)] if (more and not interleave) else []
        for r in range(nrb):
            val = conv(cur, 0, r, lo)
            gate = conv(cur, 1, r, D_FF + lo)
            act = (gate * _sigmoid(gate) * val).astype(BF16)
            acc[r] = acc[r] + _dot(act, wdn_ref[lo:lo + FF_CHUNK, :])
            if more and interleave:
                nxt.append(up_piece(j + 1, r))
        cur = nxt
    x2 = x + m[5:6, :] * jnp.concatenate(acc, axis=0)
    ms = jnp.mean(x2 * x2, axis=-1, keepdims=True)
    o_ref[0] = x2 * lax.rsqrt(ms + NORM_EPS) * gf_ref[...]


def _ffn(x, mod, mod_row, p):
    b, t, _ = x.shape
    tm = min(t, 512)
    hb = tm // SUBLANES
    last_hb = t // SUBLANES - 1
    in_specs = [pl.BlockSpec((1, tm, D_MODEL), lambda i, j: (i, j, 0)),
                pl.BlockSpec((1, SUBLANES, D_MODEL), lambda i, j: (i, jnp.maximum(j * hb - 1, 0), 0)),
                pl.BlockSpec((1, SUBLANES, D_MODEL), lambda i, j: (i, jnp.minimum((j + 1) * hb, last_hb), 0)),
                pl.BlockSpec((1, 6, D_MODEL), lambda i, j: (mod_row(i), 0, 0)),
                _const_spec((1, D_MODEL)),
                _const_spec((D_MODEL, 2 * D_FF)), _const_spec((3, 2 * D_FF)), _const_spec((1, 2 * D_FF)),
                _const_spec((D_FF, D_MODEL)), _const_spec((1, D_MODEL))]
    return pl.pallas_call(
        _ffn_kernel,
        grid=(b, t // tm),
        in_specs=in_specs,
        out_specs=pl.BlockSpec((1, tm, D_MODEL), lambda i, j: (i, j, 0)),
        out_shape=jax.ShapeDtypeStruct((b, t, D_MODEL), F32),
        compiler_params=_params("parallel", "parallel"),
        name="ffn",
    )(x, x, x, mod, p["g_norm2"], p["w_ffn_up"], p["conv_w"], p["conv_b"], p["w_ffn_down"], p["g_final"])


def _rope_tables(t):
    rows = t // GRID_W
    row = jnp.repeat(jnp.arange(rows), GRID_W).astype(F32)
    col = jnp.tile(jnp.arange(GRID_W), rows).astype(F32)
    freqs = ROPE_BASE ** (-jnp.arange(ROPE_PAIRS, dtype=F32) / ROPE_PAIRS)
    ar = row[:, None] * freqs
    ac = col[:, None] * freqs
    cos = jnp.concatenate([jnp.cos(ar), jnp.cos(ar), jnp.cos(ac), jnp.cos(ac)], axis=1)
    sin = jnp.concatenate([-jnp.sin(ar), jnp.sin(ar), -jnp.sin(ac), jnp.sin(ac)], axis=1)
    return jnp.tile(cos, (1, 2)), jnp.tile(sin, (1, 2))


def _block_diag2(w):
    z = jnp.zeros_like(w[0])
    return jnp.concatenate([jnp.concatenate([w[0], z], axis=1), jnp.concatenate([z, w[1]], axis=1)], axis=0)


def _trunk(x, mod, mod_row_tile, mod_row_seq, p, ctx):
    b, t, _ = x.shape
    rope = _rope_tables(t) if ctx is not None else None
    kv_dtype = BF16 if ctx is not None else F32
    q, k, v, ga, gb, zr = _in_proj(x.reshape(b * t, D_MODEL), mod, p["g_norm1"], p["w_in"], t,
                                   mod_row_tile, rope, kv_dtype)
    seq = lambda arr: arr.reshape(b, t, arr.shape[-1])
    q, k, v, ga, gb, zr = seq(q), seq(k), seq(v), seq(ga), seq(gb), seq(zr)
    if ctx is None:
        attn = _attn_ctx(q, k, v, p["sink"])
        s0 = None
    else:
        k_ctx, v_ctx, s0 = ctx
        attn = _attn_lat(q, k, v, k_ctx, v_ctx, p["sink"])
    lw, kd, bb, a, r, vr, g, bonus = _rw_prep(zr, p)
    y0, y1, s_final = _rw_scan(s0, lw, kd, bb, a, r, vr)
    x1 = _post(x, mod, mod_row_seq, attn, y0, y1, bonus, g, ga, gb, p)
    out = _ffn(x1, mod, mod_row_seq, p)
    return out, k, v, s_final


def kernel(x_prompt, x_sample, c, cache_k, cache_v, state_rwkv, c_ctx, w_ada, b_ada, g_norm1, w_in, attn_sink, w_proj_a, w_proj_b, rwkv_mu, rwkv_w0, rwkv_w2, rwkv_a0, rwkv_a2, rwkv_k_k, rwkv_k_a, rwkv_r_k, rwkv_g2, rwkv_ln_g, rwkv_ln_b, w_out, g_norm2, w_ffn_up, ffn_conv_w, ffn_conv_b, w_ffn_down, g_final):
    depth = w_ada.shape[0]
    assert depth == 1, "single trunk layer"
    l = 0
    nb, seq, _ = x_prompt.shape
    db, dseq, _ = x_sample.shape
    past = cache_k.shape[2]

    head_id = jnp.arange(RWKV_WIDTH) // RWKV_HEAD_SIZE
    p = dict(
        g_norm1=g_norm1[l].reshape(1, D_MODEL),
        w_in=w_in[l].astype(BF16),
        sink=attn_sink[l],
        mu=rwkv_mu[l],
        w2=_block_diag2(rwkv_w2[l]).astype(BF16),
        w0=rwkv_w0[l].reshape(1, N_DIR * RWKV_WIDTH),
        a2=_block_diag2(rwkv_a2[l]).astype(BF16),
        a0=rwkv_a0[l].reshape(1, N_DIR * RWKV_WIDTH),
        g2=rwkv_g2[l].astype(BF16),
        k_k=rwkv_k_k[l].reshape(1, RWKV_WIDTH),
        k_a=rwkv_k_a[l].reshape(1, RWKV_WIDTH),
        r_k=rwkv_r_k[l].reshape(1, RWKV_WIDTH),
        bd=(head_id[:, None] == head_id[None, :]).astype(BF16),
        ln_g=rwkv_ln_g[l].reshape(1, RWKV_WIDTH),
        ln_b=rwkv_ln_b[l].reshape(1, RWKV_WIDTH),
        w_proj_a=w_proj_a[l].astype(BF16),
        w_proj_b=w_proj_b[l].astype(BF16),
        w_out=w_out[l].astype(BF16),
        g_norm2=g_norm2[l].reshape(1, D_MODEL),
        w_ffn_up=w_ffn_up[l].astype(BF16),
        conv_w=ffn_conv_w[l],
        conv_b=ffn_conv_b[l].reshape(1, 2 * D_FF),
        w_ffn_down=w_ffn_down[l].astype(BF16),
        g_final=g_final.reshape(1, D_MODEL),
    )

    ctx_row = db
    mod_rows = 2 * SUBLANES
    cond = jnp.zeros((mod_rows, D_MODEL), F32).at[:db].set(c).at[ctx_row].set(c_ctx)
    mod = _modulation(cond, w_ada[l], b_ada[l]).reshape(mod_rows, 6, D_MODEL)

    y_prompt, kc, vc, sc = _trunk(x_prompt, mod, lambda i: ctx_row, lambda i: ctx_row, p, None)

    lat_tiles = max(dseq // 512, 1)
    ctx = (cache_k[:, l].reshape(db, past, KV_WIDTH), cache_v[:, l].reshape(db, past, KV_WIDTH), state_rwkv[:, l])
    y_sample, _, _, _ = _trunk(x_sample, mod, lambda i: i // lat_tiles, lambda i: i, p, ctx)

    new_cache_k = kc.reshape(nb, 1, seq, N_KV_HEADS, HEAD_DIM)
    new_cache_v = vc.reshape(nb, 1, seq, N_KV_HEADS, HEAD_DIM)
    new_state = sc.reshape(nb, 1, N_DIR, RWKV_HEADS, RWKV_HEAD_SIZE, RWKV_HEAD_SIZE)
    return (y_prompt, y_sample, new_cache_k, new_cache_v, new_state)
```

```python
import functools

import jax
import jax.numpy as jnp
from jax import lax
from jax.experimental import pallas as pl
from jax.experimental.pallas import tpu as pltpu

F32 = jnp.float32
BF16 = jnp.bfloat16

D_MODEL = 1024
GRID_W = 64
HEAD_DIM = 64
N_Q_HEADS = 8
N_KV_HEADS = 2
GQA_GROUP = N_Q_HEADS // N_KV_HEADS
ATTN_WIDTH = N_Q_HEADS * HEAD_DIM
KV_WIDTH = N_KV_HEADS * HEAD_DIM
WINDOW = 128
BLOCK = 128
ROPE_BASE = 10000.0
ROPE_PAIRS = HEAD_DIM // 4
RWKV_HEADS = 8
RWKV_HEAD_SIZE = 64
RWKV_WIDTH = RWKV_HEADS * RWKV_HEAD_SIZE
N_DIR = 2
DECAY_RANK = 64
ICLR_RANK = 64
GATE_RANK = 128
RW_SHIFT_COLS = 3 * RWKV_WIDTH + N_DIR * DECAY_RANK + N_DIR * ICLR_RANK + GATE_RANK
D_FF = 2816
NORM_EPS = 1e-6
GN_EPS = 64e-5
MASK_VALUE = -1e30

Q_OFF = 0
K_OFF = ATTN_WIDTH
V_OFF = K_OFF + KV_WIDTH
GA_OFF = V_OFF + KV_WIDTH
GB_OFF = GA_OFF + D_MODEL
ZR_OFF = GB_OFF + D_MODEL
W_IN_COLS = ZR_OFF + RW_SHIFT_COLS

ZW_OFF = 3 * RWKV_WIDTH
ZA_OFF = ZW_OFF + N_DIR * DECAY_RANK
ZG_OFF = ZA_OFF + N_DIR * ICLR_RANK

SCAN_CHUNK = 64
NEUMANN_STEPS = 6
SUBLANES = 8
BF16_ROWS = 16
ATTN_UNIT = 2
ATTN_QBLOCKS = 2
FF_CHUNK = 256
FF_ROWS = 128
VMEM_LIMIT = 56 * 1024 * 1024

_NT = (((1,), (1,)), ((), ()))
_TN = (((0,), (0,)), ((), ()))


def _params(*sem):
    return pltpu.CompilerParams(dimension_semantics=sem, vmem_limit_bytes=VMEM_LIMIT)


def _const_spec(shape):
    nd = len(shape)
    return pl.BlockSpec(shape, lambda *_: (0,) * nd, pipeline_mode=pl.Buffered(1))


def _dot(a, b):
    return jnp.dot(a, b, preferred_element_type=F32)


def _split_dot(x, m):
    hi = x.astype(BF16)
    lo = (x - hi.astype(F32)).astype(BF16)
    return _dot(hi, m) + _dot(lo, m)


def _sigmoid(x):
    return 1.0 / (1.0 + jnp.exp(-x))


def _mod_kernel(c_ref, w_ref, b_ref, o_ref):
    c = c_ref[...]
    s = (c * _sigmoid(c)).astype(BF16)
    o_ref[...] = _dot(s, w_ref[...].astype(BF16)) + b_ref[...]


def _modulation(cond, w_ada, b_ada):
    rows = cond.shape[0]
    n = w_ada.shape[1]
    tn = D_MODEL
    return pl.pallas_call(
        _mod_kernel,
        grid=(n // tn,),
        in_specs=[pl.BlockSpec((rows, D_MODEL), lambda j: (0, 0)),
                  pl.BlockSpec((D_MODEL, tn), lambda j: (0, j)),
                  pl.BlockSpec((1, tn), lambda j: (0, j))],
        out_specs=pl.BlockSpec((rows, tn), lambda j: (0, j)),
        out_shape=jax.ShapeDtypeStruct((rows, n), F32),
        compiler_params=_params("parallel"),
        name="modulation",
    )(cond, w_ada, b_ada.reshape(1, n))


def _swap16(x):
    w = x.shape[1]
    lane = lax.broadcasted_iota(jnp.int32, x.shape, 1)
    first = (lane % 32) < 16
    return jnp.where(first, pltpu.roll(x, w - 16, 1), pltpu.roll(x, 16, 1))


def _in_proj_kernel(*refs, rope):
    if rope:
        x_ref, mod_ref, g_ref, w_ref, cos_ref, sin_ref, q_ref, k_ref, v_ref, ga_ref, gb_ref, zr_ref = refs
    else:
        x_ref, mod_ref, g_ref, w_ref, q_ref, k_ref, v_ref, ga_ref, gb_ref, zr_ref = refs
    x = x_ref[...]
    m = mod_ref[0]
    ms = jnp.mean(x * x, axis=-1, keepdims=True)
    h = x * lax.rsqrt(ms + NORM_EPS) * g_ref[...]
    h = (h * (1.0 + m[1:2, :]) + m[0:1, :]).astype(BF16)

    q = _dot(h, w_ref[:, Q_OFF:K_OFF])
    k = _dot(h, w_ref[:, K_OFF:V_OFF])
    v = _dot(h, w_ref[:, V_OFF:GA_OFF])
    if rope:
        cos = cos_ref[...]
        sin = sin_ref[...]
        reps = ATTN_WIDTH // cos.shape[1]
        cos_q = jnp.concatenate([cos] * reps, axis=1)
        sin_q = jnp.concatenate([sin] * reps, axis=1)
        q = q * cos_q + _swap16(q) * sin_q
        k = k * cos + _swap16(k) * sin
    q_ref[...] = (q * (HEAD_DIM ** -0.5)).astype(q_ref.dtype)
    k_ref[...] = k.astype(k_ref.dtype)
    v_ref[...] = v.astype(v_ref.dtype)
    ga_ref[...] = _sigmoid(_dot(h, w_ref[:, GA_OFF:GB_OFF])).astype(ga_ref.dtype)
    gb_ref[...] = _sigmoid(_dot(h, w_ref[:, GB_OFF:ZR_OFF])).astype(gb_ref.dtype)
    zr_ref[...] = _dot(h, w_ref[:, ZR_OFF:W_IN_COLS]).astype(zr_ref.dtype)


def _in_proj(x, mod, g1, w_in, seq_len, mod_row, rope_tabs, kv_dtype):
    ntok = x.shape[0]
    tm = min(512, seq_len) if rope_tabs is not None else 512
    tiles_per_seq = max(seq_len // tm, 1)
    rope = rope_tabs is not None
    tok = lambda width: pl.BlockSpec((tm, width), lambda i: (i, 0))
    in_specs = [tok(D_MODEL),
                pl.BlockSpec((1, 6, D_MODEL), lambda i: (mod_row(i), 0, 0)),
                _const_spec((1, D_MODEL)),
                _const_spec((D_MODEL, W_IN_COLS))]
    args = [x, mod, g1, w_in]
    if rope:
        in_specs += [pl.BlockSpec((tm, 2 * HEAD_DIM), lambda i: (i % tiles_per_seq, 0))] * 2
        args += list(rope_tabs)
    out_shape = [jax.ShapeDtypeStruct((ntok, ATTN_WIDTH), BF16),
                 jax.ShapeDtypeStruct((ntok, KV_WIDTH), kv_dtype),
                 jax.ShapeDtypeStruct((ntok, KV_WIDTH), kv_dtype),
                 jax.ShapeDtypeStruct((ntok, D_MODEL), BF16),
                 jax.ShapeDtypeStruct((ntok, D_MODEL), BF16),
                 jax.ShapeDtypeStruct((ntok, RW_SHIFT_COLS), BF16)]
    out_specs = [tok(ATTN_WIDTH), tok(KV_WIDTH), tok(KV_WIDTH), tok(D_MODEL), tok(D_MODEL), tok(RW_SHIFT_COLS)]
    return pl.pallas_call(
        functools.partial(_in_proj_kernel, rope=rope),
        grid=(ntok // tm,),
        in_specs=in_specs, out_specs=out_specs, out_shape=out_shape,
        compiler_params=_params("parallel"),
        name="in_proj_rope" if rope else "in_proj",
    )(*args)


def _softmax_pv(s, sink_col, v):
    m = jnp.maximum(jnp.max(s, axis=-1, keepdims=True), sink_col)
    p = jnp.exp(s - m)
    denom = jnp.sum(p, axis=-1, keepdims=True) + jnp.exp(sink_col - m)
    return _dot(p.astype(BF16), v) / denom


def _attend(problems, sink_ref):
    units = [list(range(u, u + ATTN_UNIT)) for u in range(0, N_Q_HEADS, ATTN_UNIT)]
    kv_of = lambda heads: slice((heads[0] // GQA_GROUP) * HEAD_DIM, (heads[0] // GQA_GROUP + 1) * HEAD_DIM)
    scores = []
    for q, k, _, mask in problems:
        for heads in units:
            qu = jnp.concatenate([q[:, h * HEAD_DIM:(h + 1) * HEAD_DIM] for h in heads], axis=0)
            scores.append(mask(lax.dot_general(qu, k[:, kv_of(heads)], _NT, preferred_element_type=F32)))
    results = []
    for pi, (q, _, v, _) in enumerate(problems):
        t = q.shape[0]
        outs = []
        for ui, heads in enumerate(units):
            sink_col = jnp.concatenate([jnp.full((t, 1), sink_ref[h], F32) for h in heads], axis=0)
            o = _softmax_pv(scores[pi * len(units) + ui], sink_col, v[:, kv_of(heads)])
            outs += [o[i * t:(i + 1) * t, :] for i in range(ATTN_UNIT)]
        results.append(jnp.concatenate(outs, axis=1))
    return results


def _attn_ctx_kernel(sink_ref, q_ref, k_ref, v_ref, o_ref):
    out, = _attend([(q_ref[0], k_ref[0].astype(BF16), v_ref[0].astype(BF16), lambda s: s)], sink_ref)
    o_ref[0] = out.astype(o_ref.dtype)


def _attn_ctx(q, k, v, sink):
    b, t, _ = q.shape
    seq = lambda width: pl.BlockSpec((1, t, width), lambda i: (i, 0, 0))
    return pl.pallas_call(
        _attn_ctx_kernel,
        grid=(b,),
        in_specs=[pl.BlockSpec(memory_space=pltpu.SMEM), seq(ATTN_WIDTH), seq(KV_WIDTH), seq(KV_WIDTH)],
        out_specs=seq(ATTN_WIDTH),
        out_shape=jax.ShapeDtypeStruct((b, t, ATTN_WIDTH), BF16),
        compiler_params=_params("parallel"),
        name="attn_ctx",
    )(sink, q, k, v)


def _attn_lat_kernel(sink_ref, q_ref, kp_ref, kc_ref, kn_ref, vp_ref, vc_ref, vn_ref, kx_ref, vx_ref, o_ref):
    m = pl.program_id(1)
    nb = ATTN_QBLOCKS * pl.num_programs(1)
    kc, vc = kc_ref[0], vc_ref[0]
    kblk = [kp_ref[0]] + [kc[s * BLOCK:(s + 1) * BLOCK, :] for s in range(ATTN_QBLOCKS)] + [kn_ref[0]]
    vblk = [vp_ref[0]] + [vc[s * BLOCK:(s + 1) * BLOCK, :] for s in range(ATTN_QBLOCKS)] + [vn_ref[0]]
    kx = kx_ref[0].astype(BF16)
    vx = vx_ref[0].astype(BF16)
    rows = ATTN_UNIT * BLOCK
    rq = lax.broadcasted_iota(jnp.int32, (rows, BLOCK), 0) % BLOCK
    col = lax.broadcasted_iota(jnp.int32, (rows, BLOCK), 1)

    def make_mask(n):
        keep_prev = col >= rq + jnp.where(n > 0, 0, BLOCK)
        keep_next = col <= rq - jnp.where(n < nb - 1, 0, BLOCK)

        def mask(s):
            return jnp.concatenate(
                [jnp.where(keep_prev, s[:, :BLOCK], MASK_VALUE), s[:, BLOCK:2 * BLOCK],
                 jnp.where(keep_next, s[:, 2 * BLOCK:3 * BLOCK], MASK_VALUE), s[:, 3 * BLOCK:]], axis=1)
        return mask

    problems = []
    for s in range(ATTN_QBLOCKS):
        k = jnp.concatenate(kblk[s:s + 3] + [kx], axis=0)
        v = jnp.concatenate(vblk[s:s + 3] + [vx], axis=0)
        problems.append((q_ref[0, s * BLOCK:(s + 1) * BLOCK, :], k, v, make_mask(m * ATTN_QBLOCKS + s)))
    outs = _attend(problems, sink_ref)
    o_ref[0] = jnp.concatenate(outs, axis=0).astype(o_ref.dtype)


def _attn_lat(q, k, v, k_ctx, v_ctx, sink):
    assert WINDOW == BLOCK, "the block-triangular band masks assume one block of reach on either side"
    b, t, _ = q.shape
    nb = t // BLOCK
    p = k_ctx.shape[1]
    qb = ATTN_QBLOCKS
    blk = lambda width, f: pl.BlockSpec((1, BLOCK, width), lambda i, m: (i, f(m), 0))
    cur = lambda width: pl.BlockSpec((1, qb * BLOCK, width), lambda i, m: (i, m, 0))
    prev = lambda m: jnp.maximum(m * qb - 1, 0)
    nxt = lambda m: jnp.minimum((m + 1) * qb, nb - 1)
    ctx = pl.BlockSpec((1, p, KV_WIDTH), lambda i, m: (i, 0, 0))
    return pl.pallas_call(
        _attn_lat_kernel,
        grid=(b, nb // qb),
        in_specs=[pl.BlockSpec(memory_space=pltpu.SMEM), cur(ATTN_WIDTH),
                  blk(KV_WIDTH, prev), cur(KV_WIDTH), blk(KV_WIDTH, nxt),
                  blk(KV_WIDTH, prev), cur(KV_WIDTH), blk(KV_WIDTH, nxt), ctx, ctx],
        out_specs=cur(ATTN_WIDTH),
        out_shape=jax.ShapeDtypeStruct((b, t, ATTN_WIDTH), BF16),
        compiler_params=_params("parallel", "parallel"),
        name="attn_lat",
    )(sink, q, k, k, k, v, v, v, k_ctx, v_ctx)


def _rw_prep_kernel(z_ref, zp_ref, zn_ref, mu_ref, w2_ref, w0_ref, a2_ref, a0_ref, g2_ref, kk_ref, ka_ref,
                    rk_ref, bd_ref, lw_ref, kd_ref, bb_ref, a_ref, r_ref, v_ref, g_ref, bonus_ref):
    i = pl.program_id(1)
    nt = pl.num_programs(1)
    z = z_ref[0].astype(F32)
    tm = z.shape[0]
    row = lax.broadcasted_iota(jnp.int32, z.shape, 0)
    halo_prev = jnp.where(i > 0, zp_ref[0].astype(F32)[BF16_ROWS - 1:BF16_ROWS, :], 0.0)
    halo_next = jnp.where(i < nt - 1, zn_ref[0].astype(F32)[0:1, :], 0.0)
    prev = jnp.where(row == 0, halo_prev, pltpu.roll(z, 1, 0))
    nxt = jnp.where(row == tm - 1, halo_next, pltpu.roll(z, tm - 1, 0))
    mu = mu_ref[...]
    z = z + mu[0:1, :] * (prev - z) + mu[1:2, :] * (nxt - z)

    r = z[:, 0:RWKV_WIDTH]
    k = z[:, RWKV_WIDTH:2 * RWKV_WIDTH]
    v = z[:, 2 * RWKV_WIDTH:3 * RWKV_WIDTH]
    zw = z[:, ZW_OFF:ZA_OFF]
    za = z[:, ZA_OFF:ZG_OFF]
    zg = z[:, ZG_OFF:RW_SHIFT_COLS]

    logit = w0_ref[...] + _dot(jnp.tanh(zw).astype(BF16), w2_ref[...])
    lw = -jnp.exp(F32(-0.5)) * _sigmoid(logit)
    a = _sigmoid(a0_ref[...] + _dot(za.astype(BF16), a2_ref[...]))
    g_ref[0] = _dot(_sigmoid(zg).astype(BF16), g2_ref[...]).astype(g_ref.dtype)

    bd = bd_ref[...]
    kk = k * kk_ref[...]
    kk = kk * lax.rsqrt(jnp.maximum(_split_dot(kk * kk, bd), 1e-24))
    ka = ka_ref[...]
    ksum = jnp.zeros_like(k)
    for d in range(N_DIR):
        ad = a[:, d * RWKV_WIDTH:(d + 1) * RWKV_WIDTH]
        kd = k * (1.0 + (ad - 1.0) * ka)
        lw_ref[d, 0] = lw[:, d * RWKV_WIDTH:(d + 1) * RWKV_WIDTH]
        kd_ref[d, 0] = kd.astype(kd_ref.dtype)
        bb_ref[d, 0] = (ad * kk).astype(bb_ref.dtype)
        ksum = ksum + kd
    a_ref[0] = (-kk).astype(a_ref.dtype)
    r_ref[0] = r.astype(r_ref.dtype)
    v_ref[0] = v.astype(v_ref.dtype)
    bonus_ref[0] = (_split_dot(r * ksum * rk_ref[...], bd) * v).astype(bonus_ref.dtype)


def _rw_prep(zr, p):
    b, t, _ = zr.shape
    tm = 256
    nt = t // tm
    hb = tm // BF16_ROWS
    last_hb = t // BF16_ROWS - 1
    tok = lambda: pl.BlockSpec((1, tm, RWKV_WIDTH), lambda i, j: (i, j, 0))
    tok2 = lambda: pl.BlockSpec((N_DIR, 1, tm, RWKV_WIDTH), lambda i, j: (0, i, j, 0))
    two = jax.ShapeDtypeStruct((N_DIR, b, t, RWKV_WIDTH), F32)
    one_bf = jax.ShapeDtypeStruct((b, t, RWKV_WIDTH), BF16)
    two_bf = jax.ShapeDtypeStruct((N_DIR, b, t, RWKV_WIDTH), BF16)
    in_specs = [pl.BlockSpec((1, tm, RW_SHIFT_COLS), lambda i, j: (i, j, 0)),
                pl.BlockSpec((1, BF16_ROWS, RW_SHIFT_COLS), lambda i, j: (i, jnp.maximum(j * hb - 1, 0), 0)),
                pl.BlockSpec((1, BF16_ROWS, RW_SHIFT_COLS), lambda i, j: (i, jnp.minimum((j + 1) * hb, last_hb), 0)),
                _const_spec((2, RW_SHIFT_COLS)),
                _const_spec((N_DIR * DECAY_RANK, N_DIR * RWKV_WIDTH)), _const_spec((1, N_DIR * RWKV_WIDTH)),
                _const_spec((N_DIR * ICLR_RANK, N_DIR * RWKV_WIDTH)), _const_spec((1, N_DIR * RWKV_WIDTH)),
                _const_spec((GATE_RANK, RWKV_WIDTH)),
                _const_spec((1, RWKV_WIDTH)), _const_spec((1, RWKV_WIDTH)), _const_spec((1, RWKV_WIDTH)),
                _const_spec((RWKV_WIDTH, RWKV_WIDTH))]
    return pl.pallas_call(
        _rw_prep_kernel,
        grid=(b, nt),
        in_specs=in_specs,
        out_specs=[tok2(), tok2(), tok2(), tok(), tok(), tok(), tok(), tok()],
        out_shape=[two, two_bf, two_bf, one_bf, one_bf, one_bf, one_bf, one_bf],
        compiler_params=_params("parallel", "parallel"),
        name="rw_prep",
    )(zr, zr, zr, p["mu"], p["w2"], p["w0"], p["a2"], p["a0"], p["g2"], p["k_k"], p["k_a"], p["r_k"], p["bd"])


def _rw_scan_kernel(*refs, has_s0, nchunks):
    if has_s0:
        s0_ref, refs = refs[0], refs[1:]
    (lw0_ref, kd0_ref, bb0_ref, a0_ref, r0_ref, v0_ref, lw1_ref, kd1_ref, bb1_ref, a1_ref, r1_ref, v1_ref,
     y0_ref, y1_ref, sf_ref, s_scr) = refs
    dir_refs = ((lw0_ref, kd0_ref, bb0_ref, a0_ref, r0_ref, v0_ref, y0_ref),
                (lw1_ref, kd1_ref, bb1_ref, a1_ref, r1_ref, v1_ref, y1_ref))
    i = pl.program_id(1)
    c = SCAN_CHUNK
    n = RWKV_HEAD_SIZE

    @pl.when(i == 0)
    def _():
        if has_s0:
            s_scr[...] = s0_ref[0]
        else:
            s_scr[...] = jnp.zeros_like(s_scr)

    diff = lax.broadcasted_iota(jnp.int32, (c, c), 0) - lax.broadcasted_iota(jnp.int32, (c, c), 1)
    incl = (diff >= 0, diff <= 0)
    strict = (diff > 0, diff < 0)
    diff2 = (lax.broadcasted_iota(jnp.int32, (c, 2 * c), 0) - lax.broadcasted_iota(jnp.int32, (c, 2 * c), 1) % c)
    incl2 = (diff2 >= 0, diff2 <= 0)
    chains = [(d, h) for d in range(N_DIR) for h in range(RWKV_HEADS)]
    nch = len(chains)
    dirs = [d for d, _ in chains]
    each = lambda f: [f(d, slice(h * n, (h + 1) * n)) for d, h in chains]

    def chunk(jj, carry):
        pre = []
        for d in range(N_DIR):
            lw_ref, kd_ref, bb_ref, a_ref, r_ref, v_ref, _ = dir_refs[d]
            j = jj if d == 0 else nchunks - 1 - jj
            rows = pl.ds(pl.multiple_of(j * c, c), c)
            lw = lw_ref[0, 0, rows, :]
            kd = kd_ref[0, 0, rows, :].astype(F32)
            bb = bb_ref[0, 0, rows, :].astype(F32)
            l1 = lw.astype(BF16)
            rem = lw - l1.astype(F32)
            l2 = rem.astype(BF16)
            l3 = (rem - l2.astype(F32)).astype(BF16)
            tri = incl[d].astype(BF16)
            cum = _dot(jnp.concatenate([tri, tri, tri], axis=1), jnp.concatenate([l1, l2, l3], axis=0))
            ctot = jnp.sum(lw, axis=0, keepdims=True)
            en = jnp.exp(-cum)
            eh = jnp.exp(ctot - cum)
            pre.append(dict(
                rows=rows,
                at=a_ref[0, rows, :].astype(F32) * jnp.exp(cum - lw),
                rt=r_ref[0, rows, :].astype(F32) * jnp.exp(cum),
                bt=(bb * en).astype(BF16), kt=(kd * en).astype(BF16),
                bh=(bb * eh).astype(BF16), kh=(kd * eh).astype(BF16),
                gam=jnp.exp(ctot), v=v_ref[0, rows, :]))

        at = each(lambda d, hs: pre[d]["at"][:, hs])
        rt = each(lambda d, hs: pre[d]["rt"][:, hs])
        vh = each(lambda d, hs: pre[d]["v"][:, hs])
        bh = each(lambda d, hs: pre[d]["bh"][:, hs])
        kh = each(lambda d, hs: pre[d]["kh"][:, hs])
        gam = each(lambda d, hs: pre[d]["gam"][:, hs])
        amat = each(lambda d, hs: lax.dot_general(
            jnp.concatenate([pre[d]["at"][:, hs], pre[d]["rt"][:, hs]], axis=0).astype(BF16),
            jnp.concatenate([pre[d]["bt"][:, hs], pre[d]["kt"][:, hs]], axis=0), _NT, preferred_element_type=F32))
        a_ak = [jnp.where(strict[dirs[q]], amat[q][:c, c:], 0.0).astype(BF16) for q in range(nch)]
        a_r = [jnp.where(incl2[dirs[q]], amat[q][c:, :], 0.0).astype(BF16) for q in range(nch)]
        x = [jnp.where(strict[dirs[q]], amat[q][:c, :c], 0.0) for q in range(nch)]
        z = [jnp.concatenate([at[q], _dot(a_ak[q], vh[q])], axis=1) for q in range(nch)]
        for step in range(NEUMANN_STEPS):
            xb = [x[q].astype(BF16) for q in range(nch)]
            if step < NEUMANN_STEPS - 1:
                prod = [_dot(xb[q], jnp.concatenate([z[q].astype(BF16), xb[q]], axis=1)) for q in range(nch)]
                z = [z[q] + prod[q][:, :2 * n] for q in range(nch)]
                x = [prod[q][:, 2 * n:] for q in range(nch)]
            else:
                z = [z[q] + _dot(xb[q], z[q].astype(BF16)) for q in range(nch)]
        zero = jnp.zeros((c, n), BF16)
        rmat = [jnp.concatenate([z[q].astype(BF16), jnp.concatenate([zero, vh[q]], axis=1)], axis=0)
                for q in range(nch)]
        qy = [_dot(a_r[q], rmat[q]) for q in range(nch)]
        qm = [(rt[q] + qy[q][:, :n]).astype(BF16) for q in range(nch)]
        mn = [lax.dot_general(rmat[q], jnp.concatenate([bh[q], kh[q]], axis=0), _TN, preferred_element_type=F32)
              for q in range(nch)]
        s = [s_scr[d, h] for d, h in chains]
        sb = [s[q].astype(BF16) for q in range(nch)]
        ys = [lax.dot_general(qm[q], sb[q], _NT, preferred_element_type=F32) + qy[q][:, n:] for q in range(nch)]
        for q, (d, h) in enumerate(chains):
            s_scr[d, h] = s[q] * gam[q] + _dot(sb[q], mn[q][:n, :].astype(BF16)) + mn[q][n:, :]
        for d in range(N_DIR):
            y_ref = dir_refs[d][-1]
            y_ref[0, pre[d]["rows"], :] = jnp.concatenate(ys[d * RWKV_HEADS:(d + 1) * RWKV_HEADS], axis=1)
        return carry

    lax.fori_loop(0, nchunks, chunk, 0, unroll=2)

    @pl.when(i == pl.num_programs(1) - 1)
    def _():
        sf_ref[0] = s_scr[...]


def _rw_scan(s0, lw, kd, bb, a, r, v):
    _, b, t, _ = lw.shape
    tb = min(t, 512)
    nblk = t // tb
    blk = (lambda i: i, lambda i: nblk - 1 - i)
    dir_spec = lambda d: pl.BlockSpec((1, 1, tb, RWKV_WIDTH), lambda bi, i: (d, bi, blk[d](i), 0))
    tok_spec = lambda d: pl.BlockSpec((1, tb, RWKV_WIDTH), lambda bi, i: (bi, blk[d](i), 0))
    st_spec = pl.BlockSpec((1, N_DIR, RWKV_HEADS, RWKV_HEAD_SIZE, RWKV_HEAD_SIZE), lambda bi, i: (bi, 0, 0, 0, 0))
    has_s0 = s0 is not None
    in_specs, args = [], []
    if has_s0:
        in_specs.append(st_spec)
        args.append(s0)
    for d in range(N_DIR):
        in_specs += [dir_spec(d), dir_spec(d), dir_spec(d), tok_spec(d), tok_spec(d), tok_spec(d)]
        args += [lw, kd, bb, a, r, v]
    y_shape = jax.ShapeDtypeStruct((b, t, RWKV_WIDTH), F32)
    return pl.pallas_call(
        functools.partial(_rw_scan_kernel, has_s0=has_s0, nchunks=tb // SCAN_CHUNK),
        grid=(b, nblk),
        in_specs=in_specs,
        out_specs=[tok_spec(0), tok_spec(1), st_spec],
        out_shape=[y_shape, y_shape,
                   jax.ShapeDtypeStruct((b, N_DIR, RWKV_HEADS, RWKV_HEAD_SIZE, RWKV_HEAD_SIZE), F32)],
        scratch_shapes=[pltpu.VMEM((N_DIR, RWKV_HEADS, RWKV_HEAD_SIZE, RWKV_HEAD_SIZE), F32)],
        compiler_params=_params("parallel", "arbitrary"),
        name="rw_scan",
    )(*args)


def _post_kernel(x_ref, mod_ref, attn_ref, y0_ref, y1_ref, bonus_ref, g_ref, ga_ref, gb_ref, lng_ref, lnb_ref,
                 bd_ref, wa_ref, wb_ref, wo_ref, o_ref):
    m = mod_ref[0]
    y = y0_ref[0] + y1_ref[0]
    bd = bd_ref[...]
    inv_n = 1.0 / RWKV_HEAD_SIZE
    mean = _split_dot(y, bd) * inv_n
    yc = y - mean
    var = _split_dot(yc * yc, bd) * inv_n
    yn = yc * lax.rsqrt(var + GN_EPS) * lng_ref[...] + lnb_ref[...]
    rw = ((yn + bonus_ref[0].astype(F32)) * g_ref[0].astype(F32)).astype(BF16)
    merged = (ga_ref[0].astype(F32) * _dot(attn_ref[0], wa_ref[...])
              + gb_ref[0].astype(F32) * _dot(rw, wb_ref[...]))
    o_ref[0] = x_ref[0] + m[2:3, :] * _dot(merged.astype(BF16), wo_ref[...])


def _post(x, mod, mod_row, attn, y0, y1, bonus, g, ga, gb, p):
    b, t, _ = x.shape
    tm = min(t, 512)
    tok = lambda width: pl.BlockSpec((1, tm, width), lambda i, j: (i, j, 0))
    in_specs = [tok(D_MODEL),
                pl.BlockSpec((1, 6, D_MODEL), lambda i, j: (mod_row(i), 0, 0)),
                tok(ATTN_WIDTH),
                tok(RWKV_WIDTH), tok(RWKV_WIDTH), tok(RWKV_WIDTH), tok(RWKV_WIDTH), tok(D_MODEL), tok(D_MODEL),
                _const_spec((1, RWKV_WIDTH)), _const_spec((1, RWKV_WIDTH)),
                _const_spec((RWKV_WIDTH, RWKV_WIDTH)),
                _const_spec((ATTN_WIDTH, D_MODEL)), _const_spec((RWKV_WIDTH, D_MODEL)),
                _const_spec((D_MODEL, D_MODEL))]
    return pl.pallas_call(
        _post_kernel,
        grid=(b, t // tm),
        in_specs=in_specs,
        out_specs=tok(D_MODEL),
        out_shape=jax.ShapeDtypeStruct((b, t, D_MODEL), F32),
        compiler_params=_params("parallel", "parallel"),
        name="post",
    )(x, mod, attn, y0, y1, bonus, g, ga, gb, p["ln_g"], p["ln_b"], p["bd"], p["w_proj_a"], p["w_proj_b"], p["w_out"])


def _ffn_kernel(x_ref, xp_ref, xn_ref, mod_ref, g2_ref, wup_ref, cw_ref, cb_ref, wdn_ref, gf_ref, o_ref):
    i = pl.program_id(1)
    nt = pl.num_programs(1)
    m = mod_ref[0]
    g2 = g2_ref[...]

    def norm_mod(x):
        ms = jnp.mean(x * x, axis=-1, keepdims=True)
        return x * lax.rsqrt(ms + NORM_EPS) * g2 * (1.0 + m[4:5, :]) + m[3:4, :]

    x = x_ref[0]
    tm = x.shape[0]
    hp = jnp.where(i > 0, norm_mod(xp_ref[0]), 0.0)
    hn = jnp.where(i < nt - 1, norm_mod(xn_ref[0]), 0.0)
    h = jnp.concatenate([hp, norm_mod(x), hn], axis=0).astype(BF16)
    cw = cw_ref[...]
    cb = cb_ref[...]
    halo = 2 * SUBLANES
    interleave = tm // FF_ROWS >= 4
    nrb = tm // FF_ROWS if interleave else 1
    blk_rows = tm // nrb

    def up_piece(j, r):
        lo = j * FF_CHUNK
        hr = h[r * blk_rows:(r + 1) * blk_rows + (halo if r == nrb - 1 else 0), :]
        return _dot(hr, wup_ref[:, lo:lo + FF_CHUNK]), _dot(hr, wup_ref[:, D_FF + lo:D_FF + lo + FF_CHUNK])

    def conv(pieces, half, r, lo):
        if r == nrb - 1:
            u = pieces[r][half]
        else:
            u = jnp.concatenate([pieces[r][half], pieces[r + 1][half][:halo, :]], axis=0)
        w = cw[:, lo:lo + FF_CHUNK]
        out = (pltpu.roll(u, 1, 0) * w[0:1, :] + u * w[1:2, :] + pltpu.roll(u, blk_rows + halo - 1, 0) * w[2:3, :]
               + cb[:, lo:lo + FF_CHUNK])
        return out[SUBLANES:SUBLANES + blk_rows, :]

    nchunks = D_FF // FF_CHUNK
    acc = [jnp.zeros((blk_rows, D_MODEL), F32) for _ in range(nrb)]
    cur = [up_piece(0, r) for r in range(nrb)]
    for j in range(nchunks):
        lo = j * FF_CHUNK
        more = j + 1 < nchunks
        nxt = [up_piece(j + 1, r) for r in range(nrb)] if (more and not interleave) else []
        for r in range(nrb):
            val = conv(cur, 0, r, lo)
            gate = conv(cur, 1, r, D_FF + lo)
            act = (gate * _sigmoid(gate) * val).astype(BF16)
            acc[r] = acc[r] + _dot(act, wdn_ref[lo:lo + FF_CHUNK, :])
            if more and interleave:
                nxt.append(up_piece(j + 1, r))
        cur = nxt
    x2 = x + m[5:6, :] * jnp.concatenate(acc, axis=0)
    ms = jnp.mean(x2 * x2, axis=-1, keepdims=True)
    o_ref[0] = x2 * lax.rsqrt(ms + NORM_EPS) * gf_ref[...]


def _ffn(x, mod, mod_row, p):
    b, t, _ = x.shape
    tm = min(t, 512)
    hb = tm // SUBLANES
    last_hb = t // SUBLANES - 1
    in_specs = [pl.BlockSpec((1, tm, D_MODEL), lambda i, j: (i, j, 0)),
                pl.BlockSpec((1, SUBLANES, D_MODEL), lambda i, j: (i, jnp.maximum(j * hb - 1, 0), 0)),
                pl.BlockSpec((1, SUBLANES, D_MODEL), lambda i, j: (i, jnp.minimum((j + 1) * hb, last_hb), 0)),
                pl.BlockSpec((1, 6, D_MODEL), lambda i, j: (mod_row(i), 0, 0)),
                _const_spec((1, D_MODEL)),
                _const_spec((D_MODEL, 2 * D_FF)), _const_spec((3, 2 * D_FF)), _const_spec((1, 2 * D_FF)),
                _const_spec((D_FF, D_MODEL)), _const_spec((1, D_MODEL))]
    return pl.pallas_call(
        _ffn_kernel,
        grid=(b, t // tm),
        in_specs=in_specs,
        out_specs=pl.BlockSpec((1, tm, D_MODEL), lambda i, j: (i, j, 0)),
        out_shape=jax.ShapeDtypeStruct((b, t, D_MODEL), F32),
        compiler_params=_params("parallel", "parallel"),
        name="ffn",
    )(x, x, x, mod, p["g_norm2"], p["w_ffn_up"], p["conv_w"], p["conv_b"], p["w_ffn_down"], p["g_final"])


def _rope_tables(t):
    rows = t // GRID_W
    row = jnp.repeat(jnp.arange(rows), GRID_W).astype(F32)
    col = jnp.tile(jnp.arange(GRID_W), rows).astype(F32)
    freqs = ROPE_BASE ** (-jnp.arange(ROPE_PAIRS, dtype=F32) / ROPE_PAIRS)
    ar = row[:, None] * freqs
    ac = col[:, None] * freqs
    cos = jnp.concatenate([jnp.cos(ar), jnp.cos(ar), jnp.cos(ac), jnp.cos(ac)], axis=1)
    sin = jnp.concatenate([-jnp.sin(ar), jnp.sin(ar), -jnp.sin(ac), jnp.sin(ac)], axis=1)
    return jnp.tile(cos, (1, 2)), jnp.tile(sin, (1, 2))


def _block_diag2(w):
    z = jnp.zeros_like(w[0])
    return jnp.concatenate([jnp.concatenate([w[0], z], axis=1), jnp.concatenate([z, w[1]], axis=1)], axis=0)


def _trunk(x, mod, mod_row_tile, mod_row_seq, p, ctx):
    b, t, _ = x.shape
    rope = _rope_tables(t) if ctx is not None else None
    kv_dtype = BF16 if ctx is not None else F32
    q, k, v, ga, gb, zr = _in_proj(x.reshape(b * t, D_MODEL), mod, p["g_norm1"], p["w_in"], t,
                                   mod_row_tile, rope, kv_dtype)
    seq = lambda arr: arr.reshape(b, t, arr.shape[-1])
    q, k, v, ga, gb, zr = seq(q), seq(k), seq(v), seq(ga), seq(gb), seq(zr)
    if ctx is None:
        attn = _attn_ctx(q, k, v, p["sink"])
        s0 = None
    else:
        k_ctx, v_ctx, s0 = ctx
        attn = _attn_lat(q, k, v, k_ctx, v_ctx, p["sink"])
    lw, kd, bb, a, r, vr, g, bonus = _rw_prep(zr, p)
    y0, y1, s_final = _rw_scan(s0, lw, kd, bb, a, r, vr)
    x1 = _post(x, mod, mod_row_seq, attn, y0, y1, bonus, g, ga, gb, p)
    out = _ffn(x1, mod, mod_row_seq, p)
    return out, k, v, s_final


def kernel(x_prompt, x_sample, c, cache_k, cache_v, state_rwkv, c_ctx, w_ada, b_ada, g_norm1, w_in, attn_sink, w_proj_a, w_proj_b, rwkv_mu, rwkv_w0, rwkv_w2, rwkv_a0, rwkv_a2, rwkv_k_k, rwkv_k_a, rwkv_r_k, rwkv_g2, rwkv_ln_g, rwkv_ln_b, w_out, g_norm2, w_ffn_up, ffn_conv_w, ffn_conv_b, w_ffn_down, g_final):
    depth = w_ada.shape[0]
    assert depth == 1, "single trunk layer"
    l = 0
    nb, seq, _ = x_prompt.shape
    db, dseq, _ = x_sample.shape
    past = cache_k.shape[2]

    head_id = jnp.arange(RWKV_WIDTH) // RWKV_HEAD_SIZE
    p = dict(
        g_norm1=g_norm1[l].reshape(1, D_MODEL),
        w_in=w_in[l].astype(BF16),
        sink=attn_sink[l],
        mu=rwkv_mu[l],
        w2=_block_diag2(rwkv_w2[l]).astype(BF16),
        w0=rwkv_w0[l].reshape(1, N_DIR * RWKV_WIDTH),
        a2=_block_diag2(rwkv_a2[l]).astype(BF16),
        a0=rwkv_a0[l].reshape(1, N_DIR * RWKV_WIDTH),
        g2=rwkv_g2[l].astype(BF16),
        k_k=rwkv_k_k[l].reshape(1, RWKV_WIDTH),
        k_a=rwkv_k_a[l].reshape(1, RWKV_WIDTH),
        r_k=rwkv_r_k[l].reshape(1, RWKV_WIDTH),
        bd=(head_id[:, None] == head_id[None, :]).astype(BF16),
        ln_g=rwkv_ln_g[l].reshape(1, RWKV_WIDTH),
        ln_b=rwkv_ln_b[l].reshape(1, RWKV_WIDTH),
        w_proj_a=w_proj_a[l].astype(BF16),
        w_proj_b=w_proj_b[l].astype(BF16),
        w_out=w_out[l].astype(BF16),
        g_norm2=g_norm2[l].reshape(1, D_MODEL),
        w_ffn_up=w_ffn_up[l].astype(BF16),
        conv_w=ffn_conv_w[l],
        conv_b=ffn_conv_b[l].reshape(1, 2 * D_FF),
        w_ffn_down=w_ffn_down[l].astype(BF16),
        g_final=g_final.reshape(1, D_MODEL),
    )

    ctx_row = db
    mod_rows = 2 * SUBLANES
    cond = jnp.zeros((mod_rows, D_MODEL), F32).at[:db].set(c).at[ctx_row].set(c_ctx)
    mod = _modulation(cond, w_ada[l], b_ada[l]).reshape(mod_rows, 6, D_MODEL)

    y_prompt, kc, vc, sc = _trunk(x_prompt, mod, lambda i: ctx_row, lambda i: ctx_row, p, None)

    lat_tiles = max(dseq // 512, 1)
    ctx = (cache_k[:, l].reshape(db, past, KV_WIDTH), cache_v[:, l].reshape(db, past, KV_WIDTH), state_rwkv[:, l])
    y_sample, _, _, _ = _trunk(x_sample, mod, lambda i: i // lat_tiles, lambda i: i, p, ctx)

    new_cache_k = kc.reshape(nb, 1, seq, N_KV_HEADS, HEAD_DIM)
    new_cache_v = vc.reshape(nb, 1, seq, N_KV_HEADS, HEAD_DIM)
    new_state = sc.reshape(nb, 1, N_DIR, RWKV_HEADS, RWKV_HEAD_SIZE, RWKV_HEAD_SIZE)
    return (y_prompt, y_sample, new_cache_k, new_cache_v, new_state)
```

```python
import functools

import jax
import jax.numpy as jnp
from jax import lax
from jax.experimental import pallas as pl
from jax.experimental.pallas import tpu as pltpu

F32 = jnp.float32
BF16 = jnp.bfloat16

D_MODEL = 1024
GRID_W = 64
HEAD_DIM = 64
N_Q_HEADS = 8
N_KV_HEADS = 2
GQA_GROUP = N_Q_HEADS // N_KV_HEADS
ATTN_WIDTH = N_Q_HEADS * HEAD_DIM
KV_WIDTH = N_KV_HEADS * HEAD_DIM
WINDOW = 128
BLOCK = 128
ROPE_BASE = 10000.0
ROPE_PAIRS = HEAD_DIM // 4
RWKV_HEADS = 8
RWKV_HEAD_SIZE = 64
RWKV_WIDTH = RWKV_HEADS * RWKV_HEAD_SIZE
N_DIR = 2
DECAY_RANK = 64
ICLR_RANK = 64
GATE_RANK = 128
RW_SHIFT_COLS = 3 * RWKV_WIDTH + N_DIR * DECAY_RANK + N_DIR * ICLR_RANK + GATE_RANK
D_FF = 2816
NORM_EPS = 1e-6
GN_EPS = 64e-5
MASK_VALUE = -1e30

Q_OFF = 0
K_OFF = ATTN_WIDTH
V_OFF = K_OFF + KV_WIDTH
GA_OFF = V_OFF + KV_WIDTH
GB_OFF = GA_OFF + D_MODEL
ZR_OFF = GB_OFF + D_MODEL
W_IN_COLS = ZR_OFF + RW_SHIFT_COLS

ZW_OFF = 3 * RWKV_WIDTH
ZA_OFF = ZW_OFF + N_DIR * DECAY_RANK
ZG_OFF = ZA_OFF + N_DIR * ICLR_RANK

SCAN_CHUNK = 64
NEUMANN_STEPS = 6
SUBLANES = 8
BF16_ROWS = 16
ATTN_UNIT = 2
ATTN_QBLOCKS = 4
FF_CHUNK = 256
FF_ROWS = 128
FF_TILE = 512
VMEM_LIMIT = 56 * 1024 * 1024

_NT = (((1,), (1,)), ((), ()))
_TN = (((0,), (0,)), ((), ()))


def _params(*sem):
    return pltpu.CompilerParams(dimension_semantics=sem, vmem_limit_bytes=VMEM_LIMIT)


def _const_spec(shape):
    nd = len(shape)
    return pl.BlockSpec(shape, lambda *_: (0,) * nd, pipeline_mode=pl.Buffered(1))


def _dot(a, b):
    return jnp.dot(a, b, preferred_element_type=F32)


def _split_dot(x, m):
    hi = x.astype(BF16)
    lo = (x - hi.astype(F32)).astype(BF16)
    return _dot(hi, m) + _dot(lo, m)


def _sigmoid(x):
    return 1.0 / (1.0 + jnp.exp(-x))


def _mod_kernel(c_ref, w_ref, b_ref, o_ref):
    c = c_ref[...]
    s = (c * _sigmoid(c)).astype(BF16)
    o_ref[...] = _dot(s, w_ref[...].astype(BF16)) + b_ref[...]


def _modulation(cond, w_ada, b_ada):
    rows = cond.shape[0]
    n = w_ada.shape[1]
    tn = D_MODEL
    return pl.pallas_call(
        _mod_kernel,
        grid=(n // tn,),
        in_specs=[pl.BlockSpec((rows, D_MODEL), lambda j: (0, 0)),
                  pl.BlockSpec((D_MODEL, tn), lambda j: (0, j)),
                  pl.BlockSpec((1, tn), lambda j: (0, j))],
        out_specs=pl.BlockSpec((rows, tn), lambda j: (0, j)),
        out_shape=jax.ShapeDtypeStruct((rows, n), F32),
        compiler_params=_params("parallel"),
        name="modulation",
    )(cond, w_ada, b_ada.reshape(1, n))


def _swap16(x):
    w = x.shape[1]
    lane = lax.broadcasted_iota(jnp.int32, x.shape, 1)
    first = (lane % 32) < 16
    return jnp.where(first, pltpu.roll(x, w - 16, 1), pltpu.roll(x, 16, 1))


def _in_proj_kernel(*refs, rope):
    if rope:
        x_ref, mod_ref, g_ref, w_ref, cos_ref, sin_ref, q_ref, k_ref, v_ref, ga_ref, gb_ref, zr_ref = refs
    else:
        x_ref, mod_ref, g_ref, w_ref, q_ref, k_ref, v_ref, ga_ref, gb_ref, zr_ref = refs
    x = x_ref[...]
    m = mod_ref[0]
    ms = jnp.mean(x * x, axis=-1, keepdims=True)
    h = x * lax.rsqrt(ms + NORM_EPS) * g_ref[...]
    h = (h * (1.0 + m[1:2, :]) + m[0:1, :]).astype(BF16)

    q = _dot(h, w_ref[:, Q_OFF:K_OFF])
    k = _dot(h, w_ref[:, K_OFF:V_OFF])
    v = _dot(h, w_ref[:, V_OFF:GA_OFF])
    if rope:
        cos = cos_ref[...]
        sin = sin_ref[...]
        reps = ATTN_WIDTH // cos.shape[1]
        cos_q = jnp.concatenate([cos] * reps, axis=1)
        sin_q = jnp.concatenate([sin] * reps, axis=1)
        q = q * cos_q + _swap16(q) * sin_q
        k = k * cos + _swap16(k) * sin
    q_ref[...] = (q * (HEAD_DIM ** -0.5)).astype(q_ref.dtype)
    k_ref[...] = k.astype(k_ref.dtype)
    v_ref[...] = v.astype(v_ref.dtype)
    ga_ref[...] = _sigmoid(_dot(h, w_ref[:, GA_OFF:GB_OFF])).astype(ga_ref.dtype)
    gb_ref[...] = _sigmoid(_dot(h, w_ref[:, GB_OFF:ZR_OFF])).astype(gb_ref.dtype)
    zr_ref[...] = _dot(h, w_ref[:, ZR_OFF:W_IN_COLS]).astype(zr_ref.dtype)


def _in_proj(x, mod, g1, w_in, seq_len, mod_row, rope_tabs, kv_dtype):
    ntok = x.shape[0]
    tm = min(512, seq_len) if rope_tabs is not None else 512
    tiles_per_seq = max(seq_len // tm, 1)
    rope = rope_tabs is not None
    tok = lambda width: pl.BlockSpec((tm, width), lambda i: (i, 0))
    in_specs = [tok(D_MODEL),
                pl.BlockSpec((1, 6, D_MODEL), lambda i: (mod_row(i), 0, 0)),
                _const_spec((1, D_MODEL)),
                _const_spec((D_MODEL, W_IN_COLS))]
    args = [x, mod, g1, w_in]
    if rope:
        in_specs += [pl.BlockSpec((tm, 2 * HEAD_DIM), lambda i: (i % tiles_per_seq, 0))] * 2
        args += list(rope_tabs)
    out_shape = [jax.ShapeDtypeStruct((ntok, ATTN_WIDTH), BF16),
                 jax.ShapeDtypeStruct((ntok, KV_WIDTH), kv_dtype),
                 jax.ShapeDtypeStruct((ntok, KV_WIDTH), kv_dtype),
                 jax.ShapeDtypeStruct((ntok, D_MODEL), BF16),
                 jax.ShapeDtypeStruct((ntok, D_MODEL), BF16),
                 jax.ShapeDtypeStruct((ntok, RW_SHIFT_COLS), BF16)]
    out_specs = [tok(ATTN_WIDTH), tok(KV_WIDTH), tok(KV_WIDTH), tok(D_MODEL), tok(D_MODEL), tok(RW_SHIFT_COLS)]
    return pl.pallas_call(
        functools.partial(_in_proj_kernel, rope=rope),
        grid=(ntok // tm,),
        in_specs=in_specs, out_specs=out_specs, out_shape=out_shape,
        compiler_params=_params("parallel"),
        name="in_proj_rope" if rope else "in_proj",
    )(*args)


def _softmax_pv(s, sink_col, v):
    m = jnp.maximum(jnp.max(s, axis=-1, keepdims=True), sink_col)
    p = jnp.exp(s - m)
    denom = jnp.sum(p, axis=-1, keepdims=True) + jnp.exp(sink_col - m)
    return _dot(p.astype(BF16), v) / denom


def _attend(problems, sink_ref):
    units = [list(range(u, u + ATTN_UNIT)) for u in range(0, N_Q_HEADS, ATTN_UNIT)]
    kv_of = lambda heads: slice((heads[0] // GQA_GROUP) * HEAD_DIM, (heads[0] // GQA_GROUP + 1) * HEAD_DIM)
    scores = []
    for q, k, _, mask in problems:
        for heads in units:
            qu = jnp.concatenate([q[:, h * HEAD_DIM:(h + 1) * HEAD_DIM] for h in heads], axis=0)
            scores.append(mask(lax.dot_general(qu, k[:, kv_of(heads)], _NT, preferred_element_type=F32)))
    results = []
    for pi, (q, _, v, _) in enumerate(problems):
        t = q.shape[0]
        outs = []
        for ui, heads in enumerate(units):
            sink_col = jnp.concatenate([jnp.full((t, 1), sink_ref[h], F32) for h in heads], axis=0)
            o = _softmax_pv(scores[pi * len(units) + ui], sink_col, v[:, kv_of(heads)])
            outs += [o[i * t:(i + 1) * t, :] for i in range(ATTN_UNIT)]
        results.append(jnp.concatenate(outs, axis=1))
    return results


def _attn_ctx_kernel(sink_ref, q_ref, k_ref, v_ref, o_ref):
    out, = _attend([(q_ref[0], k_ref[0].astype(BF16), v_ref[0].astype(BF16), lambda s: s)], sink_ref)
    o_ref[0] = out.astype(o_ref.dtype)


def _attn_ctx(q, k, v, sink):
    b, t, _ = q.shape
    seq = lambda width: pl.BlockSpec((1, t, width), lambda i: (i, 0, 0))
    return pl.pallas_call(
        _attn_ctx_kernel,
        grid=(b,),
        in_specs=[pl.BlockSpec(memory_space=pltpu.SMEM), seq(ATTN_WIDTH), seq(KV_WIDTH), seq(KV_WIDTH)],
        out_specs=seq(ATTN_WIDTH),
        out_shape=jax.ShapeDtypeStruct((b, t, ATTN_WIDTH), BF16),
        compiler_params=_params("parallel"),
        name="attn_ctx",
    )(sink, q, k, v)


def _attn_lat_kernel(sink_ref, q_ref, kp_ref, kc_ref, kn_ref, vp_ref, vc_ref, vn_ref, kx_ref, vx_ref, o_ref):
    m = pl.program_id(1)
    nb = ATTN_QBLOCKS * pl.num_programs(1)
    kc, vc = kc_ref[0], vc_ref[0]
    kblk = [kp_ref[0]] + [kc[s * BLOCK:(s + 1) * BLOCK, :] for s in range(ATTN_QBLOCKS)] + [kn_ref[0]]
    vblk = [vp_ref[0]] + [vc[s * BLOCK:(s + 1) * BLOCK, :] for s in range(ATTN_QBLOCKS)] + [vn_ref[0]]
    kx = kx_ref[0].astype(BF16)
    vx = vx_ref[0].astype(BF16)
    rows = ATTN_UNIT * BLOCK
    rq = lax.broadcasted_iota(jnp.int32, (rows, BLOCK), 0) % BLOCK
    col = lax.broadcasted_iota(jnp.int32, (rows, BLOCK), 1)

    def make_mask(n):
        keep_prev = col >= rq + jnp.where(n > 0, 0, BLOCK)
        keep_next = col <= rq - jnp.where(n < nb - 1, 0, BLOCK)

        def mask(s):
            return jnp.concatenate(
                [jnp.where(keep_prev, s[:, :BLOCK], MASK_VALUE), s[:, BLOCK:2 * BLOCK],
                 jnp.where(keep_next, s[:, 2 * BLOCK:3 * BLOCK], MASK_VALUE), s[:, 3 * BLOCK:]], axis=1)
        return mask

    problems = []
    for s in range(ATTN_QBLOCKS):
        k = jnp.concatenate(kblk[s:s + 3] + [kx], axis=0)
        v = jnp.concatenate(vblk[s:s + 3] + [vx], axis=0)
        problems.append((q_ref[0, s * BLOCK:(s + 1) * BLOCK, :], k, v, make_mask(m * ATTN_QBLOCKS + s)))
    outs = _attend(problems, sink_ref)
    o_ref[0] = jnp.concatenate(outs, axis=0).astype(o_ref.dtype)


def _attn_lat(q, k, v, k_ctx, v_ctx, sink):
    assert WINDOW == BLOCK, "the block-triangular band masks assume one block of reach on either side"
    b, t, _ = q.shape
    nb = t // BLOCK
    p = k_ctx.shape[1]
    qb = ATTN_QBLOCKS
    blk = lambda width, f: pl.BlockSpec((1, BLOCK, width), lambda i, m: (i, f(m), 0))
    cur = lambda width: pl.BlockSpec((1, qb * BLOCK, width), lambda i, m: (i, m, 0))
    prev = lambda m: jnp.maximum(m * qb - 1, 0)
    nxt = lambda m: jnp.minimum((m + 1) * qb, nb - 1)
    ctx = pl.BlockSpec((1, p, KV_WIDTH), lambda i, m: (i, 0, 0))
    return pl.pallas_call(
        _attn_lat_kernel,
        grid=(b, nb // qb),
        in_specs=[pl.BlockSpec(memory_space=pltpu.SMEM), cur(ATTN_WIDTH),
                  blk(KV_WIDTH, prev), cur(KV_WIDTH), blk(KV_WIDTH, nxt),
                  blk(KV_WIDTH, prev), cur(KV_WIDTH), blk(KV_WIDTH, nxt), ctx, ctx],
        out_specs=cur(ATTN_WIDTH),
        out_shape=jax.ShapeDtypeStruct((b, t, ATTN_WIDTH), BF16),
        compiler_params=_params("parallel", "parallel"),
        name="attn_lat",
    )(sink, q, k, k, k, v, v, v, k_ctx, v_ctx)


def _rw_prep_kernel(z_ref, zp_ref, zn_ref, mu_ref, w2_ref, w0_ref, a2_ref, a0_ref, g2_ref, kk_ref, ka_ref,
                    rk_ref, bd_ref, lw_ref, kd_ref, bb_ref, a_ref, r_ref, v_ref, g_ref, bonus_ref):
    i = pl.program_id(1)
    nt = pl.num_programs(1)
    z = z_ref[0].astype(F32)
    tm = z.shape[0]
    row = lax.broadcasted_iota(jnp.int32, z.shape, 0)
    halo_prev = jnp.where(i > 0, zp_ref[0].astype(F32)[BF16_ROWS - 1:BF16_ROWS, :], 0.0)
    halo_next = jnp.where(i < nt - 1, zn_ref[0].astype(F32)[0:1, :], 0.0)
    prev = jnp.where(row == 0, halo_prev, pltpu.roll(z, 1, 0))
    nxt = jnp.where(row == tm - 1, halo_next, pltpu.roll(z, tm - 1, 0))
    mu = mu_ref[...]
    z = z + mu[0:1, :] * (prev - z) + mu[1:2, :] * (nxt - z)

    r = z[:, 0:RWKV_WIDTH]
    k = z[:, RWKV_WIDTH:2 * RWKV_WIDTH]
    v = z[:, 2 * RWKV_WIDTH:3 * RWKV_WIDTH]
    zw = z[:, ZW_OFF:ZA_OFF]
    za = z[:, ZA_OFF:ZG_OFF]
    zg = z[:, ZG_OFF:RW_SHIFT_COLS]

    logit = w0_ref[...] + _dot(jnp.tanh(zw).astype(BF16), w2_ref[...])
    lw = -jnp.exp(F32(-0.5)) * _sigmoid(logit)
    a = _sigmoid(a0_ref[...] + _dot(za.astype(BF16), a2_ref[...]))
    g_ref[0] = _dot(_sigmoid(zg).astype(BF16), g2_ref[...]).astype(g_ref.dtype)

    bd = bd_ref[...]
    kk = k * kk_ref[...]
    kk = kk * lax.rsqrt(jnp.maximum(_split_dot(kk * kk, bd), 1e-24))
    ka = ka_ref[...]
    ksum = jnp.zeros_like(k)
    for d in range(N_DIR):
        ad = a[:, d * RWKV_WIDTH:(d + 1) * RWKV_WIDTH]
        kd = k * (1.0 + (ad - 1.0) * ka)
        lw_ref[d, 0] = lw[:, d * RWKV_WIDTH:(d + 1) * RWKV_WIDTH]
        kd_ref[d, 0] = kd.astype(kd_ref.dtype)
        bb_ref[d, 0] = (ad * kk).astype(bb_ref.dtype)
        ksum = ksum + kd
    a_ref[0] = (-kk).astype(a_ref.dtype)
    r_ref[0] = r.astype(r_ref.dtype)
    v_ref[0] = v.astype(v_ref.dtype)
    bonus_ref[0] = (_split_dot(r * ksum * rk_ref[...], bd) * v).astype(bonus_ref.dtype)


def _rw_prep(zr, p):
    b, t, _ = zr.shape
    tm = 256
    nt = t // tm
    hb = tm // BF16_ROWS
    last_hb = t // BF16_ROWS - 1
    tok = lambda: pl.BlockSpec((1, tm, RWKV_WIDTH), lambda i, j: (i, j, 0))
    tok2 = lambda: pl.BlockSpec((N_DIR, 1, tm, RWKV_WIDTH), lambda i, j: (0, i, j, 0))
    two = jax.ShapeDtypeStruct((N_DIR, b, t, RWKV_WIDTH), F32)
    one_bf = jax.ShapeDtypeStruct((b, t, RWKV_WIDTH), BF16)
    two_bf = jax.ShapeDtypeStruct((N_DIR, b, t, RWKV_WIDTH), BF16)
    in_specs = [pl.BlockSpec((1, tm, RW_SHIFT_COLS), lambda i, j: (i, j, 0)),
                pl.BlockSpec((1, BF16_ROWS, RW_SHIFT_COLS), lambda i, j: (i, jnp.maximum(j * hb - 1, 0), 0)),
                pl.BlockSpec((1, BF16_ROWS, RW_SHIFT_COLS), lambda i, j: (i, jnp.minimum((j + 1) * hb, last_hb), 0)),
                _const_spec((2, RW_SHIFT_COLS)),
                _const_spec((N_DIR * DECAY_RANK, N_DIR * RWKV_WIDTH)), _const_spec((1, N_DIR * RWKV_WIDTH)),
                _const_spec((N_DIR * ICLR_RANK, N_DIR * RWKV_WIDTH)), _const_spec((1, N_DIR * RWKV_WIDTH)),
                _const_spec((GATE_RANK, RWKV_WIDTH)),
                _const_spec((1, RWKV_WIDTH)), _const_spec((1, RWKV_WIDTH)), _const_spec((1, RWKV_WIDTH)),
                _const_spec((RWKV_WIDTH, RWKV_WIDTH))]
    return pl.pallas_call(
        _rw_prep_kernel,
        grid=(b, nt),
        in_specs=in_specs,
        out_specs=[tok2(), tok2(), tok2(), tok(), tok(), tok(), tok(), tok()],
        out_shape=[two, two_bf, two_bf, one_bf, one_bf, one_bf, one_bf, one_bf],
        compiler_params=_params("parallel", "parallel"),
        name="rw_prep",
    )(zr, zr, zr, p["mu"], p["w2"], p["w0"], p["a2"], p["a0"], p["g2"], p["k_k"], p["k_a"], p["r_k"], p["bd"])


def _rw_scan_kernel(*refs, has_s0, nchunks):
    if has_s0:
        s0_ref, refs = refs[0], refs[1:]
    (lw0_ref, kd0_ref, bb0_ref, a0_ref, r0_ref, v0_ref, lw1_ref, kd1_ref, bb1_ref, a1_ref, r1_ref, v1_ref,
     y0_ref, y1_ref, sf_ref, s_scr) = refs
    dir_refs = ((lw0_ref, kd0_ref, bb0_ref, a0_ref, r0_ref, v0_ref, y0_ref),
                (lw1_ref, kd1_ref, bb1_ref, a1_ref, r1_ref, v1_ref, y1_ref))
    i = pl.program_id(1)
    c = SCAN_CHUNK
    n = RWKV_HEAD_SIZE

    @pl.when(i == 0)
    def _():
        if has_s0:
            s_scr[...] = s0_ref[0]
        else:
            s_scr[...] = jnp.zeros_like(s_scr)

    diff = lax.broadcasted_iota(jnp.int32, (c, c), 0) - lax.broadcasted_iota(jnp.int32, (c, c), 1)
    incl = (diff >= 0, diff <= 0)
    strict = (diff > 0, diff < 0)
    diff2 = (lax.broadcasted_iota(jnp.int32, (c, 2 * c), 0) - lax.broadcasted_iota(jnp.int32, (c, 2 * c), 1) % c)
    incl2 = (diff2 >= 0, diff2 <= 0)
    chains = [(d, h) for d in range(N_DIR) for h in range(RWKV_HEADS)]
    nch = len(chains)
    dirs = [d for d, _ in chains]
    each = lambda f: [f(d, slice(h * n, (h + 1) * n)) for d, h in chains]

    def chunk(jj, carry):
        pre = []
        for d in range(N_DIR):
            lw_ref, kd_ref, bb_ref, a_ref, r_ref, v_ref, _ = dir_refs[d]
            j = jj if d == 0 else nchunks - 1 - jj
            rows = pl.ds(pl.multiple_of(j * c, c), c)
            lw = lw_ref[0, 0, rows, :]
            kd = kd_ref[0, 0, rows, :].astype(F32)
            bb = bb_ref[0, 0, rows, :].astype(F32)
            l1 = lw.astype(BF16)
            rem = lw - l1.astype(F32)
            l2 = rem.astype(BF16)
            l3 = (rem - l2.astype(F32)).astype(BF16)
            tri = incl[d].astype(BF16)
            cum = _dot(jnp.concatenate([tri, tri, tri], axis=1), jnp.concatenate([l1, l2, l3], axis=0))
            ctot = jnp.sum(lw, axis=0, keepdims=True)
            en = jnp.exp(-cum)
            eh = jnp.exp(ctot - cum)
            pre.append(dict(
                rows=rows,
                at=a_ref[0, rows, :].astype(F32) * jnp.exp(cum - lw),
                rt=r_ref[0, rows, :].astype(F32) * jnp.exp(cum),
                bt=(bb * en).astype(BF16), kt=(kd * en).astype(BF16),
                bh=(bb * eh).astype(BF16), kh=(kd * eh).astype(BF16),
                gam=jnp.exp(ctot), v=v_ref[0, rows, :]))

        at = each(lambda d, hs: pre[d]["at"][:, hs])
        rt = each(lambda d, hs: pre[d]["rt"][:, hs])
        vh = each(lambda d, hs: pre[d]["v"][:, hs])
        bh = each(lambda d, hs: pre[d]["bh"][:, hs])
        kh = each(lambda d, hs: pre[d]["kh"][:, hs])
        gam = each(lambda d, hs: pre[d]["gam"][:, hs])
        amat = each(lambda d, hs: lax.dot_general(
            jnp.concatenate([pre[d]["at"][:, hs], pre[d]["rt"][:, hs]], axis=0).astype(BF16),
            jnp.concatenate([pre[d]["bt"][:, hs], pre[d]["kt"][:, hs]], axis=0), _NT, preferred_element_type=F32))
        a_ak = [jnp.where(strict[dirs[q]], amat[q][:c, c:], 0.0).astype(BF16) for q in range(nch)]
        a_r = [jnp.where(incl2[dirs[q]], amat[q][c:, :], 0.0).astype(BF16) for q in range(nch)]
        x = [jnp.where(strict[dirs[q]], amat[q][:c, :c], 0.0) for q in range(nch)]
        z = [jnp.concatenate([at[q], _dot(a_ak[q], vh[q])], axis=1) for q in range(nch)]
        for step in range(NEUMANN_STEPS):
            xb = [x[q].astype(BF16) for q in range(nch)]
            if step < NEUMANN_STEPS - 1:
                prod = [_dot(xb[q], jnp.concatenate([z[q].astype(BF16), xb[q]], axis=1)) for q in range(nch)]
                z = [z[q] + prod[q][:, :2 * n] for q in range(nch)]
                x = [prod[q][:, 2 * n:] for q in range(nch)]
            else:
                z = [z[q] + _dot(xb[q], z[q].astype(BF16)) for q in range(nch)]
        zero = jnp.zeros((c, n), BF16)
        rmat = [jnp.concatenate([z[q].astype(BF16), jnp.concatenate([zero, vh[q]], axis=1)], axis=0)
                for q in range(nch)]
        qy = [_dot(a_r[q], rmat[q]) for q in range(nch)]
        qm = [(rt[q] + qy[q][:, :n]).astype(BF16) for q in range(nch)]
        mn = [lax.dot_general(rmat[q], jnp.concatenate([bh[q], kh[q]], axis=0), _TN, preferred_element_type=F32)
              for q in range(nch)]
        s = [s_scr[d, h] for d, h in chains]
        sb = [s[q].astype(BF16) for q in range(nch)]
        ys = [lax.dot_general(qm[q], sb[q], _NT, preferred_element_type=F32) + qy[q][:, n:] for q in range(nch)]
        for q, (d, h) in enumerate(chains):
            s_scr[d, h] = s[q] * gam[q] + _dot(sb[q], mn[q][:n, :].astype(BF16)) + mn[q][n:, :]
        for d in range(N_DIR):
            y_ref = dir_refs[d][-1]
            y_ref[0, pre[d]["rows"], :] = jnp.concatenate(ys[d * RWKV_HEADS:(d + 1) * RWKV_HEADS], axis=1)
        return carry

    lax.fori_loop(0, nchunks, chunk, 0, unroll=2)

    @pl.when(i == pl.num_programs(1) - 1)
    def _():
        sf_ref[0] = s_scr[...]


def _rw_scan(s0, lw, kd, bb, a, r, v):
    _, b, t, _ = lw.shape
    tb = min(t, 512)
    nblk = t // tb
    blk = (lambda i: i, lambda i: nblk - 1 - i)
    dir_spec = lambda d: pl.BlockSpec((1, 1, tb, RWKV_WIDTH), lambda bi, i: (d, bi, blk[d](i), 0))
    tok_spec = lambda d: pl.BlockSpec((1, tb, RWKV_WIDTH), lambda bi, i: (bi, blk[d](i), 0))
    st_spec = pl.BlockSpec((1, N_DIR, RWKV_HEADS, RWKV_HEAD_SIZE, RWKV_HEAD_SIZE), lambda bi, i: (bi, 0, 0, 0, 0))
    has_s0 = s0 is not None
    in_specs, args = [], []
    if has_s0:
        in_specs.append(st_spec)
        args.append(s0)
    for d in range(N_DIR):
        in_specs += [dir_spec(d), dir_spec(d), dir_spec(d), tok_spec(d), tok_spec(d), tok_spec(d)]
        args += [lw, kd, bb, a, r, v]
    y_shape = jax.ShapeDtypeStruct((b, t, RWKV_WIDTH), F32)
    return pl.pallas_call(
        functools.partial(_rw_scan_kernel, has_s0=has_s0, nchunks=tb // SCAN_CHUNK),
        grid=(b, nblk),
        in_specs=in_specs,
        out_specs=[tok_spec(0), tok_spec(1), st_spec],
        out_shape=[y_shape, y_shape,
                   jax.ShapeDtypeStruct((b, N_DIR, RWKV_HEADS, RWKV_HEAD_SIZE, RWKV_HEAD_SIZE), F32)],
        scratch_shapes=[pltpu.VMEM((N_DIR, RWKV_HEADS, RWKV_HEAD_SIZE, RWKV_HEAD_SIZE), F32)],
        compiler_params=_params("parallel", "arbitrary"),
        name="rw_scan",
    )(*args)


def _post_kernel(x_ref, mod_ref, attn_ref, y0_ref, y1_ref, bonus_ref, g_ref, ga_ref, gb_ref, lng_ref, lnb_ref,
                 bd_ref, wa_ref, wb_ref, wo_ref, o_ref):
    m = mod_ref[0]
    y = y0_ref[0] + y1_ref[0]
    bd = bd_ref[...]
    inv_n = 1.0 / RWKV_HEAD_SIZE
    mean = _split_dot(y, bd) * inv_n
    yc = y - mean
    var = _split_dot(yc * yc, bd) * inv_n
    yn = yc * lax.rsqrt(var + GN_EPS) * lng_ref[...] + lnb_ref[...]
    rw = ((yn + bonus_ref[0].astype(F32)) * g_ref[0].astype(F32)).astype(BF16)
    merged = (ga_ref[0].astype(F32) * _dot(attn_ref[0], wa_ref[...])
              + gb_ref[0].astype(F32) * _dot(rw, wb_ref[...]))
    o_ref[0] = x_ref[0] + m[2:3, :] * _dot(merged.astype(BF16), wo_ref[...])


def _post(x, mod, mod_row, attn, y0, y1, bonus, g, ga, gb, p):
    b, t, _ = x.shape
    tm = min(t, 512)
    tok = lambda width: pl.BlockSpec((1, tm, width), lambda i, j: (i, j, 0))
    in_specs = [tok(D_MODEL),
                pl.BlockSpec((1, 6, D_MODEL), lambda i, j: (mod_row(i), 0, 0)),
                tok(ATTN_WIDTH),
                tok(RWKV_WIDTH), tok(RWKV_WIDTH), tok(RWKV_WIDTH), tok(RWKV_WIDTH), tok(D_MODEL), tok(D_MODEL),
                _const_spec((1, RWKV_WIDTH)), _const_spec((1, RWKV_WIDTH)),
                _const_spec((RWKV_WIDTH, RWKV_WIDTH)),
                _const_spec((ATTN_WIDTH, D_MODEL)), _const_spec((RWKV_WIDTH, D_MODEL)),
                _const_spec((D_MODEL, D_MODEL))]
    return pl.pallas_call(
        _post_kernel,
        grid=(b, t // tm),
        in_specs=in_specs,
        out_specs=tok(D_MODEL),
        out_shape=jax.ShapeDtypeStruct((b, t, D_MODEL), F32),
        compiler_params=_params("parallel", "parallel"),
        name="post",
    )(x, mod, attn, y0, y1, bonus, g, ga, gb, p["ln_g"], p["ln_b"], p["bd"], p["w_proj_a"], p["w_proj_b"], p["w_out"])


def _ffn_kernel(x_ref, xp_ref, xn_ref, mod_ref, g2_ref, wup_ref, cw_ref, cb_ref, wdn_ref, gf_ref, o_ref):
    i = pl.program_id(1)
    nt = pl.num_programs(1)
    m = mod_ref[0]
    g2 = g2_ref[...]

    def norm_mod(x):
        ms = jnp.mean(x * x, axis=-1, keepdims=True)
        return x * lax.rsqrt(ms + NORM_EPS) * g2 * (1.0 + m[4:5, :]) + m[3:4, :]

    x = x_ref[0]
    tm = x.shape[0]
    hp = jnp.where(i > 0, norm_mod(xp_ref[0]), 0.0)
    hn = jnp.where(i < nt - 1, norm_mod(xn_ref[0]), 0.0)
    h = jnp.concatenate([hp, norm_mod(x), hn], axis=0).astype(BF16)
    cw = cw_ref[...]
    cb = cb_ref[...]
    halo = 2 * SUBLANES
    interleave = tm // FF_ROWS >= 4
    nrb = tm // FF_ROWS if interleave else 1
    blk_rows = tm // nrb

    def up_piece(j, r):
        lo = j * FF_CHUNK
        hr = h[r * blk_rows:(r + 1) * blk_rows + (halo if r == nrb - 1 else 0), :]
        return _dot(hr, wup_ref[:, lo:lo + FF_CHUNK]), _dot(hr, wup_ref[:, D_FF + lo:D_FF + lo + FF_CHUNK])

    def conv(pieces, half, r, lo):
        if r == nrb - 1:
            u = pieces[r][half]
        else:
            u = jnp.concatenate([pieces[r][half], pieces[r + 1][half][:halo, :]], axis=0)
        w = cw[:, lo:lo + FF_CHUNK]
        out = (pltpu.roll(u, 1, 0) * w[0:1, :] + u * w[1:2, :] + pltpu.roll(u, blk_rows + halo - 1, 0) * w[2:3, :]
               + cb[:, lo:lo + FF_CHUNK])
        return out[SUBLANES:SUBLANES + blk_rows, :]

    nchunks = D_FF // FF_CHUNK
    acc = [jnp.zeros((blk_rows, D_MODEL), F32) for _ in range(nrb)]
    cur = [up_piece(0, r) for r in range(nrb)]
    for j in range(nchunks):
        lo = j * FF_CHUNK
        more = j + 1 < nchunks
        nxt = [up_piece(j + 1, r) for r in range(nrb)] if (more and not interleave) else []
        for r in range(nrb):
            val = conv(cur, 0, r, lo)
            gate = conv(cur, 1, r, D_FF + lo)
            act = (gate * _sigmoid(gate) * val).astype(BF16)
            acc[r] = acc[r] + _dot(act, wdn_ref[lo:lo + FF_CHUNK, :])
            if more and interleave:
                nxt.append(up_piece(j + 1, r))
        cur = nxt
    x2 = x + m[5:6, :] * jnp.concatenate(acc, axis=0)
    ms = jnp.mean(x2 * x2, axis=-1, keepdims=True)
    o_ref[0] = x2 * lax.rsqrt(ms + NORM_EPS) * gf_ref[...]


def _ffn(x, mod, mod_row, p):
    b, t, _ = x.shape
    tm = min(t, FF_TILE)
    hb = tm // SUBLANES
    last_hb = t // SUBLANES - 1
    in_specs = [pl.BlockSpec((1, tm, D_MODEL), lambda i, j: (i, j, 0)),
                pl.BlockSpec((1, SUBLANES, D_MODEL), lambda i, j: (i, jnp.maximum(j * hb - 1, 0), 0)),
                pl.BlockSpec((1, SUBLANES, D_MODEL), lambda i, j: (i, jnp.minimum((j + 1) * hb, last_hb), 0)),
                pl.BlockSpec((1, 6, D_MODEL), lambda i, j: (mod_row(i), 0, 0)),
                _const_spec((1, D_MODEL)),
                _const_spec((D_MODEL, 2 * D_FF)), _const_spec((3, 2 * D_FF)), _const_spec((1, 2 * D_FF)),
                _const_spec((D_FF, D_MODEL)), _const_spec((1, D_MODEL))]
    return pl.pallas_call(
        _ffn_kernel,
        grid=(b, t // tm),
        in_specs=in_specs,
        out_specs=pl.BlockSpec((1, tm, D_MODEL), lambda i, j: (i, j, 0)),
        out_shape=jax.ShapeDtypeStruct((b, t, D_MODEL), F32),
        compiler_params=_params("parallel", "parallel"),
        name="ffn",
    )(x, x, x, mod, p["g_norm2"], p["w_ffn_up"], p["conv_w"], p["conv_b"], p["w_ffn_down"], p["g_final"])


def _rope_tables(t):
    rows = t // GRID_W
    row = jnp.repeat(jnp.arange(rows), GRID_W).astype(F32)
    col = jnp.tile(jnp.arange(GRID_W), rows).astype(F32)
    freqs = ROPE_BASE ** (-jnp.arange(ROPE_PAIRS, dtype=F32) / ROPE_PAIRS)
    ar = row[:, None] * freqs
    ac = col[:, None] * freqs
    cos = jnp.concatenate([jnp.cos(ar), jnp.cos(ar), jnp.cos(ac), jnp.cos(ac)], axis=1)
    sin = jnp.concatenate([-jnp.sin(ar), jnp.sin(ar), -jnp.sin(ac), jnp.sin(ac)], axis=1)
    return jnp.tile(cos, (1, 2)), jnp.tile(sin, (1, 2))


def _block_diag2(w):
    z = jnp.zeros_like(w[0])
    return jnp.concatenate([jnp.concatenate([w[0], z], axis=1), jnp.concatenate([z, w[1]], axis=1)], axis=0)


def _trunk(x, mod, mod_row_tile, mod_row_seq, p, ctx):
    b, t, _ = x.shape
    rope = _rope_tables(t) if ctx is not None else None
    kv_dtype = BF16 if ctx is not None else F32
    q, k, v, ga, gb, zr = _in_proj(x.reshape(b * t, D_MODEL), mod, p["g_norm1"], p["w_in"], t,
                                   mod_row_tile, rope, kv_dtype)
    seq = lambda arr: arr.reshape(b, t, arr.shape[-1])
    q, k, v, ga, gb, zr = seq(q), seq(k), seq(v), seq(ga), seq(gb), seq(zr)
    if ctx is None:
        attn = _attn_ctx(q, k, v, p["sink"])
        s0 = None
    else:
        k_ctx, v_ctx, s0 = ctx
        attn = _attn_lat(q, k, v, k_ctx, v_ctx, p["sink"])
    lw, kd, bb, a, r, vr, g, bonus = _rw_prep(zr, p)
    y0, y1, s_final = _rw_scan(s0, lw, kd, bb, a, r, vr)
    x1 = _post(x, mod, mod_row_seq, attn, y0, y1, bonus, g, ga, gb, p)
    out = _ffn(x1, mod, mod_row_seq, p)
    return out, k, v, s_final


def kernel(x_prompt, x_sample, c, cache_k, cache_v, state_rwkv, c_ctx, w_ada, b_ada, g_norm1, w_in, attn_sink, w_proj_a, w_proj_b, rwkv_mu, rwkv_w0, rwkv_w2, rwkv_a0, rwkv_a2, rwkv_k_k, rwkv_k_a, rwkv_r_k, rwkv_g2, rwkv_ln_g, rwkv_ln_b, w_out, g_norm2, w_ffn_up, ffn_conv_w, ffn_conv_b, w_ffn_down, g_final):
    depth = w_ada.shape[0]
    assert depth == 1, "single trunk layer"
    l = 0
    nb, seq, _ = x_prompt.shape
    db, dseq, _ = x_sample.shape
    past = cache_k.shape[2]

    head_id = jnp.arange(RWKV_WIDTH) // RWKV_HEAD_SIZE
    p = dict(
        g_norm1=g_norm1[l].reshape(1, D_MODEL),
        w_in=w_in[l].astype(BF16),
        sink=attn_sink[l],
        mu=rwkv_mu[l],
        w2=_block_diag2(rwkv_w2[l]).astype(BF16),
        w0=rwkv_w0[l].reshape(1, N_DIR * RWKV_WIDTH),
        a2=_block_diag2(rwkv_a2[l]).astype(BF16),
        a0=rwkv_a0[l].reshape(1, N_DIR * RWKV_WIDTH),
        g2=rwkv_g2[l].astype(BF16),
        k_k=rwkv_k_k[l].reshape(1, RWKV_WIDTH),
        k_a=rwkv_k_a[l].reshape(1, RWKV_WIDTH),
        r_k=rwkv_r_k[l].reshape(1, RWKV_WIDTH),
        bd=(head_id[:, None] == head_id[None, :]).astype(BF16),
        ln_g=rwkv_ln_g[l].reshape(1, RWKV_WIDTH),
        ln_b=rwkv_ln_b[l].reshape(1, RWKV_WIDTH),
        w_proj_a=w_proj_a[l].astype(BF16),
        w_proj_b=w_proj_b[l].astype(BF16),
        w_out=w_out[l].astype(BF16),
        g_norm2=g_norm2[l].reshape(1, D_MODEL),
        w_ffn_up=w_ffn_up[l].astype(BF16),
        conv_w=ffn_conv_w[l],
        conv_b=ffn_conv_b[l].reshape(1, 2 * D_FF),
        w_ffn_down=w_ffn_down[l].astype(BF16),
        g_final=g_final.reshape(1, D_MODEL),
    )

    ctx_row = db
    mod_rows = 2 * SUBLANES
    cond = jnp.zeros((mod_rows, D_MODEL), F32).at[:db].set(c).at[ctx_row].set(c_ctx)
    mod = _modulation(cond, w_ada[l], b_ada[l]).reshape(mod_rows, 6, D_MODEL)

    y_prompt, kc, vc, sc = _trunk(x_prompt, mod, lambda i: ctx_row, lambda i: ctx_row, p, None)

    lat_tiles = max(dseq // 512, 1)
    ctx = (cache_k[:, l].reshape(db, past, KV_WIDTH), cache_v[:, l].reshape(db, past, KV_WIDTH), state_rwkv[:, l])
    y_sample, _, _, _ = _trunk(x_sample, mod, lambda i: i // lat_tiles, lambda i: i, p, ctx)

    new_cache_k = kc.reshape(nb, 1, seq, N_KV_HEADS, HEAD_DIM)
    new_cache_v = vc.reshape(nb, 1, seq, N_KV_HEADS, HEAD_DIM)
    new_state = sc.reshape(nb, 1, N_DIR, RWKV_HEADS, RWKV_HEAD_SIZE, RWKV_HEAD_SIZE)
    return (y_prompt, y_sample, new_cache_k, new_cache_v, new_state)
```

```python
import functools

import jax
import jax.numpy as jnp
from jax import lax
from jax.experimental import pallas as pl
from jax.experimental.pallas import tpu as pltpu

F32 = jnp.float32
BF16 = jnp.bfloat16

D_MODEL = 1024
GRID_W = 64
HEAD_DIM = 64
N_Q_HEADS = 8
N_KV_HEADS = 2
GQA_GROUP = N_Q_HEADS // N_KV_HEADS
ATTN_WIDTH = N_Q_HEADS * HEAD_DIM
KV_WIDTH = N_KV_HEADS * HEAD_DIM
WINDOW = 128
BLOCK = 128
ROPE_BASE = 10000.0
ROPE_PAIRS = HEAD_DIM // 4
RWKV_HEADS = 8
RWKV_HEAD_SIZE = 64
RWKV_WIDTH = RWKV_HEADS * RWKV_HEAD_SIZE
N_DIR = 2
DECAY_RANK = 64
ICLR_RANK = 64
GATE_RANK = 128
RW_SHIFT_COLS = 3 * RWKV_WIDTH + N_DIR * DECAY_RANK + N_DIR * ICLR_RANK + GATE_RANK
D_FF = 2816
NORM_EPS = 1e-6
GN_EPS = 64e-5
MASK_VALUE = -1e30

Q_OFF = 0
K_OFF = ATTN_WIDTH
V_OFF = K_OFF + KV_WIDTH
GA_OFF = V_OFF + KV_WIDTH
GB_OFF = GA_OFF + D_MODEL
ZR_OFF = GB_OFF + D_MODEL
W_IN_COLS = ZR_OFF + RW_SHIFT_COLS

ZW_OFF = 3 * RWKV_WIDTH
ZA_OFF = ZW_OFF + N_DIR * DECAY_RANK
ZG_OFF = ZA_OFF + N_DIR * ICLR_RANK

SCAN_CHUNK = 64
NEUMANN_STEPS = 6
SUBLANES = 8
BF16_ROWS = 16
ATTN_UNIT = 2
ATTN_QBLOCKS = 4
FF_CHUNK = 256
FF_ROWS = 128
TOKEN_TILE = 512
FF_TILE = TOKEN_TILE
VMEM_LIMIT = 56 * 1024 * 1024

_NT = (((1,), (1,)), ((), ()))
_TN = (((0,), (0,)), ((), ()))


def _params(*sem):
    return pltpu.CompilerParams(dimension_semantics=sem, vmem_limit_bytes=VMEM_LIMIT)


def _const_spec(shape):
    nd = len(shape)
    return pl.BlockSpec(shape, lambda *_: (0,) * nd, pipeline_mode=pl.Buffered(1))


def _dot(a, b):
    return jnp.dot(a, b, preferred_element_type=F32)


def _split_dot(x, m):
    hi = x.astype(BF16)
    lo = (x - hi.astype(F32)).astype(BF16)
    return _dot(hi, m) + _dot(lo, m)


def _sigmoid(x):
    return 1.0 / (1.0 + jnp.exp(-x))


def _mod_kernel(c_ref, w_ref, b_ref, o_ref):
    c = c_ref[...]
    s = (c * _sigmoid(c)).astype(BF16)
    o_ref[...] = _dot(s, w_ref[...].astype(BF16)) + b_ref[...]


def _modulation(cond, w_ada, b_ada):
    rows = cond.shape[0]
    n = w_ada.shape[1]
    tn = D_MODEL
    return pl.pallas_call(
        _mod_kernel,
        grid=(n // tn,),
        in_specs=[pl.BlockSpec((rows, D_MODEL), lambda j: (0, 0)),
                  pl.BlockSpec((D_MODEL, tn), lambda j: (0, j)),
                  pl.BlockSpec((1, tn), lambda j: (0, j))],
        out_specs=pl.BlockSpec((rows, tn), lambda j: (0, j)),
        out_shape=jax.ShapeDtypeStruct((rows, n), F32),
        compiler_params=_params("parallel"),
        name="modulation",
    )(cond, w_ada, b_ada.reshape(1, n))


def _swap16(x):
    w = x.shape[1]
    lane = lax.broadcasted_iota(jnp.int32, x.shape, 1)
    first = (lane % 32) < 16
    return jnp.where(first, pltpu.roll(x, w - 16, 1), pltpu.roll(x, 16, 1))


def _in_proj_kernel(*refs, rope):
    if rope:
        x_ref, mod_ref, g_ref, w_ref, cos_ref, sin_ref, q_ref, k_ref, v_ref, ga_ref, gb_ref, zr_ref = refs
    else:
        x_ref, mod_ref, g_ref, w_ref, q_ref, k_ref, v_ref, ga_ref, gb_ref, zr_ref = refs
    x = x_ref[...]
    m = mod_ref[0]
    ms = jnp.mean(x * x, axis=-1, keepdims=True)
    h = x * lax.rsqrt(ms + NORM_EPS) * g_ref[...]
    h = (h * (1.0 + m[1:2, :]) + m[0:1, :]).astype(BF16)

    q = _dot(h, w_ref[:, Q_OFF:K_OFF])
    k = _dot(h, w_ref[:, K_OFF:V_OFF])
    v = _dot(h, w_ref[:, V_OFF:GA_OFF])
    if rope:
        cos = cos_ref[...]
        sin = sin_ref[...]
        reps = ATTN_WIDTH // cos.shape[1]
        cos_q = jnp.concatenate([cos] * reps, axis=1)
        sin_q = jnp.concatenate([sin] * reps, axis=1)
        q = q * cos_q + _swap16(q) * sin_q
        k = k * cos + _swap16(k) * sin
    q_ref[...] = (q * (HEAD_DIM ** -0.5)).astype(q_ref.dtype)
    k_ref[...] = k.astype(k_ref.dtype)
    v_ref[...] = v.astype(v_ref.dtype)
    ga_ref[...] = _sigmoid(_dot(h, w_ref[:, GA_OFF:GB_OFF])).astype(ga_ref.dtype)
    gb_ref[...] = _sigmoid(_dot(h, w_ref[:, GB_OFF:ZR_OFF])).astype(gb_ref.dtype)
    zr_ref[...] = _dot(h, w_ref[:, ZR_OFF:W_IN_COLS]).astype(zr_ref.dtype)


def _in_proj(x, mod, g1, w_in, seq_len, mod_row, rope_tabs, kv_dtype):
    ntok = x.shape[0]
    tm = min(TOKEN_TILE, seq_len) if rope_tabs is not None else TOKEN_TILE
    tiles_per_seq = max(seq_len // tm, 1)
    rope = rope_tabs is not None
    tok = lambda width: pl.BlockSpec((tm, width), lambda i: (i, 0))
    in_specs = [tok(D_MODEL),
                pl.BlockSpec((1, 6, D_MODEL), lambda i: (mod_row(i), 0, 0)),
                _const_spec((1, D_MODEL)),
                _const_spec((D_MODEL, W_IN_COLS))]
    args = [x, mod, g1, w_in]
    if rope:
        in_specs += [pl.BlockSpec((tm, 2 * HEAD_DIM), lambda i: (i % tiles_per_seq, 0))] * 2
        args += list(rope_tabs)
    out_shape = [jax.ShapeDtypeStruct((ntok, ATTN_WIDTH), BF16),
                 jax.ShapeDtypeStruct((ntok, KV_WIDTH), kv_dtype),
                 jax.ShapeDtypeStruct((ntok, KV_WIDTH), kv_dtype),
                 jax.ShapeDtypeStruct((ntok, D_MODEL), BF16),
                 jax.ShapeDtypeStruct((ntok, D_MODEL), BF16),
                 jax.ShapeDtypeStruct((ntok, RW_SHIFT_COLS), BF16)]
    out_specs = [tok(ATTN_WIDTH), tok(KV_WIDTH), tok(KV_WIDTH), tok(D_MODEL), tok(D_MODEL), tok(RW_SHIFT_COLS)]
    return pl.pallas_call(
        functools.partial(_in_proj_kernel, rope=rope),
        grid=(ntok // tm,),
        in_specs=in_specs, out_specs=out_specs, out_shape=out_shape,
        compiler_params=_params("parallel"),
        name="in_proj_rope" if rope else "in_proj",
    )(*args)


def _softmax_pv(s, sink_col, v):
    m = jnp.maximum(jnp.max(s, axis=-1, keepdims=True), sink_col)
    p = jnp.exp(s - m)
    denom = jnp.sum(p, axis=-1, keepdims=True) + jnp.exp(sink_col - m)
    return _dot(p.astype(BF16), v) / denom


def _attend(problems, sink_ref):
    units = [list(range(u, u + ATTN_UNIT)) for u in range(0, N_Q_HEADS, ATTN_UNIT)]
    kv_of = lambda heads: slice((heads[0] // GQA_GROUP) * HEAD_DIM, (heads[0] // GQA_GROUP + 1) * HEAD_DIM)
    scores = []
    for q, k, _, mask in problems:
        for heads in units:
            qu = jnp.concatenate([q[:, h * HEAD_DIM:(h + 1) * HEAD_DIM] for h in heads], axis=0)
            scores.append(mask(lax.dot_general(qu, k[:, kv_of(heads)], _NT, preferred_element_type=F32)))
    results = []
    for pi, (q, _, v, _) in enumerate(problems):
        t = q.shape[0]
        outs = []
        for ui, heads in enumerate(units):
            sink_col = jnp.concatenate([jnp.full((t, 1), sink_ref[h], F32) for h in heads], axis=0)
            o = _softmax_pv(scores[pi * len(units) + ui], sink_col, v[:, kv_of(heads)])
            outs += [o[i * t:(i + 1) * t, :] for i in range(ATTN_UNIT)]
        results.append(jnp.concatenate(outs, axis=1))
    return results


def _attn_ctx_kernel(sink_ref, q_ref, k_ref, v_ref, o_ref):
    out, = _attend([(q_ref[0], k_ref[0].astype(BF16), v_ref[0].astype(BF16), lambda s: s)], sink_ref)
    o_ref[0] = out.astype(o_ref.dtype)


def _attn_ctx(q, k, v, sink):
    b, t, _ = q.shape
    seq = lambda width: pl.BlockSpec((1, t, width), lambda i: (i, 0, 0))
    return pl.pallas_call(
        _attn_ctx_kernel,
        grid=(b,),
        in_specs=[pl.BlockSpec(memory_space=pltpu.SMEM), seq(ATTN_WIDTH), seq(KV_WIDTH), seq(KV_WIDTH)],
        out_specs=seq(ATTN_WIDTH),
        out_shape=jax.ShapeDtypeStruct((b, t, ATTN_WIDTH), BF16),
        compiler_params=_params("parallel"),
        name="attn_ctx",
    )(sink, q, k, v)


def _attn_lat_kernel(sink_ref, q_ref, kp_ref, kc_ref, kn_ref, vp_ref, vc_ref, vn_ref, kx_ref, vx_ref, o_ref):
    m = pl.program_id(1)
    nb = ATTN_QBLOCKS * pl.num_programs(1)
    kc, vc = kc_ref[0], vc_ref[0]
    kblk = [kp_ref[0]] + [kc[s * BLOCK:(s + 1) * BLOCK, :] for s in range(ATTN_QBLOCKS)] + [kn_ref[0]]
    vblk = [vp_ref[0]] + [vc[s * BLOCK:(s + 1) * BLOCK, :] for s in range(ATTN_QBLOCKS)] + [vn_ref[0]]
    kx = kx_ref[0].astype(BF16)
    vx = vx_ref[0].astype(BF16)
    rows = ATTN_UNIT * BLOCK
    rq = lax.broadcasted_iota(jnp.int32, (rows, BLOCK), 0) % BLOCK
    col = lax.broadcasted_iota(jnp.int32, (rows, BLOCK), 1)

    def make_mask(n):
        keep_prev = col >= rq + jnp.where(n > 0, 0, BLOCK)
        keep_next = col <= rq - jnp.where(n < nb - 1, 0, BLOCK)

        def mask(s):
            return jnp.concatenate(
                [jnp.where(keep_prev, s[:, :BLOCK], MASK_VALUE), s[:, BLOCK:2 * BLOCK],
                 jnp.where(keep_next, s[:, 2 * BLOCK:3 * BLOCK], MASK_VALUE), s[:, 3 * BLOCK:]], axis=1)
        return mask

    problems = []
    for s in range(ATTN_QBLOCKS):
        k = jnp.concatenate(kblk[s:s + 3] + [kx], axis=0)
        v = jnp.concatenate(vblk[s:s + 3] + [vx], axis=0)
        problems.append((q_ref[0, s * BLOCK:(s + 1) * BLOCK, :], k, v, make_mask(m * ATTN_QBLOCKS + s)))
    outs = _attend(problems, sink_ref)
    o_ref[0] = jnp.concatenate(outs, axis=0).astype(o_ref.dtype)


def _attn_lat(q, k, v, k_ctx, v_ctx, sink):
    assert WINDOW == BLOCK, "the block-triangular band masks assume one block of reach on either side"
    b, t, _ = q.shape
    nb = t // BLOCK
    p = k_ctx.shape[1]
    qb = ATTN_QBLOCKS
    blk = lambda width, f: pl.BlockSpec((1, BLOCK, width), lambda i, m: (i, f(m), 0))
    cur = lambda width: pl.BlockSpec((1, qb * BLOCK, width), lambda i, m: (i, m, 0))
    prev = lambda m: jnp.maximum(m * qb - 1, 0)
    nxt = lambda m: jnp.minimum((m + 1) * qb, nb - 1)
    ctx = pl.BlockSpec((1, p, KV_WIDTH), lambda i, m: (i, 0, 0))
    return pl.pallas_call(
        _attn_lat_kernel,
        grid=(b, nb // qb),
        in_specs=[pl.BlockSpec(memory_space=pltpu.SMEM), cur(ATTN_WIDTH),
                  blk(KV_WIDTH, prev), cur(KV_WIDTH), blk(KV_WIDTH, nxt),
                  blk(KV_WIDTH, prev), cur(KV_WIDTH), blk(KV_WIDTH, nxt), ctx, ctx],
        out_specs=cur(ATTN_WIDTH),
        out_shape=jax.ShapeDtypeStruct((b, t, ATTN_WIDTH), BF16),
        compiler_params=_params("parallel", "parallel"),
        name="attn_lat",
    )(sink, q, k, k, k, v, v, v, k_ctx, v_ctx)


def _rw_prep_kernel(z_ref, zp_ref, zn_ref, mu_ref, w2_ref, w0_ref, a2_ref, a0_ref, g2_ref, kk_ref, ka_ref,
                    rk_ref, bd_ref, lw_ref, kd_ref, bb_ref, a_ref, r_ref, v_ref, g_ref, bonus_ref):
    i = pl.program_id(1)
    nt = pl.num_programs(1)
    z = z_ref[0].astype(F32)
    tm = z.shape[0]
    row = lax.broadcasted_iota(jnp.int32, z.shape, 0)
    halo_prev = jnp.where(i > 0, zp_ref[0].astype(F32)[BF16_ROWS - 1:BF16_ROWS, :], 0.0)
    halo_next = jnp.where(i < nt - 1, zn_ref[0].astype(F32)[0:1, :], 0.0)
    prev = jnp.where(row == 0, halo_prev, pltpu.roll(z, 1, 0))
    nxt = jnp.where(row == tm - 1, halo_next, pltpu.roll(z, tm - 1, 0))
    mu = mu_ref[...]
    z = z + mu[0:1, :] * (prev - z) + mu[1:2, :] * (nxt - z)

    r = z[:, 0:RWKV_WIDTH]
    k = z[:, RWKV_WIDTH:2 * RWKV_WIDTH]
    v = z[:, 2 * RWKV_WIDTH:3 * RWKV_WIDTH]
    zw = z[:, ZW_OFF:ZA_OFF]
    za = z[:, ZA_OFF:ZG_OFF]
    zg = z[:, ZG_OFF:RW_SHIFT_COLS]

    logit = w0_ref[...] + _dot(jnp.tanh(zw).astype(BF16), w2_ref[...])
    lw = -jnp.exp(F32(-0.5)) * _sigmoid(logit)
    a = _sigmoid(a0_ref[...] + _dot(za.astype(BF16), a2_ref[...]))
    g_ref[0] = _dot(_sigmoid(zg).astype(BF16), g2_ref[...]).astype(g_ref.dtype)

    bd = bd_ref[...]
    kk = k * kk_ref[...]
    kk = kk * lax.rsqrt(jnp.maximum(_split_dot(kk * kk, bd), 1e-24))
    ka = ka_ref[...]
    ksum = jnp.zeros_like(k)
    for d in range(N_DIR):
        ad = a[:, d * RWKV_WIDTH:(d + 1) * RWKV_WIDTH]
        kd = k * (1.0 + (ad - 1.0) * ka)
        lw_ref[d, 0] = lw[:, d * RWKV_WIDTH:(d + 1) * RWKV_WIDTH]
        kd_ref[d, 0] = kd.astype(kd_ref.dtype)
        bb_ref[d, 0] = (ad * kk).astype(bb_ref.dtype)
        ksum = ksum + kd
    a_ref[0] = (-kk).astype(a_ref.dtype)
    r_ref[0] = r.astype(r_ref.dtype)
    v_ref[0] = v.astype(v_ref.dtype)
    bonus_ref[0] = (_split_dot(r * ksum * rk_ref[...], bd) * v).astype(bonus_ref.dtype)


def _rw_prep(zr, p):
    b, t, _ = zr.shape
    tm = 256
    nt = t // tm
    hb = tm // BF16_ROWS
    last_hb = t // BF16_ROWS - 1
    tok = lambda: pl.BlockSpec((1, tm, RWKV_WIDTH), lambda i, j: (i, j, 0))
    tok2 = lambda: pl.BlockSpec((N_DIR, 1, tm, RWKV_WIDTH), lambda i, j: (0, i, j, 0))
    two = jax.ShapeDtypeStruct((N_DIR, b, t, RWKV_WIDTH), F32)
    one_bf = jax.ShapeDtypeStruct((b, t, RWKV_WIDTH), BF16)
    two_bf = jax.ShapeDtypeStruct((N_DIR, b, t, RWKV_WIDTH), BF16)
    in_specs = [pl.BlockSpec((1, tm, RW_SHIFT_COLS), lambda i, j: (i, j, 0)),
                pl.BlockSpec((1, BF16_ROWS, RW_SHIFT_COLS), lambda i, j: (i, jnp.maximum(j * hb - 1, 0), 0)),
                pl.BlockSpec((1, BF16_ROWS, RW_SHIFT_COLS), lambda i, j: (i, jnp.minimum((j + 1) * hb, last_hb), 0)),
                _const_spec((2, RW_SHIFT_COLS)),
                _const_spec((N_DIR * DECAY_RANK, N_DIR * RWKV_WIDTH)), _const_spec((1, N_DIR * RWKV_WIDTH)),
                _const_spec((N_DIR * ICLR_RANK, N_DIR * RWKV_WIDTH)), _const_spec((1, N_DIR * RWKV_WIDTH)),
                _const_spec((GATE_RANK, RWKV_WIDTH)),
                _const_spec((1, RWKV_WIDTH)), _const_spec((1, RWKV_WIDTH)), _const_spec((1, RWKV_WIDTH)),
                _const_spec((RWKV_WIDTH, RWKV_WIDTH))]
    return pl.pallas_call(
        _rw_prep_kernel,
        grid=(b, nt),
        in_specs=in_specs,
        out_specs=[tok2(), tok2(), tok2(), tok(), tok(), tok(), tok(), tok()],
        out_shape=[two, two_bf, two_bf, one_bf, one_bf, one_bf, one_bf, one_bf],
        compiler_params=_params("parallel", "parallel"),
        name="rw_prep",
    )(zr, zr, zr, p["mu"], p["w2"], p["w0"], p["a2"], p["a0"], p["g2"], p["k_k"], p["k_a"], p["r_k"], p["bd"])


def _rw_scan_kernel(*refs, has_s0, nchunks):
    if has_s0:
        s0_ref, refs = refs[0], refs[1:]
    (lw0_ref, kd0_ref, bb0_ref, a0_ref, r0_ref, v0_ref, lw1_ref, kd1_ref, bb1_ref, a1_ref, r1_ref, v1_ref,
     y0_ref, y1_ref, sf_ref, s_scr) = refs
    dir_refs = ((lw0_ref, kd0_ref, bb0_ref, a0_ref, r0_ref, v0_ref, y0_ref),
                (lw1_ref, kd1_ref, bb1_ref, a1_ref, r1_ref, v1_ref, y1_ref))
    i = pl.program_id(1)
    c = SCAN_CHUNK
    n = RWKV_HEAD_SIZE

    @pl.when(i == 0)
    def _():
        if has_s0:
            s_scr[...] = s0_ref[0]
        else:
            s_scr[...] = jnp.zeros_like(s_scr)

    diff = lax.broadcasted_iota(jnp.int32, (c, c), 0) - lax.broadcasted_iota(jnp.int32, (c, c), 1)
    incl = (diff >= 0, diff <= 0)
    strict = (diff > 0, diff < 0)
    diff2 = (lax.broadcasted_iota(jnp.int32, (c, 2 * c), 0) - lax.broadcasted_iota(jnp.int32, (c, 2 * c), 1) % c)
    incl2 = (diff2 >= 0, diff2 <= 0)
    chains = [(d, h) for d in range(N_DIR) for h in range(RWKV_HEADS)]
    nch = len(chains)
    dirs = [d for d, _ in chains]
    each = lambda f: [f(d, slice(h * n, (h + 1) * n)) for d, h in chains]

    def chunk(jj, carry):
        pre = []
        for d in range(N_DIR):
            lw_ref, kd_ref, bb_ref, a_ref, r_ref, v_ref, _ = dir_refs[d]
            j = jj if d == 0 else nchunks - 1 - jj
            rows = pl.ds(pl.multiple_of(j * c, c), c)
            lw = lw_ref[0, 0, rows, :]
            kd = kd_ref[0, 0, rows, :].astype(F32)
            bb = bb_ref[0, 0, rows, :].astype(F32)
            l1 = lw.astype(BF16)
            rem = lw - l1.astype(F32)
            l2 = rem.astype(BF16)
            l3 = (rem - l2.astype(F32)).astype(BF16)
            tri = incl[d].astype(BF16)
            cum = _dot(jnp.concatenate([tri, tri, tri], axis=1), jnp.concatenate([l1, l2, l3], axis=0))
            ctot = jnp.sum(lw, axis=0, keepdims=True)
            en = jnp.exp(-cum)
            eh = jnp.exp(ctot - cum)
            pre.append(dict(
                rows=rows,
                at=a_ref[0, rows, :].astype(F32) * jnp.exp(cum - lw),
                rt=r_ref[0, rows, :].astype(F32) * jnp.exp(cum),
                bt=(bb * en).astype(BF16), kt=(kd * en).astype(BF16),
                bh=(bb * eh).astype(BF16), kh=(kd * eh).astype(BF16),
                gam=jnp.exp(ctot), v=v_ref[0, rows, :]))

        at = each(lambda d, hs: pre[d]["at"][:, hs])
        rt = each(lambda d, hs: pre[d]["rt"][:, hs])
        vh = each(lambda d, hs: pre[d]["v"][:, hs])
        bh = each(lambda d, hs: pre[d]["bh"][:, hs])
        kh = each(lambda d, hs: pre[d]["kh"][:, hs])
        gam = each(lambda d, hs: pre[d]["gam"][:, hs])
        amat = each(lambda d, hs: lax.dot_general(
            jnp.concatenate([pre[d]["at"][:, hs], pre[d]["rt"][:, hs]], axis=0).astype(BF16),
            jnp.concatenate([pre[d]["bt"][:, hs], pre[d]["kt"][:, hs]], axis=0), _NT, preferred_element_type=F32))
        a_ak = [jnp.where(strict[dirs[q]], amat[q][:c, c:], 0.0).astype(BF16) for q in range(nch)]
        a_r = [jnp.where(incl2[dirs[q]], amat[q][c:, :], 0.0).astype(BF16) for q in range(nch)]
        x = [jnp.where(strict[dirs[q]], amat[q][:c, :c], 0.0) for q in range(nch)]
        z = [jnp.concatenate([at[q], _dot(a_ak[q], vh[q])], axis=1) for q in range(nch)]
        for step in range(NEUMANN_STEPS):
            xb = [x[q].astype(BF16) for q in range(nch)]
            if step < NEUMANN_STEPS - 1:
                prod = [_dot(xb[q], jnp.concatenate([z[q].astype(BF16), xb[q]], axis=1)) for q in range(nch)]
                z = [z[q] + prod[q][:, :2 * n] for q in range(nch)]
                x = [prod[q][:, 2 * n:] for q in range(nch)]
            else:
                z = [z[q] + _dot(xb[q], z[q].astype(BF16)) for q in range(nch)]
        zero = jnp.zeros((c, n), BF16)
        rmat = [jnp.concatenate([z[q].astype(BF16), jnp.concatenate([zero, vh[q]], axis=1)], axis=0)
                for q in range(nch)]
        qy = [_dot(a_r[q], rmat[q]) for q in range(nch)]
        qm = [(rt[q] + qy[q][:, :n]).astype(BF16) for q in range(nch)]
        mn = [lax.dot_general(rmat[q], jnp.concatenate([bh[q], kh[q]], axis=0), _TN, preferred_element_type=F32)
              for q in range(nch)]
        s = [s_scr[d, h] for d, h in chains]
        sb = [s[q].astype(BF16) for q in range(nch)]
        ys = [lax.dot_general(qm[q], sb[q], _NT, preferred_element_type=F32) + qy[q][:, n:] for q in range(nch)]
        for q, (d, h) in enumerate(chains):
            s_scr[d, h] = s[q] * gam[q] + _dot(sb[q], mn[q][:n, :].astype(BF16)) + mn[q][n:, :]
        for d in range(N_DIR):
            y_ref = dir_refs[d][-1]
            y_ref[0, pre[d]["rows"], :] = jnp.concatenate(ys[d * RWKV_HEADS:(d + 1) * RWKV_HEADS], axis=1)
        return carry

    lax.fori_loop(0, nchunks, chunk, 0, unroll=2)

    @pl.when(i == pl.num_programs(1) - 1)
    def _():
        sf_ref[0] = s_scr[...]


def _rw_scan(s0, lw, kd, bb, a, r, v):
    _, b, t, _ = lw.shape
    tb = min(t, 512)
    nblk = t // tb
    blk = (lambda i: i, lambda i: nblk - 1 - i)
    dir_spec = lambda d: pl.BlockSpec((1, 1, tb, RWKV_WIDTH), lambda bi, i: (d, bi, blk[d](i), 0))
    tok_spec = lambda d: pl.BlockSpec((1, tb, RWKV_WIDTH), lambda bi, i: (bi, blk[d](i), 0))
    st_spec = pl.BlockSpec((1, N_DIR, RWKV_HEADS, RWKV_HEAD_SIZE, RWKV_HEAD_SIZE), lambda bi, i: (bi, 0, 0, 0, 0))
    has_s0 = s0 is not None
    in_specs, args = [], []
    if has_s0:
        in_specs.append(st_spec)
        args.append(s0)
    for d in range(N_DIR):
        in_specs += [dir_spec(d), dir_spec(d), dir_spec(d), tok_spec(d), tok_spec(d), tok_spec(d)]
        args += [lw, kd, bb, a, r, v]
    y_shape = jax.ShapeDtypeStruct((b, t, RWKV_WIDTH), F32)
    return pl.pallas_call(
        functools.partial(_rw_scan_kernel, has_s0=has_s0, nchunks=tb // SCAN_CHUNK),
        grid=(b, nblk),
        in_specs=in_specs,
        out_specs=[tok_spec(0), tok_spec(1), st_spec],
        out_shape=[y_shape, y_shape,
                   jax.ShapeDtypeStruct((b, N_DIR, RWKV_HEADS, RWKV_HEAD_SIZE, RWKV_HEAD_SIZE), F32)],
        scratch_shapes=[pltpu.VMEM((N_DIR, RWKV_HEADS, RWKV_HEAD_SIZE, RWKV_HEAD_SIZE), F32)],
        compiler_params=_params("parallel", "arbitrary"),
        name="rw_scan",
    )(*args)


def _post_kernel(x_ref, mod_ref, attn_ref, y0_ref, y1_ref, bonus_ref, g_ref, ga_ref, gb_ref, lng_ref, lnb_ref,
                 bd_ref, wa_ref, wb_ref, wo_ref, o_ref):
    m = mod_ref[0]
    y = y0_ref[0] + y1_ref[0]
    bd = bd_ref[...]
    inv_n = 1.0 / RWKV_HEAD_SIZE
    mean = _split_dot(y, bd) * inv_n
    yc = y - mean
    var = _split_dot(yc * yc, bd) * inv_n
    yn = yc * lax.rsqrt(var + GN_EPS) * lng_ref[...] + lnb_ref[...]
    rw = ((yn + bonus_ref[0].astype(F32)) * g_ref[0].astype(F32)).astype(BF16)
    merged = (ga_ref[0].astype(F32) * _dot(attn_ref[0], wa_ref[...])
              + gb_ref[0].astype(F32) * _dot(rw, wb_ref[...]))
    o_ref[0] = x_ref[0] + m[2:3, :] * _dot(merged.astype(BF16), wo_ref[...])


def _post(x, mod, mod_row, attn, y0, y1, bonus, g, ga, gb, p):
    b, t, _ = x.shape
    tm = min(t, 512)
    tok = lambda width: pl.BlockSpec((1, tm, width), lambda i, j: (i, j, 0))
    in_specs = [tok(D_MODEL),
                pl.BlockSpec((1, 6, D_MODEL), lambda i, j: (mod_row(i), 0, 0)),
                tok(ATTN_WIDTH),
                tok(RWKV_WIDTH), tok(RWKV_WIDTH), tok(RWKV_WIDTH), tok(RWKV_WIDTH), tok(D_MODEL), tok(D_MODEL),
                _const_spec((1, RWKV_WIDTH)), _const_spec((1, RWKV_WIDTH)),
                _const_spec((RWKV_WIDTH, RWKV_WIDTH)),
                _const_spec((ATTN_WIDTH, D_MODEL)), _const_spec((RWKV_WIDTH, D_MODEL)),
                _const_spec((D_MODEL, D_MODEL))]
    return pl.pallas_call(
        _post_kernel,
        grid=(b, t // tm),
        in_specs=in_specs,
        out_specs=tok(D_MODEL),
        out_shape=jax.ShapeDtypeStruct((b, t, D_MODEL), F32),
        compiler_params=_params("parallel", "parallel"),
        name="post",
    )(x, mod, attn, y0, y1, bonus, g, ga, gb, p["ln_g"], p["ln_b"], p["bd"], p["w_proj_a"], p["w_proj_b"], p["w_out"])


def _ffn_kernel(x_ref, xp_ref, xn_ref, mod_ref, g2_ref, wup_ref, cw_ref, cb_ref, wdn_ref, gf_ref, o_ref, *, seq_len):
    i = pl.program_id(0)
    m = mod_ref[0]
    g2 = g2_ref[...]

    def norm_mod(x):
        ms = jnp.mean(x * x, axis=-1, keepdims=True)
        return x * lax.rsqrt(ms + NORM_EPS) * g2 * (1.0 + m[4:5, :]) + m[3:4, :]

    x = x_ref[...]
    tm = x.shape[0]
    hp = jnp.where((i * tm) % seq_len != 0, norm_mod(xp_ref[...]), 0.0)
    hn = jnp.where(((i + 1) * tm) % seq_len != 0, norm_mod(xn_ref[...]), 0.0)
    h = jnp.concatenate([hp, norm_mod(x), hn], axis=0).astype(BF16)
    cw = cw_ref[...]
    cb = cb_ref[...]
    halo = 2 * SUBLANES
    interleave = tm // FF_ROWS >= 4
    nrb = tm // FF_ROWS if interleave else 1
    blk_rows = tm // nrb

    def up_piece(j, r):
        lo = j * FF_CHUNK
        hr = h[r * blk_rows:(r + 1) * blk_rows + (halo if r == nrb - 1 else 0), :]
        return _dot(hr, wup_ref[:, lo:lo + FF_CHUNK]), _dot(hr, wup_ref[:, D_FF + lo:D_FF + lo + FF_CHUNK])

    def conv(pieces, half, r, lo):
        zeros = jnp.zeros((SUBLANES, FF_CHUNK), F32)
        u = pieces[r][half]
        if r > 0 and (r * blk_rows) % seq_len == 0:
            u = jnp.concatenate([zeros, u[SUBLANES:, :]], axis=0)
        if r < nrb - 1:
            nh = pieces[r + 1][half][:halo, :]
            if ((r + 1) * blk_rows) % seq_len == 0:
                nh = jnp.concatenate([nh[:SUBLANES, :], zeros], axis=0)
            u = jnp.concatenate([u, nh], axis=0)
        w = cw[:, lo:lo + FF_CHUNK]
        out = (pltpu.roll(u, 1, 0) * w[0:1, :] + u * w[1:2, :] + pltpu.roll(u, blk_rows + halo - 1, 0) * w[2:3, :]
               + cb[:, lo:lo + FF_CHUNK])
        return out[SUBLANES:SUBLANES + blk_rows, :]

    nchunks = D_FF // FF_CHUNK
    acc = [jnp.zeros((blk_rows, D_MODEL), F32) for _ in range(nrb)]
    cur = [up_piece(0, r) for r in range(nrb)]
    for j in range(nchunks):
        lo = j * FF_CHUNK
        more = j + 1 < nchunks
        nxt = [up_piece(j + 1, r) for r in range(nrb)] if (more and not interleave) else []
        for r in range(nrb):
            val = conv(cur, 0, r, lo)
            gate = conv(cur, 1, r, D_FF + lo)
            act = (gate * _sigmoid(gate) * val).astype(BF16)
            acc[r] = acc[r] + _dot(act, wdn_ref[lo:lo + FF_CHUNK, :])
            if more and interleave:
                nxt.append(up_piece(j + 1, r))
        cur = nxt
    x2 = x + m[5:6, :] * jnp.concatenate(acc, axis=0)
    ms = jnp.mean(x2 * x2, axis=-1, keepdims=True)
    o_ref[...] = x2 * lax.rsqrt(ms + NORM_EPS) * gf_ref[...]


def _ffn(x, mod, mod_row, p):
    b, t, _ = x.shape
    ntok = b * t
    tm = FF_TILE
    assert ntok % tm == 0 and (t % tm == 0 or (tm % t == 0 and t % FF_ROWS == 0 and tm // FF_ROWS >= 4))
    hb = tm // SUBLANES
    last_hb = ntok // SUBLANES - 1
    xf = x.reshape(ntok, D_MODEL)
    in_specs = [pl.BlockSpec((tm, D_MODEL), lambda i: (i, 0)),
                pl.BlockSpec((SUBLANES, D_MODEL), lambda i: (jnp.maximum(i * hb - 1, 0), 0)),
                pl.BlockSpec((SUBLANES, D_MODEL), lambda i: (jnp.minimum((i + 1) * hb, last_hb), 0)),
                pl.BlockSpec((1, 6, D_MODEL), lambda i: (mod_row(i), 0, 0)),
                _const_spec((1, D_MODEL)),
                _const_spec((D_MODEL, 2 * D_FF)), _const_spec((3, 2 * D_FF)), _const_spec((1, 2 * D_FF)),
                _const_spec((D_FF, D_MODEL)), _const_spec((1, D_MODEL))]
    out = pl.pallas_call(
        functools.partial(_ffn_kernel, seq_len=t),
        grid=(ntok // tm,),
        in_specs=in_specs,
        out_specs=pl.BlockSpec((tm, D_MODEL), lambda i: (i, 0)),
        out_shape=jax.ShapeDtypeStruct((ntok, D_MODEL), F32),
        compiler_params=_params("parallel"),
        name="ffn",
    )(xf, xf, xf, mod, p["g_norm2"], p["w_ffn_up"], p["conv_w"], p["conv_b"], p["w_ffn_down"], p["g_final"])
    return out.reshape(b, t, D_MODEL)


def _rope_tables(t):
    rows = t // GRID_W
    row = jnp.repeat(jnp.arange(rows), GRID_W).astype(F32)
    col = jnp.tile(jnp.arange(GRID_W), rows).astype(F32)
    freqs = ROPE_BASE ** (-jnp.arange(ROPE_PAIRS, dtype=F32) / ROPE_PAIRS)
    ar = row[:, None] * freqs
    ac = col[:, None] * freqs
    cos = jnp.concatenate([jnp.cos(ar), jnp.cos(ar), jnp.cos(ac), jnp.cos(ac)], axis=1)
    sin = jnp.concatenate([-jnp.sin(ar), jnp.sin(ar), -jnp.sin(ac), jnp.sin(ac)], axis=1)
    return jnp.tile(cos, (1, 2)), jnp.tile(sin, (1, 2))


def _block_diag2(w):
    z = jnp.zeros_like(w[0])
    return jnp.concatenate([jnp.concatenate([w[0], z], axis=1), jnp.concatenate([z, w[1]], axis=1)], axis=0)


def _trunk(x, mod, mod_row_tile, mod_row_seq, p, ctx):
    b, t, _ = x.shape
    rope = _rope_tables(t) if ctx is not None else None
    kv_dtype = BF16 if ctx is not None else F32
    q, k, v, ga, gb, zr = _in_proj(x.reshape(b * t, D_MODEL), mod, p["g_norm1"], p["w_in"], t,
                                   mod_row_tile, rope, kv_dtype)
    seq = lambda arr: arr.reshape(b, t, arr.shape[-1])
    q, k, v, ga, gb, zr = seq(q), seq(k), seq(v), seq(ga), seq(gb), seq(zr)
    if ctx is None:
        attn = _attn_ctx(q, k, v, p["sink"])
        s0 = None
    else:
        k_ctx, v_ctx, s0 = ctx
        attn = _attn_lat(q, k, v, k_ctx, v_ctx, p["sink"])
    lw, kd, bb, a, r, vr, g, bonus = _rw_prep(zr, p)
    y0, y1, s_final = _rw_scan(s0, lw, kd, bb, a, r, vr)
    x1 = _post(x, mod, mod_row_seq, attn, y0, y1, bonus, g, ga, gb, p)
    out = _ffn(x1, mod, mod_row_tile, p)
    return out, k, v, s_final


def kernel(x_prompt, x_sample, c, cache_k, cache_v, state_rwkv, c_ctx, w_ada, b_ada, g_norm1, w_in, attn_sink, w_proj_a, w_proj_b, rwkv_mu, rwkv_w0, rwkv_w2, rwkv_a0, rwkv_a2, rwkv_k_k, rwkv_k_a, rwkv_r_k, rwkv_g2, rwkv_ln_g, rwkv_ln_b, w_out, g_norm2, w_ffn_up, ffn_conv_w, ffn_conv_b, w_ffn_down, g_final):
    depth = w_ada.shape[0]
    assert depth == 1, "single trunk layer"
    l = 0
    nb, seq, _ = x_prompt.shape
    db, dseq, _ = x_sample.shape
    past = cache_k.shape[2]

    head_id = jnp.arange(RWKV_WIDTH) // RWKV_HEAD_SIZE
    p = dict(
        g_norm1=g_norm1[l].reshape(1, D_MODEL),
        w_in=w_in[l].astype(BF16),
        sink=attn_sink[l],
        mu=rwkv_mu[l],
        w2=_block_diag2(rwkv_w2[l]).astype(BF16),
        w0=rwkv_w0[l].reshape(1, N_DIR * RWKV_WIDTH),
        a2=_block_diag2(rwkv_a2[l]).astype(BF16),
        a0=rwkv_a0[l].reshape(1, N_DIR * RWKV_WIDTH),
        g2=rwkv_g2[l].astype(BF16),
        k_k=rwkv_k_k[l].reshape(1, RWKV_WIDTH),
        k_a=rwkv_k_a[l].reshape(1, RWKV_WIDTH),
        r_k=rwkv_r_k[l].reshape(1, RWKV_WIDTH),
        bd=(head_id[:, None] == head_id[None, :]).astype(BF16),
        ln_g=rwkv_ln_g[l].reshape(1, RWKV_WIDTH),
        ln_b=rwkv_ln_b[l].reshape(1, RWKV_WIDTH),
        w_proj_a=w_proj_a[l].astype(BF16),
        w_proj_b=w_proj_b[l].astype(BF16),
        w_out=w_out[l].astype(BF16),
        g_norm2=g_norm2[l].reshape(1, D_MODEL),
        w_ffn_up=w_ffn_up[l].astype(BF16),
        conv_w=ffn_conv_w[l],
        conv_b=ffn_conv_b[l].reshape(1, 2 * D_FF),
        w_ffn_down=w_ffn_down[l].astype(BF16),
        g_final=g_final.reshape(1, D_MODEL),
    )

    ctx_row = db
    mod_rows = 2 * SUBLANES
    cond = jnp.zeros((mod_rows, D_MODEL), F32).at[:db].set(c).at[ctx_row].set(c_ctx)
    mod = _modulation(cond, w_ada[l], b_ada[l]).reshape(mod_rows, 6, D_MODEL)

    y_prompt, kc, vc, sc = _trunk(x_prompt, mod, lambda i: ctx_row, lambda i: ctx_row, p, None)

    lat_tiles = max(dseq // TOKEN_TILE, 1)
    ctx = (cache_k[:, l].reshape(db, past, KV_WIDTH), cache_v[:, l].reshape(db, past, KV_WIDTH), state_rwkv[:, l])
    y_sample, _, _, _ = _trunk(x_sample, mod, lambda i: i // lat_tiles, lambda i: i, p, ctx)

    new_cache_k = kc.reshape(nb, 1, seq, N_KV_HEADS, HEAD_DIM)
    new_cache_v = vc.reshape(nb, 1, seq, N_KV_HEADS, HEAD_DIM)
    new_state = sc.reshape(nb, 1, N_DIR, RWKV_HEADS, RWKV_HEAD_SIZE, RWKV_HEAD_SIZE)
    return (y_prompt, y_sample, new_cache_k, new_cache_v, new_state)
```

```python
import functools

import jax
import jax.numpy as jnp
from jax import lax
from jax.experimental import pallas as pl
from jax.experimental.pallas import tpu as pltpu

F32 = jnp.float32
BF16 = jnp.bfloat16

D_MODEL = 1024
GRID_W = 64
HEAD_DIM = 64
N_Q_HEADS = 8
N_KV_HEADS = 2
GQA_GROUP = N_Q_HEADS // N_KV_HEADS
ATTN_WIDTH = N_Q_HEADS * HEAD_DIM
KV_WIDTH = N_KV_HEADS * HEAD_DIM
WINDOW = 128
BLOCK = 128
ROPE_BASE = 10000.0
ROPE_PAIRS = HEAD_DIM // 4
RWKV_HEADS = 8
RWKV_HEAD_SIZE = 64
RWKV_WIDTH = RWKV_HEADS * RWKV_HEAD_SIZE
N_DIR = 2
DECAY_RANK = 64
ICLR_RANK = 64
GATE_RANK = 128
RW_SHIFT_COLS = 3 * RWKV_WIDTH + N_DIR * DECAY_RANK + N_DIR * ICLR_RANK + GATE_RANK
D_FF = 2816
NORM_EPS = 1e-6
GN_EPS = 64e-5
MASK_VALUE = -1e30

Q_OFF = 0
K_OFF = ATTN_WIDTH
V_OFF = K_OFF + KV_WIDTH
GA_OFF = V_OFF + KV_WIDTH
GB_OFF = GA_OFF + D_MODEL
ZR_OFF = GB_OFF + D_MODEL
W_IN_COLS = ZR_OFF + RW_SHIFT_COLS

ZW_OFF = 3 * RWKV_WIDTH
ZA_OFF = ZW_OFF + N_DIR * DECAY_RANK
ZG_OFF = ZA_OFF + N_DIR * ICLR_RANK

SCAN_CHUNK = 64
SCAN_UNROLL = 4
NEUMANN_STEPS = 6
SUBLANES = 8
BF16_ROWS = 16
ATTN_UNIT = 2
ATTN_CTX_SEQS = 2
ATTN_QBLOCKS = 4
FF_CHUNK = 256
FF_ROWS = 128
TOKEN_TILE = 512
FF_TILE = TOKEN_TILE
VMEM_LIMIT = 56 * 1024 * 1024

_NT = (((1,), (1,)), ((), ()))
_TN = (((0,), (0,)), ((), ()))


def _params(*sem):
    return pltpu.CompilerParams(dimension_semantics=sem, vmem_limit_bytes=VMEM_LIMIT)


def _const_spec(shape):
    nd = len(shape)
    return pl.BlockSpec(shape, lambda *_: (0,) * nd, pipeline_mode=pl.Buffered(1))


def _dot(a, b):
    return jnp.dot(a, b, preferred_element_type=F32)


def _split_dot(x, m):
    hi = x.astype(BF16)
    lo = (x - hi.astype(F32)).astype(BF16)
    return _dot(hi, m) + _dot(lo, m)


def _sigmoid(x):
    return 1.0 / (1.0 + jnp.exp(-x))


def _mod_kernel(c_ref, w_ref, b_ref, o_ref):
    c = c_ref[...]
    s = (c * _sigmoid(c)).astype(BF16)
    o_ref[...] = _dot(s, w_ref[...].astype(BF16)) + b_ref[...]


def _modulation(cond, w_ada, b_ada):
    rows = cond.shape[0]
    n = w_ada.shape[1]
    tn = D_MODEL
    return pl.pallas_call(
        _mod_kernel,
        grid=(n // tn,),
        in_specs=[pl.BlockSpec((rows, D_MODEL), lambda j: (0, 0)),
                  pl.BlockSpec((D_MODEL, tn), lambda j: (0, j)),
                  pl.BlockSpec((1, tn), lambda j: (0, j))],
        out_specs=pl.BlockSpec((rows, tn), lambda j: (0, j)),
        out_shape=jax.ShapeDtypeStruct((rows, n), F32),
        compiler_params=_params("parallel"),
        name="modulation",
    )(cond, w_ada, b_ada.reshape(1, n))


def _swap16(x):
    w = x.shape[1]
    lane = lax.broadcasted_iota(jnp.int32, x.shape, 1)
    first = (lane % 32) < 16
    return jnp.where(first, pltpu.roll(x, w - 16, 1), pltpu.roll(x, 16, 1))


def _in_proj_kernel(*refs, rope):
    if rope:
        x_ref, mod_ref, g_ref, w_ref, cos_ref, sin_ref, q_ref, k_ref, v_ref, ga_ref, gb_ref, zr_ref = refs
    else:
        x_ref, mod_ref, g_ref, w_ref, q_ref, k_ref, v_ref, ga_ref, gb_ref, zr_ref = refs
    x = x_ref[...]
    m = mod_ref[0]
    ms = jnp.mean(x * x, axis=-1, keepdims=True)
    h = x * lax.rsqrt(ms + NORM_EPS) * g_ref[...]
    h = (h * (1.0 + m[1:2, :]) + m[0:1, :]).astype(BF16)

    q = _dot(h, w_ref[:, Q_OFF:K_OFF])
    k = _dot(h, w_ref[:, K_OFF:V_OFF])
    v = _dot(h, w_ref[:, V_OFF:GA_OFF])
    if rope:
        cos = cos_ref[...]
        sin = sin_ref[...]
        reps = ATTN_WIDTH // cos.shape[1]
        cos_q = jnp.concatenate([cos] * reps, axis=1)
        sin_q = jnp.concatenate([sin] * reps, axis=1)
        q = q * cos_q + _swap16(q) * sin_q
        k = k * cos + _swap16(k) * sin
    q_ref[...] = (q * (HEAD_DIM ** -0.5)).astype(q_ref.dtype)
    k_ref[...] = k.astype(k_ref.dtype)
    v_ref[...] = v.astype(v_ref.dtype)
    ga_ref[...] = _sigmoid(_dot(h, w_ref[:, GA_OFF:GB_OFF])).astype(ga_ref.dtype)
    gb_ref[...] = _sigmoid(_dot(h, w_ref[:, GB_OFF:ZR_OFF])).astype(gb_ref.dtype)
    zr_ref[...] = _dot(h, w_ref[:, ZR_OFF:W_IN_COLS]).astype(zr_ref.dtype)


def _in_proj(x, mod, g1, w_in, seq_len, mod_row, rope_tabs, kv_dtype):
    ntok = x.shape[0]
    tm = min(TOKEN_TILE, seq_len) if rope_tabs is not None else TOKEN_TILE
    tiles_per_seq = max(seq_len // tm, 1)
    rope = rope_tabs is not None
    tok = lambda width: pl.BlockSpec((tm, width), lambda i: (i, 0))
    in_specs = [tok(D_MODEL),
                pl.BlockSpec((1, 6, D_MODEL), lambda i: (mod_row(i), 0, 0)),
                _const_spec((1, D_MODEL)),
                _const_spec((D_MODEL, W_IN_COLS))]
    args = [x, mod, g1, w_in]
    if rope:
        in_specs += [pl.BlockSpec((tm, 2 * HEAD_DIM), lambda i: (i % tiles_per_seq, 0))] * 2
        args += list(rope_tabs)
    out_shape = [jax.ShapeDtypeStruct((ntok, ATTN_WIDTH), BF16),
                 jax.ShapeDtypeStruct((ntok, KV_WIDTH), kv_dtype),
                 jax.ShapeDtypeStruct((ntok, KV_WIDTH), kv_dtype),
                 jax.ShapeDtypeStruct((ntok, D_MODEL), BF16),
                 jax.ShapeDtypeStruct((ntok, D_MODEL), BF16),
                 jax.ShapeDtypeStruct((ntok, RW_SHIFT_COLS), BF16)]
    out_specs = [tok(ATTN_WIDTH), tok(KV_WIDTH), tok(KV_WIDTH), tok(D_MODEL), tok(D_MODEL), tok(RW_SHIFT_COLS)]
    return pl.pallas_call(
        functools.partial(_in_proj_kernel, rope=rope),
        grid=(ntok // tm,),
        in_specs=in_specs, out_specs=out_specs, out_shape=out_shape,
        compiler_params=_params("parallel"),
        name="in_proj_rope" if rope else "in_proj",
    )(*args)


def _softmax_pv(s, sink_col, v):
    m = jnp.maximum(jnp.max(s, axis=-1, keepdims=True), sink_col)
    p = jnp.exp(s - m)
    denom = jnp.sum(p, axis=-1, keepdims=True) + jnp.exp(sink_col - m)
    return _dot(p.astype(BF16), v) / denom


def _attend(problems, sink_ref):
    units = [list(range(u, u + ATTN_UNIT)) for u in range(0, N_Q_HEADS, ATTN_UNIT)]
    kv_of = lambda heads: slice((heads[0] // GQA_GROUP) * HEAD_DIM, (heads[0] // GQA_GROUP + 1) * HEAD_DIM)
    scores = []
    for q, k, _, mask in problems:
        for heads in units:
            qu = jnp.concatenate([q[:, h * HEAD_DIM:(h + 1) * HEAD_DIM] for h in heads], axis=0)
            scores.append(mask(lax.dot_general(qu, k[:, kv_of(heads)], _NT, preferred_element_type=F32)))
    results = []
    for pi, (q, _, v, _) in enumerate(problems):
        t = q.shape[0]
        outs = []
        for ui, heads in enumerate(units):
            sink_col = jnp.concatenate([jnp.full((t, 1), sink_ref[h], F32) for h in heads], axis=0)
            o = _softmax_pv(scores[pi * len(units) + ui], sink_col, v[:, kv_of(heads)])
            outs += [o[i * t:(i + 1) * t, :] for i in range(ATTN_UNIT)]
        results.append(jnp.concatenate(outs, axis=1))
    return results


def _attn_ctx_kernel(sink_ref, q_ref, k_ref, v_ref, o_ref):
    nseq = q_ref.shape[0]
    outs = _attend([(q_ref[s], k_ref[s].astype(BF16), v_ref[s].astype(BF16), lambda x: x) for s in range(nseq)],
                   sink_ref)
    for s in range(nseq):
        o_ref[s] = outs[s].astype(o_ref.dtype)


def _attn_ctx(q, k, v, sink):
    b, t, _ = q.shape
    nseq = ATTN_CTX_SEQS if b % ATTN_CTX_SEQS == 0 else 1
    seq = lambda width: pl.BlockSpec((nseq, t, width), lambda i: (i, 0, 0))
    return pl.pallas_call(
        _attn_ctx_kernel,
        grid=(b // nseq,),
        in_specs=[pl.BlockSpec(memory_space=pltpu.SMEM), seq(ATTN_WIDTH), seq(KV_WIDTH), seq(KV_WIDTH)],
        out_specs=seq(ATTN_WIDTH),
        out_shape=jax.ShapeDtypeStruct((b, t, ATTN_WIDTH), BF16),
        compiler_params=_params("parallel"),
        name="attn_ctx",
    )(sink, q, k, v)


def _attn_lat_kernel(sink_ref, q_ref, kp_ref, kc_ref, kn_ref, vp_ref, vc_ref, vn_ref, kx_ref, vx_ref, o_ref):
    m = pl.program_id(1)
    nb = ATTN_QBLOCKS * pl.num_programs(1)
    kc, vc = kc_ref[0], vc_ref[0]
    kblk = [kp_ref[0]] + [kc[s * BLOCK:(s + 1) * BLOCK, :] for s in range(ATTN_QBLOCKS)] + [kn_ref[0]]
    vblk = [vp_ref[0]] + [vc[s * BLOCK:(s + 1) * BLOCK, :] for s in range(ATTN_QBLOCKS)] + [vn_ref[0]]
    kx = kx_ref[0].astype(BF16)
    vx = vx_ref[0].astype(BF16)
    rows = ATTN_UNIT * BLOCK
    rq = lax.broadcasted_iota(jnp.int32, (rows, BLOCK), 0) % BLOCK
    col = lax.broadcasted_iota(jnp.int32, (rows, BLOCK), 1)

    def make_mask(n):
        keep_prev = col >= rq + jnp.where(n > 0, 0, BLOCK)
        keep_next = col <= rq - jnp.where(n < nb - 1, 0, BLOCK)

        def mask(s):
            return jnp.concatenate(
                [jnp.where(keep_prev, s[:, :BLOCK], MASK_VALUE), s[:, BLOCK:2 * BLOCK],
                 jnp.where(keep_next, s[:, 2 * BLOCK:3 * BLOCK], MASK_VALUE), s[:, 3 * BLOCK:]], axis=1)
        return mask

    problems = []
    for s in range(ATTN_QBLOCKS):
        k = jnp.concatenate(kblk[s:s + 3] + [kx], axis=0)
        v = jnp.concatenate(vblk[s:s + 3] + [vx], axis=0)
        problems.append((q_ref[0, s * BLOCK:(s + 1) * BLOCK, :], k, v, make_mask(m * ATTN_QBLOCKS + s)))
    outs = _attend(problems, sink_ref)
    o_ref[0] = jnp.concatenate(outs, axis=0).astype(o_ref.dtype)


def _attn_lat(q, k, v, k_ctx, v_ctx, sink):
    assert WINDOW == BLOCK, "the block-triangular band masks assume one block of reach on either side"
    b, t, _ = q.shape
    nb = t // BLOCK
    p = k_ctx.shape[1]
    qb = ATTN_QBLOCKS
    blk = lambda width, f: pl.BlockSpec((1, BLOCK, width), lambda i, m: (i, f(m), 0))
    cur = lambda width: pl.BlockSpec((1, qb * BLOCK, width), lambda i, m: (i, m, 0))
    prev = lambda m: jnp.maximum(m * qb - 1, 0)
    nxt = lambda m: jnp.minimum((m + 1) * qb, nb - 1)
    ctx = pl.BlockSpec((1, p, KV_WIDTH), lambda i, m: (i, 0, 0))
    return pl.pallas_call(
        _attn_lat_kernel,
        grid=(b, nb // qb),
        in_specs=[pl.BlockSpec(memory_space=pltpu.SMEM), cur(ATTN_WIDTH),
                  blk(KV_WIDTH, prev), cur(KV_WIDTH), blk(KV_WIDTH, nxt),
                  blk(KV_WIDTH, prev), cur(KV_WIDTH), blk(KV_WIDTH, nxt), ctx, ctx],
        out_specs=cur(ATTN_WIDTH),
        out_shape=jax.ShapeDtypeStruct((b, t, ATTN_WIDTH), BF16),
        compiler_params=_params("parallel", "parallel"),
        name="attn_lat",
    )(sink, q, k, k, k, v, v, v, k_ctx, v_ctx)


def _rw_prep_kernel(z_ref, zp_ref, zn_ref, mu_ref, w2_ref, w0_ref, a2_ref, a0_ref, g2_ref, kk_ref, ka_ref,
                    rk_ref, bd_ref, lw_ref, kd_ref, bb_ref, a_ref, r_ref, v_ref, g_ref, bonus_ref):
    i = pl.program_id(1)
    nt = pl.num_programs(1)
    z = z_ref[0].astype(F32)
    tm = z.shape[0]
    row = lax.broadcasted_iota(jnp.int32, z.shape, 0)
    halo_prev = jnp.where(i > 0, zp_ref[0].astype(F32)[BF16_ROWS - 1:BF16_ROWS, :], 0.0)
    halo_next = jnp.where(i < nt - 1, zn_ref[0].astype(F32)[0:1, :], 0.0)
    prev = jnp.where(row == 0, halo_prev, pltpu.roll(z, 1, 0))
    nxt = jnp.where(row == tm - 1, halo_next, pltpu.roll(z, tm - 1, 0))
    mu = mu_ref[...]
    z = z + mu[0:1, :] * (prev - z) + mu[1:2, :] * (nxt - z)

    r = z[:, 0:RWKV_WIDTH]
    k = z[:, RWKV_WIDTH:2 * RWKV_WIDTH]
    v = z[:, 2 * RWKV_WIDTH:3 * RWKV_WIDTH]
    zw = z[:, ZW_OFF:ZA_OFF]
    za = z[:, ZA_OFF:ZG_OFF]
    zg = z[:, ZG_OFF:RW_SHIFT_COLS]

    logit = w0_ref[...] + _dot(jnp.tanh(zw).astype(BF16), w2_ref[...])
    lw = -jnp.exp(F32(-0.5)) * _sigmoid(logit)
    a = _sigmoid(a0_ref[...] + _dot(za.astype(BF16), a2_ref[...]))
    g_ref[0] = _dot(_sigmoid(zg).astype(BF16), g2_ref[...]).astype(g_ref.dtype)

    bd = bd_ref[...]
    kk = k * kk_ref[...]
    kk = kk * lax.rsqrt(jnp.maximum(_split_dot(kk * kk, bd), 1e-24))
    ka = ka_ref[...]
    ksum = jnp.zeros_like(k)
    for d in range(N_DIR):
        ad = a[:, d * RWKV_WIDTH:(d + 1) * RWKV_WIDTH]
        kd = k * (1.0 + (ad - 1.0) * ka)
        lw_ref[d, 0] = lw[:, d * RWKV_WIDTH:(d + 1) * RWKV_WIDTH]
        kd_ref[d, 0] = kd.astype(kd_ref.dtype)
        bb_ref[d, 0] = (ad * kk).astype(bb_ref.dtype)
        ksum = ksum + kd
    a_ref[0] = (-kk).astype(a_ref.dtype)
    r_ref[0] = r.astype(r_ref.dtype)
    v_ref[0] = v.astype(v_ref.dtype)
    bonus_ref[0] = (_split_dot(r * ksum * rk_ref[...], bd) * v).astype(bonus_ref.dtype)


def _rw_prep(zr, p):
    b, t, _ = zr.shape
    tm = 256
    nt = t // tm
    hb = tm // BF16_ROWS
    last_hb = t // BF16_ROWS - 1
    tok = lambda: pl.BlockSpec((1, tm, RWKV_WIDTH), lambda i, j: (i, j, 0))
    tok2 = lambda: pl.BlockSpec((N_DIR, 1, tm, RWKV_WIDTH), lambda i, j: (0, i, j, 0))
    two = jax.ShapeDtypeStruct((N_DIR, b, t, RWKV_WIDTH), F32)
    one_bf = jax.ShapeDtypeStruct((b, t, RWKV_WIDTH), BF16)
    two_bf = jax.ShapeDtypeStruct((N_DIR, b, t, RWKV_WIDTH), BF16)
    in_specs = [pl.BlockSpec((1, tm, RW_SHIFT_COLS), lambda i, j: (i, j, 0)),
                pl.BlockSpec((1, BF16_ROWS, RW_SHIFT_COLS), lambda i, j: (i, jnp.maximum(j * hb - 1, 0), 0)),
                pl.BlockSpec((1, BF16_ROWS, RW_SHIFT_COLS), lambda i, j: (i, jnp.minimum((j + 1) * hb, last_hb), 0)),
                _const_spec((2, RW_SHIFT_COLS)),
                _const_spec((N_DIR * DECAY_RANK, N_DIR * RWKV_WIDTH)), _const_spec((1, N_DIR * RWKV_WIDTH)),
                _const_spec((N_DIR * ICLR_RANK, N_DIR * RWKV_WIDTH)), _const_spec((1, N_DIR * RWKV_WIDTH)),
                _const_spec((GATE_RANK, RWKV_WIDTH)),
                _const_spec((1, RWKV_WIDTH)), _const_spec((1, RWKV_WIDTH)), _const_spec((1, RWKV_WIDTH)),
                _const_spec((RWKV_WIDTH, RWKV_WIDTH))]
    return pl.pallas_call(
        _rw_prep_kernel,
        grid=(b, nt),
        in_specs=in_specs,
        out_specs=[tok2(), tok2(), tok2(), tok(), tok(), tok(), tok(), tok()],
        out_shape=[two, two_bf, two_bf, one_bf, one_bf, one_bf, one_bf, one_bf],
        compiler_params=_params("parallel", "parallel"),
        name="rw_prep",
    )(zr, zr, zr, p["mu"], p["w2"], p["w0"], p["a2"], p["a0"], p["g2"], p["k_k"], p["k_a"], p["r_k"], p["bd"])


def _rw_scan_kernel(*refs, has_s0, nchunks):
    if has_s0:
        s0_ref, refs = refs[0], refs[1:]
    (lw0_ref, kd0_ref, bb0_ref, a0_ref, r0_ref, v0_ref, lw1_ref, kd1_ref, bb1_ref, a1_ref, r1_ref, v1_ref,
     y0_ref, y1_ref, sf_ref, s_scr) = refs
    dir_refs = ((lw0_ref, kd0_ref, bb0_ref, a0_ref, r0_ref, v0_ref, y0_ref),
                (lw1_ref, kd1_ref, bb1_ref, a1_ref, r1_ref, v1_ref, y1_ref))
    i = pl.program_id(1)
    c = SCAN_CHUNK
    n = RWKV_HEAD_SIZE

    @pl.when(i == 0)
    def _():
        if has_s0:
            s_scr[...] = s0_ref[0]
        else:
            s_scr[...] = jnp.zeros_like(s_scr)

    diff = lax.broadcasted_iota(jnp.int32, (c, c), 0) - lax.broadcasted_iota(jnp.int32, (c, c), 1)
    incl = (diff >= 0, diff <= 0)
    strict = (diff > 0, diff < 0)
    diff2 = (lax.broadcasted_iota(jnp.int32, (c, 2 * c), 0) - lax.broadcasted_iota(jnp.int32, (c, 2 * c), 1) % c)
    incl2 = (diff2 >= 0, diff2 <= 0)
    chains = [(d, h) for d in range(N_DIR) for h in range(RWKV_HEADS)]
    nch = len(chains)
    dirs = [d for d, _ in chains]
    each = lambda f: [f(d, slice(h * n, (h + 1) * n)) for d, h in chains]

    def chunk(jj, carry):
        pre = []
        for d in range(N_DIR):
            lw_ref, kd_ref, bb_ref, a_ref, r_ref, v_ref, _ = dir_refs[d]
            j = jj if d == 0 else nchunks - 1 - jj
            rows = pl.ds(pl.multiple_of(j * c, c), c)
            lw = lw_ref[0, 0, rows, :]
            kd = kd_ref[0, 0, rows, :].astype(F32)
            bb = bb_ref[0, 0, rows, :].astype(F32)
            l1 = lw.astype(BF16)
            rem = lw - l1.astype(F32)
            l2 = rem.astype(BF16)
            l3 = (rem - l2.astype(F32)).astype(BF16)
            tri = incl[d].astype(BF16)
            cum = _dot(jnp.concatenate([tri, tri, tri], axis=1), jnp.concatenate([l1, l2, l3], axis=0))
            ctot = jnp.sum(lw, axis=0, keepdims=True)
            en = jnp.exp(-cum)
            eh = jnp.exp(ctot - cum)
            pre.append(dict(
                rows=rows,
                at=a_ref[0, rows, :].astype(F32) * jnp.exp(cum - lw),
                rt=r_ref[0, rows, :].astype(F32) * jnp.exp(cum),
                bt=(bb * en).astype(BF16), kt=(kd * en).astype(BF16),
                bh=(bb * eh).astype(BF16), kh=(kd * eh).astype(BF16),
                gam=jnp.exp(ctot), v=v_ref[0, rows, :]))

        at = each(lambda d, hs: pre[d]["at"][:, hs])
        rt = each(lambda d, hs: pre[d]["rt"][:, hs])
        vh = each(lambda d, hs: pre[d]["v"][:, hs])
        bh = each(lambda d, hs: pre[d]["bh"][:, hs])
        kh = each(lambda d, hs: pre[d]["kh"][:, hs])
        gam = each(lambda d, hs: pre[d]["gam"][:, hs])
        amat = each(lambda d, hs: lax.dot_general(
            jnp.concatenate([pre[d]["at"][:, hs], pre[d]["rt"][:, hs]], axis=0).astype(BF16),
            jnp.concatenate([pre[d]["bt"][:, hs], pre[d]["kt"][:, hs]], axis=0), _NT, preferred_element_type=F32))
        a_ak = [jnp.where(strict[dirs[q]], amat[q][:c, c:], 0.0).astype(BF16) for q in range(nch)]
        a_r = [jnp.where(incl2[dirs[q]], amat[q][c:, :], 0.0).astype(BF16) for q in range(nch)]
        x = [jnp.where(strict[dirs[q]], amat[q][:c, :c], 0.0) for q in range(nch)]
        z = [jnp.concatenate([at[q], _dot(a_ak[q], vh[q])], axis=1) for q in range(nch)]
        for step in range(NEUMANN_STEPS):
            xb = [x[q].astype(BF16) for q in range(nch)]
            if step < NEUMANN_STEPS - 1:
                prod = [_dot(xb[q], jnp.concatenate([z[q].astype(BF16), xb[q]], axis=1)) for q in range(nch)]
                z = [z[q] + prod[q][:, :2 * n] for q in range(nch)]
                x = [prod[q][:, 2 * n:] for q in range(nch)]
            else:
                z = [z[q] + _dot(xb[q], z[q].astype(BF16)) for q in range(nch)]
        zero = jnp.zeros((c, n), BF16)
        rmat = [jnp.concatenate([z[q].astype(BF16), jnp.concatenate([zero, vh[q]], axis=1)], axis=0)
                for q in range(nch)]
        qy = [_dot(a_r[q], rmat[q]) for q in range(nch)]
        qm = [(rt[q] + qy[q][:, :n]).astype(BF16) for q in range(nch)]
        mn = [lax.dot_general(rmat[q], jnp.concatenate([bh[q], kh[q]], axis=0), _TN, preferred_element_type=F32)
              for q in range(nch)]
        s = [s_scr[d, h] for d, h in chains]
        sb = [s[q].astype(BF16) for q in range(nch)]
        ys = [lax.dot_general(qm[q], sb[q], _NT, preferred_element_type=F32) + qy[q][:, n:] for q in range(nch)]
        for q, (d, h) in enumerate(chains):
            s_scr[d, h] = s[q] * gam[q] + _dot(sb[q], mn[q][:n, :].astype(BF16)) + mn[q][n:, :]
        for d in range(N_DIR):
            y_ref = dir_refs[d][-1]
            y_ref[0, pre[d]["rows"], :] = jnp.concatenate(ys[d * RWKV_HEADS:(d + 1) * RWKV_HEADS], axis=1)
        return carry

    lax.fori_loop(0, nchunks, chunk, 0, unroll=SCAN_UNROLL)

    @pl.when(i == pl.num_programs(1) - 1)
    def _():
        sf_ref[0] = s_scr[...]


def _rw_scan(s0, lw, kd, bb, a, r, v):
    _, b, t, _ = lw.shape
    tb = min(t, 512)
    nblk = t // tb
    blk = (lambda i: i, lambda i: nblk - 1 - i)
    dir_spec = lambda d: pl.BlockSpec((1, 1, tb, RWKV_WIDTH), lambda bi, i: (d, bi, blk[d](i), 0))
    tok_spec = lambda d: pl.BlockSpec((1, tb, RWKV_WIDTH), lambda bi, i: (bi, blk[d](i), 0))
    st_spec = pl.BlockSpec((1, N_DIR, RWKV_HEADS, RWKV_HEAD_SIZE, RWKV_HEAD_SIZE), lambda bi, i: (bi, 0, 0, 0, 0))
    has_s0 = s0 is not None
    in_specs, args = [], []
    if has_s0:
        in_specs.append(st_spec)
        args.append(s0)
    for d in range(N_DIR):
        in_specs += [dir_spec(d), dir_spec(d), dir_spec(d), tok_spec(d), tok_spec(d), tok_spec(d)]
        args += [lw, kd, bb, a, r, v]
    y_shape = jax.ShapeDtypeStruct((b, t, RWKV_WIDTH), F32)
    return pl.pallas_call(
        functools.partial(_rw_scan_kernel, has_s0=has_s0, nchunks=tb // SCAN_CHUNK),
        grid=(b, nblk),
        in_specs=in_specs,
        out_specs=[tok_spec(0), tok_spec(1), st_spec],
        out_shape=[y_shape, y_shape,
                   jax.ShapeDtypeStruct((b, N_DIR, RWKV_HEADS, RWKV_HEAD_SIZE, RWKV_HEAD_SIZE), F32)],
        scratch_shapes=[pltpu.VMEM((N_DIR, RWKV_HEADS, RWKV_HEAD_SIZE, RWKV_HEAD_SIZE), F32)],
        compiler_params=_params("parallel", "arbitrary"),
        name="rw_scan",
    )(*args)


def _post_kernel(x_ref, mod_ref, attn_ref, y0_ref, y1_ref, bonus_ref, g_ref, ga_ref, gb_ref, lng_ref, lnb_ref,
                 bd_ref, wa_ref, wb_ref, wo_ref, o_ref):
    m = mod_ref[0]
    y = y0_ref[0] + y1_ref[0]
    bd = bd_ref[...]
    inv_n = 1.0 / RWKV_HEAD_SIZE
    mean = _split_dot(y, bd) * inv_n
    yc = y - mean
    var = _split_dot(yc * yc, bd) * inv_n
    yn = yc * lax.rsqrt(var + GN_EPS) * lng_ref[...] + lnb_ref[...]
    rw = ((yn + bonus_ref[0].astype(F32)) * g_ref[0].astype(F32)).astype(BF16)
    merged = (ga_ref[0].astype(F32) * _dot(attn_ref[0], wa_ref[...])
              + gb_ref[0].astype(F32) * _dot(rw, wb_ref[...]))
    o_ref[0] = x_ref[0] + m[2:3, :] * _dot(merged.astype(BF16), wo_ref[...])


def _post(x, mod, mod_row, attn, y0, y1, bonus, g, ga, gb, p):
    b, t, _ = x.shape
    tm = min(t, 512)
    tok = lambda width: pl.BlockSpec((1, tm, width), lambda i, j: (i, j, 0))
    in_specs = [tok(D_MODEL),
                pl.BlockSpec((1, 6, D_MODEL), lambda i, j: (mod_row(i), 0, 0)),
                tok(ATTN_WIDTH),
                tok(RWKV_WIDTH), tok(RWKV_WIDTH), tok(RWKV_WIDTH), tok(RWKV_WIDTH), tok(D_MODEL), tok(D_MODEL),
                _const_spec((1, RWKV_WIDTH)), _const_spec((1, RWKV_WIDTH)),
                _const_spec((RWKV_WIDTH, RWKV_WIDTH)),
                _const_spec((ATTN_WIDTH, D_MODEL)), _const_spec((RWKV_WIDTH, D_MODEL)),
                _const_spec((D_MODEL, D_MODEL))]
    return pl.pallas_call(
        _post_kernel,
        grid=(b, t // tm),
        in_specs=in_specs,
        out_specs=tok(D_MODEL),
        out_shape=jax.ShapeDtypeStruct((b, t, D_MODEL), F32),
        compiler_params=_params("parallel", "parallel"),
        name="post",
    )(x, mod, attn, y0, y1, bonus, g, ga, gb, p["ln_g"], p["ln_b"], p["bd"], p["w_proj_a"], p["w_proj_b"], p["w_out"])


def _ffn_kernel(x_ref, xp_ref, xn_ref, mod_ref, g2_ref, wup_ref, cw_ref, cb_ref, wdn_ref, gf_ref, o_ref, *, seq_len):
    i = pl.program_id(0)
    m = mod_ref[0]
    g2 = g2_ref[...]

    def norm_mod(x):
        ms = jnp.mean(x * x, axis=-1, keepdims=True)
        return x * lax.rsqrt(ms + NORM_EPS) * g2 * (1.0 + m[4:5, :]) + m[3:4, :]

    x = x_ref[...]
    tm = x.shape[0]
    hp = jnp.where((i * tm) % seq_len != 0, norm_mod(xp_ref[...]), 0.0)
    hn = jnp.where(((i + 1) * tm) % seq_len != 0, norm_mod(xn_ref[...]), 0.0)
    h = jnp.concatenate([hp, norm_mod(x), hn], axis=0).astype(BF16)
    cw = cw_ref[...]
    cb = cb_ref[...]
    halo = 2 * SUBLANES
    interleave = tm // FF_ROWS >= 4
    nrb = tm // FF_ROWS if interleave else 1
    blk_rows = tm // nrb

    def up_piece(j, r):
        lo = j * FF_CHUNK
        hr = h[r * blk_rows:(r + 1) * blk_rows + (halo if r == nrb - 1 else 0), :]
        return _dot(hr, wup_ref[:, lo:lo + FF_CHUNK]), _dot(hr, wup_ref[:, D_FF + lo:D_FF + lo + FF_CHUNK])

    def conv(pieces, half, r, lo):
        zeros = jnp.zeros((SUBLANES, FF_CHUNK), F32)
        u = pieces[r][half]
        if r > 0 and (r * blk_rows) % seq_len == 0:
            u = jnp.concatenate([zeros, u[SUBLANES:, :]], axis=0)
        if r < nrb - 1:
            nh = pieces[r + 1][half][:halo, :]
            if ((r + 1) * blk_rows) % seq_len == 0:
                nh = jnp.concatenate([nh[:SUBLANES, :], zeros], axis=0)
            u = jnp.concatenate([u, nh], axis=0)
        w = cw[:, lo:lo + FF_CHUNK]
        out = (pltpu.roll(u, 1, 0) * w[0:1, :] + u * w[1:2, :] + pltpu.roll(u, blk_rows + halo - 1, 0) * w[2:3, :]
               + cb[:, lo:lo + FF_CHUNK])
        return out[SUBLANES:SUBLANES + blk_rows, :]

    nchunks = D_FF // FF_CHUNK
    acc = [jnp.zeros((blk_rows, D_MODEL), F32) for _ in range(nrb)]
    cur = [up_piece(0, r) for r in range(nrb)]
    for j in range(nchunks):
        lo = j * FF_CHUNK
        more = j + 1 < nchunks
        nxt = [up_piece(j + 1, r) for r in range(nrb)] if (more and not interleave) else []
        for r in range(nrb):
            val = conv(cur, 0, r, lo)
            gate = conv(cur, 1, r, D_FF + lo)
            act = (gate * _sigmoid(gate) * val).astype(BF16)
            acc[r] = acc[r] + _dot(act, wdn_ref[lo:lo + FF_CHUNK, :])
            if more and interleave:
                nxt.append(up_piece(j + 1, r))
        cur = nxt
    x2 = x + m[5:6, :] * jnp.concatenate(acc, axis=0)
    ms = jnp.mean(x2 * x2, axis=-1, keepdims=True)
    o_ref[...] = x2 * lax.rsqrt(ms + NORM_EPS) * gf_ref[...]


def _ffn(x, mod, mod_row, p):
    b, t, _ = x.shape
    ntok = b * t
    tm = FF_TILE
    assert ntok % tm == 0 and (t % tm == 0 or (tm % t == 0 and t % FF_ROWS == 0 and tm // FF_ROWS >= 4))
    hb = tm // SUBLANES
    last_hb = ntok // SUBLANES - 1
    xf = x.reshape(ntok, D_MODEL)
    in_specs = [pl.BlockSpec((tm, D_MODEL), lambda i: (i, 0)),
                pl.BlockSpec((SUBLANES, D_MODEL), lambda i: (jnp.maximum(i * hb - 1, 0), 0)),
                pl.BlockSpec((SUBLANES, D_MODEL), lambda i: (jnp.minimum((i + 1) * hb, last_hb), 0)),
                pl.BlockSpec((1, 6, D_MODEL), lambda i: (mod_row(i), 0, 0)),
                _const_spec((1, D_MODEL)),
                _const_spec((D_MODEL, 2 * D_FF)), _const_spec((3, 2 * D_FF)), _const_spec((1, 2 * D_FF)),
                _const_spec((D_FF, D_MODEL)), _const_spec((1, D_MODEL))]
    out = pl.pallas_call(
        functools.partial(_ffn_kernel, seq_len=t),
        grid=(ntok // tm,),
        in_specs=in_specs,
        out_specs=pl.BlockSpec((tm, D_MODEL), lambda i: (i, 0)),
        out_shape=jax.ShapeDtypeStruct((ntok, D_MODEL), F32),
        compiler_params=_params("parallel"),
        name="ffn",
    )(xf, xf, xf, mod, p["g_norm2"], p["w_ffn_up"], p["conv_w"], p["conv_b"], p["w_ffn_down"], p["g_final"])
    return out.reshape(b, t, D_MODEL)


def _rope_tables(t):
    rows = t // GRID_W
    row = jnp.repeat(jnp.arange(rows), GRID_W).astype(F32)
    col = jnp.tile(jnp.arange(GRID_W), rows).astype(F32)
    freqs = ROPE_BASE ** (-jnp.arange(ROPE_PAIRS, dtype=F32) / ROPE_PAIRS)
    ar = row[:, None] * freqs
    ac = col[:, None] * freqs
    cos = jnp.concatenate([jnp.cos(ar), jnp.cos(ar), jnp.cos(ac), jnp.cos(ac)], axis=1)
    sin = jnp.concatenate([-jnp.sin(ar), jnp.sin(ar), -jnp.sin(ac), jnp.sin(ac)], axis=1)
    return jnp.tile(cos, (1, 2)), jnp.tile(sin, (1, 2))


def _block_diag2(w):
    z = jnp.zeros_like(w[0])
    return jnp.concatenate([jnp.concatenate([w[0], z], axis=1), jnp.concatenate([z, w[1]], axis=1)], axis=0)


def _trunk(x, mod, mod_row_tile, mod_row_seq, p, ctx):
    b, t, _ = x.shape
    rope = _rope_tables(t) if ctx is not None else None
    kv_dtype = BF16 if ctx is not None else F32
    q, k, v, ga, gb, zr = _in_proj(x.reshape(b * t, D_MODEL), mod, p["g_norm1"], p["w_in"], t,
                                   mod_row_tile, rope, kv_dtype)
    seq = lambda arr: arr.reshape(b, t, arr.shape[-1])
    q, k, v, ga, gb, zr = seq(q), seq(k), seq(v), seq(ga), seq(gb), seq(zr)
    if ctx is None:
        attn = _attn_ctx(q, k, v, p["sink"])
        s0 = None
    else:
        k_ctx, v_ctx, s0 = ctx
        attn = _attn_lat(q, k, v, k_ctx, v_ctx, p["sink"])
    lw, kd, bb, a, r, vr, g, bonus = _rw_prep(zr, p)
    y0, y1, s_final = _rw_scan(s0, lw, kd, bb, a, r, vr)
    x1 = _post(x, mod, mod_row_seq, attn, y0, y1, bonus, g, ga, gb, p)
    out = _ffn(x1, mod, mod_row_tile, p)
    return out, k, v, s_final


def kernel(x_prompt, x_sample, c, cache_k, cache_v, state_rwkv, c_ctx, w_ada, b_ada, g_norm1, w_in, attn_sink, w_proj_a, w_proj_b, rwkv_mu, rwkv_w0, rwkv_w2, rwkv_a0, rwkv_a2, rwkv_k_k, rwkv_k_a, rwkv_r_k, rwkv_g2, rwkv_ln_g, rwkv_ln_b, w_out, g_norm2, w_ffn_up, ffn_conv_w, ffn_conv_b, w_ffn_down, g_final):
    depth = w_ada.shape[0]
    assert depth == 1, "single trunk layer"
    l = 0
    nb, seq, _ = x_prompt.shape
    db, dseq, _ = x_sample.shape
    past = cache_k.shape[2]

    head_id = jnp.arange(RWKV_WIDTH) // RWKV_HEAD_SIZE
    p = dict(
        g_norm1=g_norm1[l].reshape(1, D_MODEL),
        w_in=w_in[l].astype(BF16),
        sink=attn_sink[l],
        mu=rwkv_mu[l],
        w2=_block_diag2(rwkv_w2[l]).astype(BF16),
        w0=rwkv_w0[l].reshape(1, N_DIR * RWKV_WIDTH),
        a2=_block_diag2(rwkv_a2[l]).astype(BF16),
        a0=rwkv_a0[l].reshape(1, N_DIR * RWKV_WIDTH),
        g2=rwkv_g2[l].astype(BF16),
        k_k=rwkv_k_k[l].reshape(1, RWKV_WIDTH),
        k_a=rwkv_k_a[l].reshape(1, RWKV_WIDTH),
        r_k=rwkv_r_k[l].reshape(1, RWKV_WIDTH),
        bd=(head_id[:, None] == head_id[None, :]).astype(BF16),
        ln_g=rwkv_ln_g[l].reshape(1, RWKV_WIDTH),
        ln_b=rwkv_ln_b[l].reshape(1, RWKV_WIDTH),
        w_proj_a=w_proj_a[l].astype(BF16),
        w_proj_b=w_proj_b[l].astype(BF16),
        w_out=w_out[l].astype(BF16),
        g_norm2=g_norm2[l].reshape(1, D_MODEL),
        w_ffn_up=w_ffn_up[l].astype(BF16),
        conv_w=ffn_conv_w[l],
        conv_b=ffn_conv_b[l].reshape(1, 2 * D_FF),
        w_ffn_down=w_ffn_down[l].astype(BF16),
        g_final=g_final.reshape(1, D_MODEL),
    )

    ctx_row = db
    mod_rows = 2 * SUBLANES
    cond = jnp.zeros((mod_rows, D_MODEL), F32).at[:db].set(c).at[ctx_row].set(c_ctx)
    mod = _modulation(cond, w_ada[l], b_ada[l]).reshape(mod_rows, 6, D_MODEL)

    y_prompt, kc, vc, sc = _trunk(x_prompt, mod, lambda i: ctx_row, lambda i: ctx_row, p, None)

    lat_tiles = max(dseq // TOKEN_TILE, 1)
    ctx = (cache_k[:, l].reshape(db, past, KV_WIDTH), cache_v[:, l].reshape(db, past, KV_WIDTH), state_rwkv[:, l])
    y_sample, _, _, _ = _trunk(x_sample, mod, lambda i: i // lat_tiles, lambda i: i, p, ctx)

    new_cache_k = kc.reshape(nb, 1, seq, N_KV_HEADS, HEAD_DIM)
    new_cache_v = vc.reshape(nb, 1, seq, N_KV_HEADS, HEAD_DIM)
    new_state = sc.reshape(nb, 1, N_DIR, RWKV_HEADS, RWKV_HEAD_SIZE, RWKV_HEAD_SIZE)
    return (y_prompt, y_sample, new_cache_k, new_cache_v, new_state)
```
